```python
import math
import jax, jax.numpy as jnp
from jax import lax
import numpy as np

D_MODEL = 1024
BATCH = 8
SEQ = 16384
DEPTH = 4

D_FF = 2816
CONV_K = 4
SSM_EXPAND = 2
SSM_INNER = SSM_EXPAND * D_MODEL
SSM_HEAD_DIM = 64
SSM_HEADS = SSM_INNER // SSM_HEAD_DIM
SSM_GROUPS = 4
SSM_STATE = 128
SSM_CONV_DIM = SSM_INNER + 2 * SSM_GROUPS * SSM_STATE
SSM_CHUNK = 128
DN_HEADS = 8
DN_HEAD_K = 128
DN_HEAD_V = 128
DN_K_DIM = DN_HEADS * DN_HEAD_K
DN_V_DIM = DN_HEADS * DN_HEAD_V
DN_CONV_DIM = 2 * DN_K_DIM + DN_V_DIM
DN_CHUNK = 64
IN_SPLIT_SIZES = (SSM_INNER, SSM_CONV_DIM, SSM_HEADS,
                  DN_CONV_DIM, DN_V_DIM, DN_HEADS, DN_HEADS,
                  D_MODEL, D_MODEL)
IN_DIM = sum(IN_SPLIT_SIZES)
EPS = 1e-6

kernel_name = "hybrid_ssd_gdn_macaron_trunk"


def rms_norm(x, w):
    xf = x.astype(jnp.float32)
    y = xf * lax.rsqrt(jnp.mean(xf * xf, axis=-1, keepdims=True) + EPS)
    return (y * w.astype(jnp.float32)).astype(x.dtype)


def l2_norm(x):
    xf = x.astype(jnp.float32)
    return xf * lax.rsqrt(jnp.sum(xf * xf, axis=-1, keepdims=True) + EPS)


def swiglu(x, w_in, w_out):
    gate, up = jnp.split(x @ w_in, 2, axis=-1)
    return (jax.nn.silu(gate) * up) @ w_out


def causal_depthwise_conv(x, w, bias=None):
    K, C = w.shape
    y = lax.conv_general_dilated(
        x, w[:, None, :].astype(x.dtype), window_strides=(1,), padding=[(K - 1, 0)],
        dimension_numbers=("NWC", "WIO", "NWC"), feature_group_count=C)
    if bias is not None:
        y = y + bias.astype(x.dtype)
    return y


def ssd_chunked(x, dt, a, Bm, Cm):
    b, S, H, P = x.shape
    G, N = Bm.shape[2], Bm.shape[3]
    hg = H // G
    L = SSM_CHUNK
    nc = S // L
    f32 = jnp.float32
    xdt = (x.astype(f32) * dt[..., None]).reshape(b, nc, L, G, hg, P)
    log_a = (dt * a).reshape(b, nc, L, G, hg)
    Bc = Bm.astype(f32).reshape(b, nc, L, G, N)
    Cc = Cm.astype(f32).reshape(b, nc, L, G, N)
    a_cum = jnp.cumsum(log_a, axis=2)
    causal = jnp.tril(jnp.ones((L, L), dtype=bool))[None, None, :, :, None, None]
    seg = a_cum[:, :, :, None] - a_cum[:, :, None, :]
    decay = jnp.where(causal, jnp.exp(jnp.where(causal, seg, 0.0)), 0.0)
    cb = jnp.einsum("bclgn,bcsgn->bclsg", Cc, Bc)
    y_diag = jnp.einsum("bclsg,bclsgh,bcsghp->bclghp", cb, decay, xdt)
    decay_to_end = jnp.exp(a_cum[:, :, -1:] - a_cum)
    chunk_states = jnp.einsum("bclgn,bclgh,bclghp->bcghpn", Bc, decay_to_end, xdt)
    chunk_decay = jnp.exp(a_cum[:, :, -1])

    def step(state, inp):
        cs, cd = inp
        return state * cd[..., None, None] + cs, state

    s0 = jnp.zeros((b, G, hg, P, N), f32)
    _, states_in = lax.scan(step, s0, (jnp.moveaxis(chunk_states, 1, 0), jnp.moveaxis(chunk_decay, 1, 0)))
    states_in = jnp.moveaxis(states_in, 0, 1)
    y_off = jnp.einsum("bclgn,bcghpn,bclgh->bclghp", Cc, states_in, jnp.exp(a_cum))
    return (y_diag + y_off).reshape(b, S, H, P)


def gated_delta_rule_chunked(q, k, v, g, beta):
    b, S, H, dk = q.shape
    dv = v.shape[-1]
    C = DN_CHUNK
    n = S // C
    f32 = jnp.float32

    def chunks(t):
        return jnp.moveaxis(t.astype(f32).reshape(b, n, C, *t.shape[2:]), 2, 3)

    q, k, v, g, beta = (chunks(t) for t in (q, k, v, g, beta))
    G = jnp.cumsum(g, axis=-1)
    incl = jnp.tril(jnp.ones((C, C), dtype=bool))
    strict = jnp.tril(jnp.ones((C, C), dtype=bool), -1)
    diff = G[..., :, None] - G[..., None, :]
    decay = jnp.where(incl, jnp.exp(jnp.where(incl, diff, 0.0)), 0.0)
    kb = k * beta[..., None]
    M = jnp.where(strict, jnp.einsum("bnhid,bnhjd->bnhij", kb, k) * decay, 0.0)
    eye = jnp.eye(C, dtype=f32)
    rhs = jnp.concatenate([v * beta[..., None], kb * jnp.exp(G)[..., None]], axis=-1)
    sol = lax.linalg.triangular_solve(eye + M, rhs, left_side=True, lower=True, unit_diagonal=True)
    u, w = sol[..., :dv], sol[..., dv:]
    a_qk = jnp.einsum("bnhid,bnhjd->bnhij", q, k) * decay
    q_dec = q * jnp.exp(G)[..., None]
    g_last = G[..., -1]
    k_dec = k * jnp.exp(g_last[..., None] - G)[..., None]

    def step(state, inp):
        q_c, a_c, u_c, w_c, k_c, gl = inp
        v_new = u_c - jnp.einsum("bhck,bhkv->bhcv", w_c, state)
        o = jnp.einsum("bhck,bhkv->bhcv", q_c, state) + jnp.einsum("bhij,bhjv->bhiv", a_c, v_new)
        state = state * jnp.exp(gl)[..., None, None] + jnp.einsum("bhck,bhcv->bhkv", k_c, v_new)
        return state, o

    xs = tuple(jnp.moveaxis(t, 1, 0) for t in (q_dec, a_qk, u, w, k_dec, g_last))
    s0 = jnp.zeros((b, H, dk, dv), f32)
    _, o = lax.scan(step, s0, xs)
    o = jnp.moveaxis(jnp.moveaxis(o, 0, 1), 2, 3)
    return o.reshape(b, S, H, dv)


def mamba2_branch(z, xbc, dt_raw, conv_w, conv_b, dt_bias, a_log, d_skip, norm_w):
    b, S, _ = xbc.shape
    f32 = jnp.float32
    xbc = jax.nn.silu(causal_depthwise_conv(xbc, conv_w, conv_b))
    xs, Bm, Cm = jnp.split(xbc, [SSM_INNER, SSM_INNER + SSM_GROUPS * SSM_STATE], axis=-1)
    xs = xs.reshape(b, S, SSM_HEADS, SSM_HEAD_DIM)
    Bm = Bm.reshape(b, S, SSM_GROUPS, SSM_STATE)
    Cm = Cm.reshape(b, S, SSM_GROUPS, SSM_STATE)
    dt = jax.nn.softplus(dt_raw.astype(f32) + dt_bias.astype(f32))
    a = -jnp.exp(a_log.astype(f32))
    y = ssd_chunked(xs, dt, a, Bm, Cm) + xs.astype(f32) * d_skip.astype(f32)[:, None]
    yg = (y.reshape(b, S, SSM_INNER) * jax.nn.silu(z.astype(f32))).reshape(b, S, SSM_GROUPS, -1)
    yg = yg * lax.rsqrt(jnp.mean(yg * yg, axis=-1, keepdims=True) + EPS)
    return (yg.reshape(b, S, SSM_INNER) * norm_w.astype(f32)).astype(z.dtype)


def deltanet_branch(qkv, z, b_raw, a_raw, conv_w, dt_bias, a_log, norm_w):
    b, S, _ = qkv.shape
    f32 = jnp.float32
    qkv = jax.nn.silu(causal_depthwise_conv(qkv, conv_w))
    q, k, v = jnp.split(qkv, [DN_K_DIM, 2 * DN_K_DIM], axis=-1)
    q = l2_norm(q.reshape(b, S, DN_HEADS, DN_HEAD_K)) * (DN_HEAD_K ** -0.5)
    k = l2_norm(k.reshape(b, S, DN_HEADS, DN_HEAD_K))
    v = v.reshape(b, S, DN_HEADS, DN_HEAD_V)
    beta = jax.nn.sigmoid(b_raw.astype(f32))
    g = -jnp.exp(a_log.astype(f32)) * jax.nn.softplus(a_raw.astype(f32) + dt_bias.astype(f32))
    o = gated_delta_rule_chunked(q, k, v, g, beta)
    o = o * lax.rsqrt(jnp.mean(o * o, axis=-1, keepdims=True) + EPS) * norm_w.astype(f32)
    o = o * jax.nn.silu(z.astype(f32).reshape(b, S, DN_HEADS, DN_HEAD_V))
    return o.reshape(b, S, DN_V_DIM).astype(z.dtype)


def _fwd_setup_inputs(seed: int = 0) -> dict:
    key = jax.random.key(seed)
    k = jax.random.split(key, 23)
    f32 = jnp.float32

    def dense(kk, shape, fan_in):
        return jax.random.normal(kk, shape, f32) * fan_in ** -0.5

    def gain(kk, shape):
        return 1.0 + 0.02 * jax.random.normal(kk, shape, f32)

    def dt_bias(kk, shape):
        u = jax.random.uniform(kk, shape, f32)
        dt = jnp.exp(u * (math.log(0.1) - math.log(1e-3)) + math.log(1e-3))
        return dt + jnp.log(-jnp.expm1(-dt))

    def a_log(kk, shape):
        return jnp.log(jax.random.uniform(kk, shape, f32, 1.0, 16.0))

    L = DEPTH
    return {
        "x": jax.random.normal(k[0], (BATCH, SEQ, D_MODEL), f32),
        "ffn1_norm": gain(k[1], (L, D_MODEL)),
        "ffn1_w_in": dense(k[2], (L, D_MODEL, 2 * D_FF), D_MODEL),
        "ffn1_w_out": dense(k[3], (L, D_FF, D_MODEL), D_FF),
        "mix_norm": gain(k[4], (L, D_MODEL)),
        "w_in": dense(k[5], (L, D_MODEL, IN_DIM), D_MODEL),
        "ssm_conv_w": dense(k[6], (L, CONV_K, SSM_CONV_DIM), CONV_K),
        "ssm_conv_b": 0.02 * jax.random.normal(k[7], (L, SSM_CONV_DIM), f32),
        "ssm_dt_bias": dt_bias(k[8], (L, SSM_HEADS)),
        "ssm_a_log": a_log(k[9], (L, SSM_HEADS)),
        "ssm_d": 1.0 + 0.1 * jax.random.normal(k[10], (L, SSM_HEADS), f32),
        "ssm_norm": gain(k[11], (L, SSM_INNER)),
        "ssm_w_branch": dense(k[12], (L, SSM_INNER, D_MODEL), SSM_INNER),
        "dn_conv_w": dense(k[13], (L, CONV_K, DN_CONV_DIM), CONV_K),
        "dn_dt_bias": dt_bias(k[14], (L, DN_HEADS)),
        "dn_a_log": a_log(k[15], (L, DN_HEADS)),
        "dn_norm": gain(k[16], (L, DN_HEAD_V)),
        "dn_w_branch": dense(k[17], (L, DN_V_DIM, D_MODEL), DN_V_DIM),
        "w_out": dense(k[18], (L, D_MODEL, D_MODEL), D_MODEL),
        "ffn2_norm": gain(k[19], (L, D_MODEL)),
        "ffn2_w_in": dense(k[20], (L, D_MODEL, 2 * D_FF), D_MODEL),
        "ffn2_w_out": dense(k[21], (L, D_FF, D_MODEL), D_FF),
        "final_norm": gain(k[22], (D_MODEL,)),
    }


def _fwd_reference(x, ffn1_norm, ffn1_w_in, ffn1_w_out, mix_norm, w_in, ssm_conv_w, ssm_conv_b,
              ssm_dt_bias, ssm_a_log, ssm_d, ssm_norm, ssm_w_branch, dn_conv_w, dn_dt_bias,
              dn_a_log, dn_norm, dn_w_branch, w_out, ffn2_norm, ffn2_w_in, ffn2_w_out, final_norm):
    split_points = [int(p) for p in np.cumsum(IN_SPLIT_SIZES)[:-1]]
    h = x
    for l in range(DEPTH):
        h = h + 0.5 * swiglu(rms_norm(h, ffn1_norm[l]), ffn1_w_in[l], ffn1_w_out[l])
        u = rms_norm(h, mix_norm[l])
        (z_s, xbc, dt_s, qkv, z_d, b_d, a_d, gate_s, gate_d) = jnp.split(u @ w_in[l], split_points, axis=-1)
        y_s = mamba2_branch(z_s, xbc, dt_s, ssm_conv_w[l], ssm_conv_b[l], ssm_dt_bias[l],
                            ssm_a_log[l], ssm_d[l], ssm_norm[l])
        y_d = deltanet_branch(qkv, z_d, b_d, a_d, dn_conv_w[l], dn_dt_bias[l],
                              dn_a_log[l], dn_norm[l])
        merged = (jax.nn.sigmoid(gate_s) * (y_s @ ssm_w_branch[l])
                  + jax.nn.sigmoid(gate_d) * (y_d @ dn_w_branch[l]))
        h = h + merged @ w_out[l]
        h = h + 0.5 * swiglu(rms_norm(h, ffn2_norm[l]), ffn2_w_in[l], ffn2_w_out[l])
    return rms_norm(h, final_norm)


import jax as _jax
import jax.numpy as _jnp

TWIN_FORMAT = 'train_step'
FWD_PARAMS = ['x', 'ffn1_norm', 'ffn1_w_in', 'ffn1_w_out', 'mix_norm', 'w_in', 'ssm_conv_w', 'ssm_conv_b', 'ssm_dt_bias', 'ssm_a_log', 'ssm_d', 'ssm_norm', 'ssm_w_branch', 'dn_conv_w', 'dn_dt_bias', 'dn_a_log', 'dn_norm', 'dn_w_branch', 'w_out', 'ffn2_norm', 'ffn2_w_in', 'ffn2_w_out', 'final_norm']
TWIN_WEIGHTS = ['ffn1_norm', 'ffn1_w_in', 'ffn1_w_out', 'mix_norm', 'w_in', 'ssm_conv_w', 'ssm_conv_b', 'ssm_dt_bias', 'ssm_a_log', 'ssm_d', 'ssm_norm', 'ssm_w_branch', 'dn_conv_w', 'dn_dt_bias', 'dn_a_log', 'dn_norm', 'dn_w_branch', 'w_out', 'ffn2_norm', 'ffn2_w_in', 'ffn2_w_out', 'final_norm']
TWIN_DIFF_INPUT = 'x'
TWIN_INPUTS = ['x', 'ffn1_norm', 'ffn1_w_in', 'ffn1_w_out', 'mix_norm', 'w_in', 'ssm_conv_w', 'ssm_conv_b', 'ssm_dt_bias', 'ssm_a_log', 'ssm_d', 'ssm_norm', 'ssm_w_branch', 'dn_conv_w', 'dn_dt_bias', 'dn_a_log', 'dn_norm', 'dn_w_branch', 'w_out', 'ffn2_norm', 'ffn2_w_in', 'ffn2_w_out', 'final_norm', 'loss_target', 'm_ffn1_norm', 'm_ffn1_w_in', 'm_ffn1_w_out', 'm_mix_norm', 'm_w_in', 'm_ssm_conv_w', 'm_ssm_conv_b', 'm_ssm_dt_bias', 'm_ssm_a_log', 'm_ssm_d', 'm_ssm_norm', 'm_ssm_w_branch', 'm_dn_conv_w', 'm_dn_dt_bias', 'm_dn_a_log', 'm_dn_norm', 'm_dn_w_branch', 'm_w_out', 'm_ffn2_norm', 'm_ffn2_w_in', 'm_ffn2_w_out', 'm_final_norm', 'v_ffn1_norm', 'v_ffn1_w_in', 'v_ffn1_w_out', 'v_mix_norm', 'v_w_in', 'v_ssm_conv_w', 'v_ssm_conv_b', 'v_ssm_dt_bias', 'v_ssm_a_log', 'v_ssm_d', 'v_ssm_norm', 'v_ssm_w_branch', 'v_dn_conv_w', 'v_dn_dt_bias', 'v_dn_a_log', 'v_dn_norm', 'v_dn_w_branch', 'v_w_out', 'v_ffn2_norm', 'v_ffn2_w_in', 'v_ffn2_w_out', 'v_final_norm']
TWIN_OUTPUTS = ['loss', 'grad_x', 'grad_ffn1_norm', 'grad_ffn1_w_in', 'grad_ffn1_w_out', 'grad_mix_norm', 'grad_w_in', 'grad_ssm_conv_w', 'grad_ssm_conv_b', 'grad_ssm_dt_bias', 'grad_ssm_a_log', 'grad_ssm_d', 'grad_ssm_norm', 'grad_ssm_w_branch', 'grad_dn_conv_w', 'grad_dn_dt_bias', 'grad_dn_a_log', 'grad_dn_norm', 'grad_dn_w_branch', 'grad_w_out', 'grad_ffn2_norm', 'grad_ffn2_w_in', 'grad_ffn2_w_out', 'grad_final_norm', 'delta_ffn1_norm', 'delta_ffn1_w_in', 'delta_ffn1_w_out', 'delta_mix_norm', 'delta_w_in', 'delta_ssm_conv_w', 'delta_ssm_conv_b', 'delta_ssm_dt_bias', 'delta_ssm_a_log', 'delta_ssm_d', 'delta_ssm_norm', 'delta_ssm_w_branch', 'delta_dn_conv_w', 'delta_dn_dt_bias', 'delta_dn_a_log', 'delta_dn_norm', 'delta_dn_w_branch', 'delta_w_out', 'delta_ffn2_norm', 'delta_ffn2_w_in', 'delta_ffn2_w_out', 'delta_final_norm', 'new_m_ffn1_norm', 'new_m_ffn1_w_in', 'new_m_ffn1_w_out', 'new_m_mix_norm', 'new_m_w_in', 'new_m_ssm_conv_w', 'new_m_ssm_conv_b', 'new_m_ssm_dt_bias', 'new_m_ssm_a_log', 'new_m_ssm_d', 'new_m_ssm_norm', 'new_m_ssm_w_branch', 'new_m_dn_conv_w', 'new_m_dn_dt_bias', 'new_m_dn_a_log', 'new_m_dn_norm', 'new_m_dn_w_branch', 'new_m_w_out', 'new_m_ffn2_norm', 'new_m_ffn2_w_in', 'new_m_ffn2_w_out', 'new_m_final_norm', 'new_v_ffn1_norm', 'new_v_ffn1_w_in', 'new_v_ffn1_w_out', 'new_v_mix_norm', 'new_v_w_in', 'new_v_ssm_conv_w', 'new_v_ssm_conv_b', 'new_v_ssm_dt_bias', 'new_v_ssm_a_log', 'new_v_ssm_d', 'new_v_ssm_norm', 'new_v_ssm_w_branch', 'new_v_dn_conv_w', 'new_v_dn_dt_bias', 'new_v_dn_a_log', 'new_v_dn_norm', 'new_v_dn_w_branch', 'new_v_w_out', 'new_v_ffn2_norm', 'new_v_ffn2_w_in', 'new_v_ffn2_w_out', 'new_v_final_norm']
TWIN_LEAF_KINDS = {'loss': 'loss', 'grad_x': 'grad_x', 'grad_ffn1_norm': 'grad_w', 'grad_ffn1_w_in': 'grad_w', 'grad_ffn1_w_out': 'grad_w', 'grad_mix_norm': 'grad_w', 'grad_w_in': 'grad_w', 'grad_ssm_conv_w': 'grad_w', 'grad_ssm_conv_b': 'grad_w', 'grad_ssm_dt_bias': 'grad_w', 'grad_ssm_a_log': 'grad_w', 'grad_ssm_d': 'grad_w', 'grad_ssm_norm': 'grad_w', 'grad_ssm_w_branch': 'grad_w', 'grad_dn_conv_w': 'grad_w', 'grad_dn_dt_bias': 'grad_w', 'grad_dn_a_log': 'grad_w', 'grad_dn_norm': 'grad_w', 'grad_dn_w_branch': 'grad_w', 'grad_w_out': 'grad_w', 'grad_ffn2_norm': 'grad_w', 'grad_ffn2_w_in': 'grad_w', 'grad_ffn2_w_out': 'grad_w', 'grad_final_norm': 'grad_w', 'delta_ffn1_norm': 'delta_w', 'delta_ffn1_w_in': 'delta_w', 'delta_ffn1_w_out': 'delta_w', 'delta_mix_norm': 'delta_w', 'delta_w_in': 'delta_w', 'delta_ssm_conv_w': 'delta_w', 'delta_ssm_conv_b': 'delta_w', 'delta_ssm_dt_bias': 'delta_w', 'delta_ssm_a_log': 'delta_w', 'delta_ssm_d': 'delta_w', 'delta_ssm_norm': 'delta_w', 'delta_ssm_w_branch': 'delta_w', 'delta_dn_conv_w': 'delta_w', 'delta_dn_dt_bias': 'delta_w', 'delta_dn_a_log': 'delta_w', 'delta_dn_norm': 'delta_w', 'delta_dn_w_branch': 'delta_w', 'delta_w_out': 'delta_w', 'delta_ffn2_norm': 'delta_w', 'delta_ffn2_w_in': 'delta_w', 'delta_ffn2_w_out': 'delta_w', 'delta_final_norm': 'delta_w', 'new_m_ffn1_norm': 'new_m', 'new_m_ffn1_w_in': 'new_m', 'new_m_ffn1_w_out': 'new_m', 'new_m_mix_norm': 'new_m', 'new_m_w_in': 'new_m', 'new_m_ssm_conv_w': 'new_m', 'new_m_ssm_conv_b': 'new_m', 'new_m_ssm_dt_bias': 'new_m', 'new_m_ssm_a_log': 'new_m', 'new_m_ssm_d': 'new_m', 'new_m_ssm_norm': 'new_m', 'new_m_ssm_w_branch': 'new_m', 'new_m_dn_conv_w': 'new_m', 'new_m_dn_dt_bias': 'new_m', 'new_m_dn_a_log': 'new_m', 'new_m_dn_norm': 'new_m', 'new_m_dn_w_branch': 'new_m', 'new_m_w_out': 'new_m', 'new_m_ffn2_norm': 'new_m', 'new_m_ffn2_w_in': 'new_m', 'new_m_ffn2_w_out': 'new_m', 'new_m_final_norm': 'new_m', 'new_v_ffn1_norm': 'new_v', 'new_v_ffn1_w_in': 'new_v', 'new_v_ffn1_w_out': 'new_v', 'new_v_mix_norm': 'new_v', 'new_v_w_in': 'new_v', 'new_v_ssm_conv_w': 'new_v', 'new_v_ssm_conv_b': 'new_v', 'new_v_ssm_dt_bias': 'new_v', 'new_v_ssm_a_log': 'new_v', 'new_v_ssm_d': 'new_v', 'new_v_ssm_norm': 'new_v', 'new_v_ssm_w_branch': 'new_v', 'new_v_dn_conv_w': 'new_v', 'new_v_dn_dt_bias': 'new_v', 'new_v_dn_a_log': 'new_v', 'new_v_dn_norm': 'new_v', 'new_v_dn_w_branch': 'new_v', 'new_v_w_out': 'new_v', 'new_v_ffn2_norm': 'new_v', 'new_v_ffn2_w_in': 'new_v', 'new_v_ffn2_w_out': 'new_v', 'new_v_final_norm': 'new_v'}


def _forward(args):
    return _fwd_reference(*[args[k] for k in FWD_PARAMS])


def _output_shape():
    def fwd():
        inp = _fwd_setup_inputs(0)
        return _fwd_reference(*[inp[k] for k in FWD_PARAMS])
    out = _jax.eval_shape(fwd)
    return out.shape, out.dtype

N_MICROBATCH = 1
ADAM_LR = 0.001
ADAM_B1 = 0.9
ADAM_B2 = 0.999
ADAM_EPS = 1e-08
ADAM_WD = 0.01
ADAM_STEP = 10
PER_EXAMPLE_BATCH_AXIS = {'x': 0, 'loss_target': 0}
SHARED_INPUTS = []
_WEIGHT_DTYPES = {'ffn1_norm': _jnp.float32, 'ffn1_w_in': _jnp.float32, 'ffn1_w_out': _jnp.float32, 'mix_norm': _jnp.float32, 'w_in': _jnp.float32, 'ssm_conv_w': _jnp.float32, 'ssm_conv_b': _jnp.float32, 'ssm_dt_bias': _jnp.float32, 'ssm_a_log': _jnp.float32, 'ssm_d': _jnp.float32, 'ssm_norm': _jnp.float32, 'ssm_w_branch': _jnp.float32, 'dn_conv_w': _jnp.float32, 'dn_dt_bias': _jnp.float32, 'dn_a_log': _jnp.float32, 'dn_norm': _jnp.float32, 'dn_w_branch': _jnp.float32, 'w_out': _jnp.float32, 'ffn2_norm': _jnp.float32, 'ffn2_w_in': _jnp.float32, 'ffn2_w_out': _jnp.float32, 'final_norm': _jnp.float32}
MOMENT_SCALE = {'ffn1_norm': 1.943683e-01, 'ffn1_w_in': 7.818506e-02, 'ffn1_w_out': 1.274417e-01, 'mix_norm': 3.251236e-01, 'w_in': 9.807283e-02, 'ssm_conv_w': 1.088630e-01, 'ssm_conv_b': 1.559033e-01, 'ssm_dt_bias': 2.498232e-01, 'ssm_a_log': 3.988718e-01, 'ssm_d': 7.142102e-01, 'ssm_norm': 1.252680e-01, 'ssm_w_branch': 1.771808e-01, 'dn_conv_w': 7.834323e-02, 'dn_dt_bias': 3.738401e-01, 'dn_a_log': 3.809092e-01, 'dn_norm': 3.159295e-01, 'dn_w_branch': 1.033723e-01, 'w_out': 2.052423e-01, 'ffn2_norm': 1.371019e-01, 'ffn2_w_in': 5.694282e-02, 'ffn2_w_out': 9.288665e-02, 'final_norm': 1.280980e+02}


def _to_microbatches(a, axis):
    t = _jnp.moveaxis(a, axis, 0)
    t = t.reshape((N_MICROBATCH, t.shape[0] // N_MICROBATCH) + t.shape[1:])
    return _jnp.moveaxis(t, 1, axis + 1)


def setup_inputs(seed: int = 0) -> dict:
    inp = _fwd_setup_inputs(seed)
    key = _jax.random.fold_in(_jax.random.key(seed), 7919)
    shape, _ = _output_shape()
    out = dict(inp)
    out["loss_target"] = _jax.random.normal(_jax.random.fold_in(key, 0), shape, _jnp.float32)
    for i, name in enumerate(TWIN_WEIGHTS):
        w = inp[name].astype(_jnp.float32)
        if MOMENT_SCALE is None:
            s = _jnp.sqrt(_jnp.mean(_jnp.square(w)) + 1e-30)
        else:
            s = MOMENT_SCALE[name]
        km, kv = _jax.random.split(_jax.random.fold_in(key, i + 1))
        out[name] = w
        out["m_" + name] = s * _jax.random.normal(km, w.shape, _jnp.float32)
        out["v_" + name] = (s * s) * _jax.random.uniform(kv, w.shape, _jnp.float32, 0.5, 1.5)
    if N_MICROBATCH > 1:
        for name, axis in PER_EXAMPLE_BATCH_AXIS.items():
            out[name] = _to_microbatches(out[name], axis)
    return {'x': out['x'], 'ffn1_norm': out['ffn1_norm'], 'ffn1_w_in': out['ffn1_w_in'], 'ffn1_w_out': out['ffn1_w_out'], 'mix_norm': out['mix_norm'], 'w_in': out['w_in'], 'ssm_conv_w': out['ssm_conv_w'], 'ssm_conv_b': out['ssm_conv_b'], 'ssm_dt_bias': out['ssm_dt_bias'], 'ssm_a_log': out['ssm_a_log'], 'ssm_d': out['ssm_d'], 'ssm_norm': out['ssm_norm'], 'ssm_w_branch': out['ssm_w_branch'], 'dn_conv_w': out['dn_conv_w'], 'dn_dt_bias': out['dn_dt_bias'], 'dn_a_log': out['dn_a_log'], 'dn_norm': out['dn_norm'], 'dn_w_branch': out['dn_w_branch'], 'w_out': out['w_out'], 'ffn2_norm': out['ffn2_norm'], 'ffn2_w_in': out['ffn2_w_in'], 'ffn2_w_out': out['ffn2_w_out'], 'final_norm': out['final_norm'], 'loss_target': out['loss_target'], 'm_ffn1_norm': out['m_ffn1_norm'], 'm_ffn1_w_in': out['m_ffn1_w_in'], 'm_ffn1_w_out': out['m_ffn1_w_out'], 'm_mix_norm': out['m_mix_norm'], 'm_w_in': out['m_w_in'], 'm_ssm_conv_w': out['m_ssm_conv_w'], 'm_ssm_conv_b': out['m_ssm_conv_b'], 'm_ssm_dt_bias': out['m_ssm_dt_bias'], 'm_ssm_a_log': out['m_ssm_a_log'], 'm_ssm_d': out['m_ssm_d'], 'm_ssm_norm': out['m_ssm_norm'], 'm_ssm_w_branch': out['m_ssm_w_branch'], 'm_dn_conv_w': out['m_dn_conv_w'], 'm_dn_dt_bias': out['m_dn_dt_bias'], 'm_dn_a_log': out['m_dn_a_log'], 'm_dn_norm': out['m_dn_norm'], 'm_dn_w_branch': out['m_dn_w_branch'], 'm_w_out': out['m_w_out'], 'm_ffn2_norm': out['m_ffn2_norm'], 'm_ffn2_w_in': out['m_ffn2_w_in'], 'm_ffn2_w_out': out['m_ffn2_w_out'], 'm_final_norm': out['m_final_norm'], 'v_ffn1_norm': out['v_ffn1_norm'], 'v_ffn1_w_in': out['v_ffn1_w_in'], 'v_ffn1_w_out': out['v_ffn1_w_out'], 'v_mix_norm': out['v_mix_norm'], 'v_w_in': out['v_w_in'], 'v_ssm_conv_w': out['v_ssm_conv_w'], 'v_ssm_conv_b': out['v_ssm_conv_b'], 'v_ssm_dt_bias': out['v_ssm_dt_bias'], 'v_ssm_a_log': out['v_ssm_a_log'], 'v_ssm_d': out['v_ssm_d'], 'v_ssm_norm': out['v_ssm_norm'], 'v_ssm_w_branch': out['v_ssm_w_branch'], 'v_dn_conv_w': out['v_dn_conv_w'], 'v_dn_dt_bias': out['v_dn_dt_bias'], 'v_dn_a_log': out['v_dn_a_log'], 'v_dn_norm': out['v_dn_norm'], 'v_dn_w_branch': out['v_dn_w_branch'], 'v_w_out': out['v_w_out'], 'v_ffn2_norm': out['v_ffn2_norm'], 'v_ffn2_w_in': out['v_ffn2_w_in'], 'v_ffn2_w_out': out['v_ffn2_w_out'], 'v_final_norm': out['v_final_norm']}


def _loss(weights, diff, rest, loss_target):
    with _jax.named_scope("forward"):
        args = {**rest, TWIN_DIFF_INPUT: diff, **{k: w.astype(_WEIGHT_DTYPES[k]) for k, w in weights.items()}}
        y = _forward(args)
    with _jax.named_scope("loss_head"):
        err = _jnp.square(y.astype(_jnp.float32) - loss_target)
        return 0.5 * _jnp.sum(_jnp.mean(err, axis=-1)) if err.ndim else 0.5 * err


def _adamw(w, g, m, v):
    m = ADAM_B1 * m + (1.0 - ADAM_B1) * g
    v = ADAM_B2 * v + (1.0 - ADAM_B2) * _jnp.square(g)
    m_hat = m / (1.0 - ADAM_B1 ** ADAM_STEP)
    v_hat = v / (1.0 - ADAM_B2 ** ADAM_STEP)
    delta = -ADAM_LR * (m_hat / (_jnp.sqrt(v_hat) + ADAM_EPS) + ADAM_WD * w)
    return delta, m, v


def reference(x, ffn1_norm, ffn1_w_in, ffn1_w_out, mix_norm, w_in, ssm_conv_w, ssm_conv_b, ssm_dt_bias, ssm_a_log, ssm_d, ssm_norm, ssm_w_branch, dn_conv_w, dn_dt_bias, dn_a_log, dn_norm, dn_w_branch, w_out, ffn2_norm, ffn2_w_in, ffn2_w_out, final_norm, loss_target, m_ffn1_norm, m_ffn1_w_in, m_ffn1_w_out, m_mix_norm, m_w_in, m_ssm_conv_w, m_ssm_conv_b, m_ssm_dt_bias, m_ssm_a_log, m_ssm_d, m_ssm_norm, m_ssm_w_branch, m_dn_conv_w, m_dn_dt_bias, m_dn_a_log, m_dn_norm, m_dn_w_branch, m_w_out, m_ffn2_norm, m_ffn2_w_in, m_ffn2_w_out, m_final_norm, v_ffn1_norm, v_ffn1_w_in, v_ffn1_w_out, v_mix_norm, v_w_in, v_ssm_conv_w, v_ssm_conv_b, v_ssm_dt_bias, v_ssm_a_log, v_ssm_d, v_ssm_norm, v_ssm_w_branch, v_dn_conv_w, v_dn_dt_bias, v_dn_a_log, v_dn_norm, v_dn_w_branch, v_w_out, v_ffn2_norm, v_ffn2_w_in, v_ffn2_w_out, v_final_norm):
    given = dict(x=x, ffn1_norm=ffn1_norm, ffn1_w_in=ffn1_w_in, ffn1_w_out=ffn1_w_out, mix_norm=mix_norm, w_in=w_in, ssm_conv_w=ssm_conv_w, ssm_conv_b=ssm_conv_b, ssm_dt_bias=ssm_dt_bias, ssm_a_log=ssm_a_log, ssm_d=ssm_d, ssm_norm=ssm_norm, ssm_w_branch=ssm_w_branch, dn_conv_w=dn_conv_w, dn_dt_bias=dn_dt_bias, dn_a_log=dn_a_log, dn_norm=dn_norm, dn_w_branch=dn_w_branch, w_out=w_out, ffn2_norm=ffn2_norm, ffn2_w_in=ffn2_w_in, ffn2_w_out=ffn2_w_out, final_norm=final_norm, loss_target=loss_target, m_ffn1_norm=m_ffn1_norm, m_ffn1_w_in=m_ffn1_w_in, m_ffn1_w_out=m_ffn1_w_out, m_mix_norm=m_mix_norm, m_w_in=m_w_in, m_ssm_conv_w=m_ssm_conv_w, m_ssm_conv_b=m_ssm_conv_b, m_ssm_dt_bias=m_ssm_dt_bias, m_ssm_a_log=m_ssm_a_log, m_ssm_d=m_ssm_d, m_ssm_norm=m_ssm_norm, m_ssm_w_branch=m_ssm_w_branch, m_dn_conv_w=m_dn_conv_w, m_dn_dt_bias=m_dn_dt_bias, m_dn_a_log=m_dn_a_log, m_dn_norm=m_dn_norm, m_dn_w_branch=m_dn_w_branch, m_w_out=m_w_out, m_ffn2_norm=m_ffn2_norm, m_ffn2_w_in=m_ffn2_w_in, m_ffn2_w_out=m_ffn2_w_out, m_final_norm=m_final_norm, v_ffn1_norm=v_ffn1_norm, v_ffn1_w_in=v_ffn1_w_in, v_ffn1_w_out=v_ffn1_w_out, v_mix_norm=v_mix_norm, v_w_in=v_w_in, v_ssm_conv_w=v_ssm_conv_w, v_ssm_conv_b=v_ssm_conv_b, v_ssm_dt_bias=v_ssm_dt_bias, v_ssm_a_log=v_ssm_a_log, v_ssm_d=v_ssm_d, v_ssm_norm=v_ssm_norm, v_ssm_w_branch=v_ssm_w_branch, v_dn_conv_w=v_dn_conv_w, v_dn_dt_bias=v_dn_dt_bias, v_dn_a_log=v_dn_a_log, v_dn_norm=v_dn_norm, v_dn_w_branch=v_dn_w_branch, v_w_out=v_w_out, v_ffn2_norm=v_ffn2_norm, v_ffn2_w_in=v_ffn2_w_in, v_ffn2_w_out=v_ffn2_w_out, v_final_norm=v_final_norm)
    weights = {n: given[n] for n in TWIN_WEIGHTS}
    shared = {n: given[n] for n in SHARED_INPUTS}
    per_example = {n: given[n] for n in ['x']}
    grad_fn = _jax.value_and_grad(_loss, argnums=(0, 1))

    def one_microbatch(ex, loss_target):
        ex = dict(ex)
        diff = ex.pop(TWIN_DIFF_INPUT)
        return grad_fn(weights, diff, {**shared, **ex}, loss_target)

    if N_MICROBATCH == 1:
        loss, (grad_w, grad_x) = one_microbatch(per_example, given["loss_target"])
    else:
        def body(carry, xs):
            loss_sum, grad_sum = carry
            l_k, (gw_k, gx_k) = one_microbatch(xs[0], xs[1])
            with _jax.named_scope("update"):
                return (loss_sum + l_k, _jax.tree.map(_jnp.add, grad_sum, gw_k)), gx_k

        init = (_jnp.zeros((), _jnp.float32), _jax.tree.map(_jnp.zeros_like, weights))
        (loss, grad_w), grad_x = _jax.lax.scan(body, init, (per_example, given["loss_target"]))
    with _jax.named_scope("update"):
        delta_w, new_m, new_v = {}, {}, {}
        for n in TWIN_WEIGHTS:
            delta_w[n], new_m[n], new_v[n] = _adamw(weights[n], grad_w[n], given["m_" + n], given["v_" + n])
    return (loss, grad_x, *[grad_w[n] for n in TWIN_WEIGHTS], *[delta_w[n] for n in TWIN_WEIGHTS],
            *[new_m[n] for n in TWIN_WEIGHTS], *[new_v[n] for n in TWIN_WEIGHTS])
```

```python
import functools
import math

import jax
import jax.numpy as jnp
from jax import lax
from jax.experimental import pallas as pl
from jax.experimental.pallas import tpu as pltpu

F32, BF16 = jnp.float32, jnp.bfloat16
HIGHEST = lax.Precision.HIGHEST
MESH = pl.DeviceIdType.MESH

N_DEV = 8
EPS = 1e-6
CONV_K = 4
SSM_GROUPS = 4
SSM_CHUNK = 128
DN_CHUNK = 64
ADAM_LR, ADAM_B1, ADAM_B2, ADAM_EPS, ADAM_WD, ADAM_STEP = 0.001, 0.9, 0.999, 1e-08, 0.01, 10

V7X_VMEM_BYTES = 64 * 1024 * 1024
VMEM_LIMIT = 52 * 1024 * 1024
LANES = 128
SMALL_W = 256
CONV_HALO = 8


def _tile(n, target, quantum):
    if n <= target:
        return n
    t = (target // quantum) * quantum
    while t >= quantum:
        if n % t == 0:
            return t
        t -= quantum
    return n


def _cp(*sem):
    return pltpu.CompilerParams(dimension_semantics=sem, vmem_limit_bytes=VMEM_LIMIT)


def _softplus(x):
    return jnp.maximum(x, 0.0) + jnp.log1p(jnp.exp(-jnp.abs(x)))


def _silu(x):
    return x * jax.nn.sigmoid(x)


def _bdot(a, b, dims):
    return lax.dot_general(a.astype(BF16), b.astype(BF16), dims, preferred_element_type=F32)


_NN = (((1,), (0,)), ((), ()))
_NT = (((1,), (1,)), ((), ()))
_TN = (((0,), (0,)), ((), ()))
_MM_DIMS = {"nn": _NN, "nt": _NT, "tn": _TN}


def _matmul(a, b, *, mode, name, out_dtype=F32, res=None, scale=1.0):
    if mode == "nn":
        (m, k), (k2, n) = a.shape, b.shape
    elif mode == "nt":
        (m, k), (n, k2) = a.shape, b.shape
    else:
        (k, m), (k2, n) = a.shape, b.shape
    assert k == k2, (name, a.shape, b.shape)
    tm = _tile(m, 1024, LANES)
    tn = _tile(n, 1408, LANES)
    tk = _tile(k, 1408 if mode != "tn" else 1024, LANES)
    nk = k // tk
    dims = _MM_DIMS[mode]
    a_spec = pl.BlockSpec((tk, tm), lambda i, j, q: (q, i)) if mode == "tn" else pl.BlockSpec((tm, tk), lambda i, j, q: (i, q))
    b_spec = pl.BlockSpec((tn, tk), lambda i, j, q: (j, q)) if mode == "nt" else pl.BlockSpec((tk, tn), lambda i, j, q: (q, j))
    o_spec = pl.BlockSpec((tm, tn), lambda i, j, q: (i, j))
    has_res = res is not None

    def body(*refs):
        a_ref, b_ref = refs[0], refs[1]
        res_ref = refs[2] if has_res else None
        o_ref = refs[3 if has_res else 2]
        acc_ref = refs[-1] if nk > 1 else None
        part = _bdot(a_ref[...], b_ref[...], dims)

        def finish(acc):
            val = acc * scale if scale != 1.0 else acc
            if has_res:
                val = res_ref[...].astype(F32) + val
            o_ref[...] = val.astype(o_ref.dtype)

        if nk == 1:
            finish(part)
        else:
            q = pl.program_id(2)

            @pl.when(q == 0)
            def _():
                acc_ref[...] = part

            @pl.when(q > 0)
            def _():
                acc_ref[...] += part

            @pl.when(q == nk - 1)
            def _():
                finish(acc_ref[...])

    ins = [a, b] + ([res] if has_res else [])
    in_specs = [a_spec, b_spec] + ([o_spec] if has_res else [])
    return pl.pallas_call(
        body, name=name, grid=(m // tm, n // tn, nk), in_specs=in_specs, out_specs=o_spec,
        out_shape=jax.ShapeDtypeStruct((m, n), out_dtype),
        scratch_shapes=[pltpu.VMEM((tm, tn), F32)] if nk > 1 else [],
        compiler_params=_cp("parallel", "parallel", "arbitrary"),
    )(*ins)


def _read(ref, split):
    if split is None:
        return [ref[...].astype(F32)]
    out, off = [], 0
    for w in split:
        out.append(ref[:, off:off + w].astype(F32))
        off += w
    return out


def _write(ref, vals, split):
    if split is None:
        ref[...] = vals[0].astype(ref.dtype)
        return
    off = 0
    for w, v in zip(split, vals):
        ref[:, off:off + w] = v.astype(ref.dtype)
        off += w


def _rw_fwd(fn, name, grid, ins, outs):
    n_in = len(ins)

    def body(*refs):
        args = []
        for r, (_, _, split) in zip(refs[:n_in], ins):
            args += _read(r, split)
        vals = list(fn(*args))
        for r, (_, _, split) in zip(refs[n_in:], outs):
            n = 1 if split is None else len(split)
            _write(r, vals[:n], split)
            vals = vals[n:]

    return pl.pallas_call(
        body, name=name, grid=grid, in_specs=[s for _, s, _ in ins], out_specs=[s for _, s, _ in outs],
        out_shape=[o for o, _, _ in outs], compiler_params=_cp("parallel", "parallel"),
    )(*[a for a, _, _ in ins])


def _rw_bwd(fn, name, grid, ins, cts, grads, add=None, alias=None):
    n_in, n_ct = len(ins), len(cts)
    n_fixed = n_in + n_ct + (1 if add is not None else 0) + (1 if alias is not None else 0)
    arg_pos, pos = [], 0
    for _, _, split in ins:
        n = 1 if split is None else len(split)
        arg_pos.append((pos, n))
        pos += n

    def body(*refs):
        args = []
        for r, (_, _, split) in zip(refs[:n_in], ins):
            args += _read(r, split)
        ct_vals = []
        for r, (_, _, split) in zip(refs[n_in:n_in + n_ct], cts):
            ct_vals += _read(r, split)
        wrt = []
        for idx, _, _, _ in grads:
            p, n = arg_pos[idx]
            wrt += list(range(p, p + n))

        def f(*w):
            full = list(args)
            for p, v in zip(wrt, w):
                full[p] = v
            return tuple(fn(*full))

        _, vjp = jax.vjp(f, *[args[p] for p in wrt])
        g = list(vjp(tuple(ct_vals)))
        first_row = pl.program_id(1) == 0
        first_all = jnp.logical_and(pl.program_id(0) == 0, first_row)
        out_refs = refs[n_fixed:]
        for gi, (idx, _, _, mode) in enumerate(grads):
            n = arg_pos[idx][1]
            vals, g = g[:n], g[n:]
            split = ins[idx][2]
            o = out_refs[gi]
            if mode == "tile":
                if add is not None and add[2] == gi:
                    vals = [vals[0] + refs[n_in + n_ct][...].astype(F32)]
                _write(o, vals, split)
            else:
                first = first_row if mode == "acc_row" else first_all

                @pl.when(first)
                def _(o=o, vals=vals):
                    o[...] = vals[0]

                @pl.when(jnp.logical_not(first))
                def _(o=o, vals=vals):
                    o[...] += vals[0]

    arrays = [a for a, _, _ in ins] + [a for a, _, _ in cts]
    in_specs = [s for _, s, _ in ins] + [s for _, s, _ in cts]
    if add is not None:
        arrays.append(add[0])
        in_specs.append(add[1])
    aliases = {}
    if alias is not None:
        aliases = {len(arrays): alias[1]}
        arrays.append(alias[0])
        in_specs.append(pl.BlockSpec(memory_space=pl.ANY))
    return pl.pallas_call(
        body, name=name, grid=grid, in_specs=in_specs, out_specs=[s for _, _, s, _ in grads],
        out_shape=[o for _, o, _, _ in grads], input_output_aliases=aliases,
        compiler_params=_cp("arbitrary", "arbitrary"),
    )(*arrays)


def _rows(t, w, col=0):
    if callable(col):
        return pl.BlockSpec((t, w), lambda j, i: (i, col(j)))
    return pl.BlockSpec((t, w), lambda j, i: (i, col))


def _par(w, per_col=False):
    return pl.BlockSpec((1, w), (lambda j, i: (0, j)) if per_col else (lambda j, i: (0, 0)))


def _rms_fn(x, w):
    return (x * lax.rsqrt(jnp.mean(x * x, axis=-1, keepdims=True) + EPS) * w,)


def _swiglu_fn(gate, up):
    return (_silu(gate) * up,)


def _ssm_out_fn(y, z, w):
    yg = y * _silu(z)
    return (yg * lax.rsqrt(jnp.mean(yg * yg, axis=-1, keepdims=True) + EPS) * w,)


def _dn_out_fn(o, z, w):
    return (o * lax.rsqrt(jnp.mean(o * o, axis=-1, keepdims=True) + EPS) * w * _silu(z),)


def _merge_fn(gs, gd, ps, pd):
    return (jax.nn.sigmoid(gs) * ps + jax.nn.sigmoid(gd) * pd,)


def _add_fn(a, b):
    return (a + b,)


def _conv_fwd(x, x_col0, w, b, *, name, cw, tr):
    s, c = x.shape[0], w.shape[1]
    nr, ncol, hb = s // tr, c // cw, tr // CONV_HALO

    def body(x_ref, prev_ref, w_ref, b_ref, o_ref, buf):
        i = pl.program_id(1)
        buf[0:CONV_HALO, :] = jnp.where(i > 0, prev_ref[...], 0.0)
        buf[CONV_HALO:, :] = x_ref[...]
        acc = jnp.zeros((tr, cw), F32) + b_ref[...]
        for q in range(CONV_K):
            acc = acc + w_ref[q:q + 1, :] * buf[pl.ds(CONV_HALO - (CONV_K - 1) + q, tr), :]
        o_ref[...] = _silu(acc)

    return pl.pallas_call(
        body, name=name, grid=(ncol, nr),
        in_specs=[pl.BlockSpec((tr, cw), lambda j, i: (i, x_col0 + j)),
                  pl.BlockSpec((CONV_HALO, cw), lambda j, i: (jnp.maximum(i * hb - 1, 0), x_col0 + j)),
                  pl.BlockSpec((CONV_K, cw), lambda j, i: (0, j)), pl.BlockSpec((1, cw), lambda j, i: (0, j))],
        out_specs=pl.BlockSpec((tr, cw), lambda j, i: (i, j)),
        out_shape=jax.ShapeDtypeStruct((s, c), F32),
        scratch_shapes=[pltpu.VMEM((CONV_HALO + tr, cw), F32)],
        compiler_params=_cp("parallel", "parallel"),
    )(x, x, w, b)


def _conv_bwd(x, x_col0, w, b, dy, dproj, out_col0, *, name, cw, tr):
    s, c = dy.shape
    nr, ncol, hb = s // tr, c // cw, tr // CONV_HALO
    last_hb = s // CONV_HALO - 1
    ext = tr + CONV_HALO
    fresh = isinstance(dproj, jax.ShapeDtypeStruct)

    def body(x_ref, prev_ref, next_ref, dy_ref, dyn_ref, w_ref, b_ref, *rest):
        dx_ref, dw_ref, db_ref, xbuf, gbuf = rest[-5:]
        i = pl.program_id(1)
        xbuf[0:CONV_HALO, :] = jnp.where(i > 0, prev_ref[...], 0.0)
        xbuf[CONV_HALO:CONV_HALO + tr, :] = x_ref[...]
        xbuf[CONV_HALO + tr:, :] = next_ref[...]
        pre = jnp.zeros((ext, cw), F32) + b_ref[...]
        for q in range(CONV_K):
            pre = pre + w_ref[q:q + 1, :] * xbuf[pl.ds(CONV_HALO - (CONV_K - 1) + q, ext), :]
        sg = jax.nn.sigmoid(pre)
        dsilu = sg * (1.0 + pre * (1.0 - sg))
        gbuf[0:tr, :] = dy_ref[...].astype(F32) * dsilu[0:tr, :]
        gbuf[tr:, :] = jnp.where(i < nr - 1, dyn_ref[...].astype(F32), 0.0) * dsilu[tr:, :]
        dx = jnp.zeros((tr, cw), F32)
        for q in range(CONV_K):
            dx = dx + w_ref[q:q + 1, :] * gbuf[pl.ds(CONV_K - 1 - q, tr), :]
        dx_ref[...] = dx.astype(dx_ref.dtype)
        g_own = gbuf[0:tr, :]
        dws = [jnp.sum(g_own * xbuf[pl.ds(CONV_HALO - (CONV_K - 1) + q, tr), :], axis=0, keepdims=True) for q in range(CONV_K)]
        dbv = jnp.sum(g_own, axis=0, keepdims=True)

        @pl.when(i == 0)
        def _():
            for q in range(CONV_K):
                dw_ref[q:q + 1, :] = dws[q]
            db_ref[...] = dbv

        @pl.when(i > 0)
        def _():
            for q in range(CONV_K):
                dw_ref[q:q + 1, :] += dws[q]
            db_ref[...] += dbv

    xmap = lambda j, i: (i, x_col0 + j)
    ins = [x, x, x, dy, dy, w, b]
    in_specs = [pl.BlockSpec((tr, cw), xmap),
                pl.BlockSpec((CONV_HALO, cw), lambda j, i: (jnp.maximum(i * hb - 1, 0), x_col0 + j)),
                pl.BlockSpec((CONV_HALO, cw), lambda j, i: (jnp.minimum((i + 1) * hb, last_hb), x_col0 + j)),
                pl.BlockSpec((tr, cw), lambda j, i: (i, j)),
                pl.BlockSpec((CONV_HALO, cw), lambda j, i: (jnp.minimum((i + 1) * hb, last_hb), j)),
                pl.BlockSpec((CONV_K, cw), lambda j, i: (0, j)), pl.BlockSpec((1, cw), lambda j, i: (0, j))]
    aliases = {}
    if not fresh:
        aliases = {len(ins): 0}
        ins.append(dproj)
        in_specs.append(pl.BlockSpec(memory_space=pl.ANY))
    return pl.pallas_call(
        body, name=name, grid=(ncol, nr), in_specs=in_specs,
        out_specs=[pl.BlockSpec((tr, cw), lambda j, i: (i, out_col0 + j)),
                   pl.BlockSpec((CONV_K, cw), lambda j, i: (0, j)), pl.BlockSpec((1, cw), lambda j, i: (0, j))],
        out_shape=[jax.ShapeDtypeStruct(dproj.shape, dproj.dtype), jax.ShapeDtypeStruct((CONV_K, c), F32),
                   jax.ShapeDtypeStruct((1, c), F32)],
        scratch_shapes=[pltpu.VMEM((CONV_HALO + ext, cw), F32), pltpu.VMEM((ext, cw), F32)],
        input_output_aliases=aliases, compiler_params=_cp("arbitrary", "arbitrary"),
    )(*ins)


def _ssd_chunk(xs, bm, cm, small, dtb, alog, dsk, st, g, *, hg, p):
    l, hp = xs.shape
    w = small.shape[1]
    lane_w = lax.broadcasted_iota(jnp.int32, (1, w), 1)
    lane_hp = lax.broadcasted_iota(jnp.int32, (1, hp), 1)
    r = lax.broadcasted_iota(jnp.int32, (l, l), 0)
    c = lax.broadcasted_iota(jnp.int32, (l, l), 1)
    tri = r >= c
    eye = (r == c).astype(F32)
    dt_all = _softplus(small + dtb)
    acum_all = jnp.dot(tri.astype(F32), dt_all * (-jnp.exp(alog)), precision=HIGHEST, preferred_element_type=F32)
    cb = _bdot(cm, bm, _NT)
    dt_exp = jnp.zeros((l, hp), F32)
    acum_exp = jnp.zeros((l, hp), F32)
    d_exp = jnp.zeros((1, hp), F32)
    decays, masks = [], []
    for j in range(hg):
        mh = (lane_w == g * hg + j).astype(F32)
        mj = jnp.logical_and(lane_hp >= j * p, lane_hp < (j + 1) * p).astype(F32)
        ac = jnp.sum(acum_all * mh, axis=1, keepdims=True)
        dt_exp = dt_exp + jnp.sum(dt_all * mh, axis=1, keepdims=True) * mj
        acum_exp = acum_exp + ac * mj
        d_exp = d_exp + jnp.sum(dsk * mh, axis=1, keepdims=True) * mj
        seg = ac - jnp.sum(ac * eye, axis=0, keepdims=True)
        decays.append(jnp.where(tri, jnp.exp(jnp.where(tri, seg, 0.0)), 0.0))
        masks.append(mj)
    xdt = xs * dt_exp
    y = xs * d_exp
    for dec, mj in zip(decays, masks):
        y = y + _bdot(cb * dec, xdt * mj, _NN)
    last = (lax.broadcasted_iota(jnp.int32, (l, 1), 0) == l - 1).astype(F32)
    a_last = jnp.sum(acum_exp * last, axis=0, keepdims=True)
    y = y + jnp.exp(acum_exp) * _bdot(cm, st, _NN)
    st_new = st * jnp.exp(a_last) + _bdot(bm, xdt * jnp.exp(a_last - acum_exp), _TN)
    return y, st_new


def _ssd_specs(cfg, rev):
    l, hp, n, g = SSM_CHUNK, cfg["hp"], cfg["n"], SSM_GROUPS
    nc = cfg["s"] // l
    cc = (lambda c: nc - 1 - c) if rev else (lambda c: c)
    nb = cfg["inner"] // n
    specs = [pl.BlockSpec((l, hp), lambda c, q: (cc(c), q)),
             pl.BlockSpec((l, n), lambda c, q: (cc(c), nb + q)),
             pl.BlockSpec((l, n), lambda c, q: (cc(c), nb + g + q)),
             pl.BlockSpec((l, SMALL_W), lambda c, q: (cc(c), cfg["small_blk"])),
             pl.BlockSpec((1, SMALL_W), lambda c, q: (0, 0)), pl.BlockSpec((1, SMALL_W), lambda c, q: (0, 0)),
             pl.BlockSpec((1, SMALL_W), lambda c, q: (0, 0))]
    st_spec = pl.BlockSpec((None, None, n, hp), lambda c, q: (cc(c), q, 0, 0))
    y_spec = pl.BlockSpec((l, hp), lambda c, q: (cc(c), q))
    return specs, st_spec, y_spec, nc


def _ssd_fwd(xbc_c, proj, dtb, alog, dsk, cfg, name):
    specs, st_spec, y_spec, nc = _ssd_specs(cfg, False)
    fn = functools.partial(_ssd_chunk, hg=cfg["hg"], p=cfg["p"])

    def body(xs, bm, cm, sm, dtb_r, alog_r, dsk_r, y_ref, sts_ref, st):
        c, g = pl.program_id(0), pl.program_id(1)

        @pl.when(c == 0)
        def _():
            st[g] = jnp.zeros(st.shape[1:], F32)

        s_in = st[g]
        sts_ref[...] = s_in
        y, s_out = fn(xs[...], bm[...], cm[...], sm[...], dtb_r[...], alog_r[...], dsk_r[...], s_in, g)
        y_ref[...] = y
        st[g] = s_out

    return pl.pallas_call(
        body, name=name, grid=(nc, SSM_GROUPS), in_specs=specs, out_specs=[y_spec, st_spec],
        out_shape=[jax.ShapeDtypeStruct((cfg["s"], cfg["inner"]), F32),
                   jax.ShapeDtypeStruct((nc, SSM_GROUPS, cfg["n"], cfg["hp"]), F32)],
        scratch_shapes=[pltpu.VMEM((SSM_GROUPS, cfg["n"], cfg["hp"]), F32)],
        compiler_params=_cp("arbitrary", "arbitrary"),
    )(xbc_c, xbc_c, xbc_c, proj, dtb, alog, dsk)


def _ssd_bwd(xbc_c, proj, dtb, alog, dsk, states, dy, cfg, name):
    specs, st_spec, y_spec, nc = _ssd_specs(cfg, True)
    l, n, gn = SSM_CHUNK, cfg["n"], SSM_GROUPS * cfg["n"]
    fn = functools.partial(_ssd_chunk, hg=cfg["hg"], p=cfg["p"])
    rc = lambda c: nc - 1 - c

    def body(xs, bm, cm, sm, dtb_r, alog_r, dsk_r, sts_ref, dy_ref, dxs, dbm, dcm, dsm, ddtb, dalog, ddsk, dst):
        c, g = pl.program_id(0), pl.program_id(1)

        @pl.when(c == 0)
        def _():
            dst[g] = jnp.zeros(dst.shape[1:], F32)

        f = lambda *a: fn(*a, g)
        _, vjp = jax.vjp(f, xs[...], bm[...], cm[...], sm[...], dtb_r[...], alog_r[...], dsk_r[...], sts_ref[...])
        gx, gb, gc, gs, g1, g2, g3, gst = vjp((dy_ref[...], dst[g]))
        dxs[...] = gx
        dbm[...] = gb
        dcm[...] = gc
        dst[g] = gst

        @pl.when(g == 0)
        def _():
            dsm[...] = gs

        @pl.when(g > 0)
        def _():
            dsm[...] += gs

        first = jnp.logical_and(c == 0, g == 0)

        @pl.when(first)
        def _():
            ddtb[...] = g1
            dalog[...] = g2
            ddsk[...] = g3

        @pl.when(jnp.logical_not(first))
        def _():
            ddtb[...] += g1
            dalog[...] += g2
            ddsk[...] += g3

    par = pl.BlockSpec((1, SMALL_W), lambda c, q: (0, 0))
    return pl.pallas_call(
        body, name=name, grid=(nc, SSM_GROUPS), in_specs=specs + [st_spec, y_spec],
        out_specs=[y_spec, pl.BlockSpec((l, n), lambda c, q: (rc(c), q)), pl.BlockSpec((l, n), lambda c, q: (rc(c), q)),
                   pl.BlockSpec((l, SMALL_W), lambda c, q: (rc(c), 0)), par, par, par],
        out_shape=[jax.ShapeDtypeStruct((cfg["s"], cfg["inner"]), F32), jax.ShapeDtypeStruct((cfg["s"], gn), F32),
                   jax.ShapeDtypeStruct((cfg["s"], gn), F32), jax.ShapeDtypeStruct((cfg["s"], SMALL_W), F32),
                   jax.ShapeDtypeStruct((1, SMALL_W), F32), jax.ShapeDtypeStruct((1, SMALL_W), F32),
                   jax.ShapeDtypeStruct((1, SMALL_W), F32)],
        scratch_shapes=[pltpu.VMEM((SSM_GROUPS, cfg["n"], cfg["hp"]), F32)],
        compiler_params=_cp("arbitrary", "arbitrary"),
    )(xbc_c, xbc_c, xbc_c, proj, dtb, alog, dsk, states, dy)


def _gdn_prep(q, k, v, small, alog, dtb, h, *, boff, aoff):
    c = DN_CHUNK
    rr, dk = q.shape
    nb, w_ = rr // c, small.shape[1]
    lane = lax.broadcasted_iota(jnp.int32, (1, w_), 1)
    g_all = -jnp.exp(alog) * _softplus(small + dtb)
    r = lax.broadcasted_iota(jnp.int32, (c, c), 0)
    cc = lax.broadcasted_iota(jnp.int32, (c, c), 1)
    incl, strict = (r >= cc)[None], (r > cc)[None]
    eye = (r == cc).astype(F32)[None]
    tri_b = jnp.broadcast_to((r >= cc).astype(F32)[None], (nb, c, c))
    gcum_all = jnp.einsum("bij,bjw->biw", tri_b, g_all.reshape(nb, c, w_), precision=HIGHEST, preferred_element_type=F32)
    gc = jnp.sum(gcum_all * (lane == aoff + h).astype(F32)[None], axis=2, keepdims=True)
    beta = jnp.sum(jax.nn.sigmoid(small) * (lane == boff + h).astype(F32), axis=1, keepdims=True).reshape(nb, c, 1)
    qn = (q * lax.rsqrt(jnp.sum(q * q, axis=1, keepdims=True) + EPS) * (dk ** -0.5)).reshape(nb, c, dk)
    kn = (k * lax.rsqrt(jnp.sum(k * k, axis=1, keepdims=True) + EPS)).reshape(nb, c, dk)
    v3 = v.reshape(nb, c, v.shape[1])
    g_row = jnp.sum(gc * eye, axis=1, keepdims=True)
    decay = jnp.where(incl, jnp.exp(jnp.where(incl, gc - g_row, 0.0)), 0.0)
    kb = kn * beta
    kk = jnp.einsum("bik,bjk->bij", kb.astype(BF16), kn.astype(BF16), preferred_element_type=F32)
    neg_m = jnp.where(strict, -kk * decay, 0.0)
    t_inv = eye + neg_m
    pw = neg_m
    for _ in range(int(math.log2(c)) - 1):
        pw = jnp.einsum("bij,bjk->bik", pw, pw, precision=HIGHEST, preferred_element_type=F32)
        t_inv = t_inv + jnp.einsum("bij,bjk->bik", t_inv, pw, precision=HIGHEST, preferred_element_type=F32)
    eg = jnp.exp(gc)
    u = jnp.einsum("bij,bjv->biv", t_inv, v3 * beta, precision=HIGHEST, preferred_element_type=F32)
    w = jnp.einsum("bij,bjk->bik", t_inv, kb * eg, precision=HIGHEST, preferred_element_type=F32)
    a_qk = jnp.einsum("bik,bjk->bij", qn.astype(BF16), kn.astype(BF16), preferred_element_type=F32) * decay
    last = (lax.broadcasted_iota(jnp.int32, (1, c, 1), 1) == c - 1).astype(F32)
    g_last = jnp.sum(gc * last, axis=1, keepdims=True)
    q_dec = qn * eg
    k_dec = kn * jnp.exp(g_last - gc)
    gl = jnp.broadcast_to(g_last, (nb, 1, LANES))
    return (u.reshape(rr, -1), w.reshape(rr, dk), q_dec.reshape(rr, dk), k_dec.reshape(rr, dk), a_qk, gl)


def _gdn_prep_specs(cfg):
    rr, hd, dk, dv = cfg["prep_rows"], cfg["hd"], cfg["dk"], cfg["dv"]
    nb = rr // DN_CHUNK
    ins = [pl.BlockSpec((rr, dk), lambda i, h: (i, h)), pl.BlockSpec((rr, dk), lambda i, h: (i, hd + h)),
           pl.BlockSpec((rr, dv), lambda i, h: (i, (2 * hd * dk) // dv + h)),
           pl.BlockSpec((rr, SMALL_W), lambda i, h: (i, cfg["small_blk"])),
           pl.BlockSpec((1, SMALL_W), lambda i, h: (0, 0)), pl.BlockSpec((1, SMALL_W), lambda i, h: (0, 0))]
    hs = lambda d: pl.BlockSpec((None, rr, d), lambda i, h: (h, i, 0))
    outs = [hs(dv), hs(dk), hs(dk), hs(dk), pl.BlockSpec((None, nb, DN_CHUNK, DN_CHUNK), lambda i, h: (h, i, 0, 0)),
            pl.BlockSpec((None, nb, 1, LANES), lambda i, h: (h, i, 0, 0))]
    s, nch = cfg["s"], cfg["s"] // DN_CHUNK
    shapes = [jax.ShapeDtypeStruct((hd, s, dv), F32)] + [jax.ShapeDtypeStruct((hd, s, dk), F32)] * 3 + [
        jax.ShapeDtypeStruct((hd, nch, DN_CHUNK, DN_CHUNK), F32), jax.ShapeDtypeStruct((hd, nch, 1, LANES), F32)]
    return ins, outs, shapes


def _gdn_prep_fwd(qkv_c, proj, alog, dtb, cfg, name):
    ins, outs, shapes = _gdn_prep_specs(cfg)
    fn = functools.partial(_gdn_prep, boff=cfg["hs"], aoff=cfg["hs"] + cfg["hd"])

    def body(q, k, v, sm, al, db, *o):
        vals = fn(q[...], k[...], v[...], sm[...], al[...], db[...], pl.program_id(1))
        for ref, val in zip(o, vals):
            ref[...] = val

    return pl.pallas_call(
        body, name=name, grid=(cfg["s"] // cfg["prep_rows"], cfg["hd"]), in_specs=ins, out_specs=outs, out_shape=shapes,
        compiler_params=_cp("parallel", "parallel"),
    )(qkv_c, qkv_c, qkv_c, proj, alog, dtb)


def _gdn_prep_bwd(qkv_c, proj, alog, dtb, cts, cfg, name):
    ins, outs, _ = _gdn_prep_specs(cfg)
    rr, hd, dk, dv, s = cfg["prep_rows"], cfg["hd"], cfg["dk"], cfg["dv"], cfg["s"]
    fn = functools.partial(_gdn_prep, boff=cfg["hs"], aoff=cfg["hs"] + cfg["hd"])

    def body(q, k, v, sm, al, db, c0, c1, c2, c3, c4, c5, dq, dkk, dvv, dsm, dal, ddb):
        i, h = pl.program_id(0), pl.program_id(1)
        f = lambda *a: fn(*a, h)
        _, vjp = jax.vjp(f, q[...], k[...], v[...], sm[...], al[...], db[...])
        gq, gk, gv, gs, ga, gd = vjp((c0[...], c1[...], c2[...], c3[...], c4[...], c5[...]))
        dq[...] = gq
        dkk[...] = gk
        dvv[...] = gv

        @pl.when(h == 0)
        def _():
            dsm[...] = gs

        @pl.when(h > 0)
        def _():
            dsm[...] += gs

        first = jnp.logical_and(i == 0, h == 0)

        @pl.when(first)
        def _():
            dal[...] = ga
            ddb[...] = gd

        @pl.when(jnp.logical_not(first))
        def _():
            dal[...] += ga
            ddb[...] += gd

    par = pl.BlockSpec((1, SMALL_W), lambda i, h: (0, 0))
    return pl.pallas_call(
        body, name=name, grid=(s // rr, hd), in_specs=ins + outs,
        out_specs=[pl.BlockSpec((rr, dk), lambda i, h: (i, h)), pl.BlockSpec((rr, dk), lambda i, h: (i, h)),
                   pl.BlockSpec((rr, dv), lambda i, h: (i, h)), pl.BlockSpec((rr, SMALL_W), lambda i, h: (i, 0)), par, par],
        out_shape=[jax.ShapeDtypeStruct((s, hd * dk), F32), jax.ShapeDtypeStruct((s, hd * dk), F32),
                   jax.ShapeDtypeStruct((s, hd * dv), F32), jax.ShapeDtypeStruct((s, SMALL_W), F32),
                   jax.ShapeDtypeStruct((1, SMALL_W), F32), jax.ShapeDtypeStruct((1, SMALL_W), F32)],
        compiler_params=_cp("arbitrary", "arbitrary"),
    )(qkv_c, qkv_c, qkv_c, proj, alog, dtb, *cts)


def _gdn_scan_chunk(qd, aq, u, w, kd, gl, st):
    bm = lambda spec, a, b: jnp.einsum(spec, a.astype(BF16), b.astype(BF16), preferred_element_type=F32)
    v_new = u - bm("hck,hkv->hcv", w, st)
    o = bm("hck,hkv->hcv", qd, st) + bm("hij,hjv->hiv", aq, v_new)
    st_new = st * jnp.exp(gl) + bm("hck,hcv->hkv", kd, v_new)
    return o, st_new


def _gdn_scan_specs(cfg, rev):
    rr, hd, dk, dv = cfg["scan_rows"], cfg["hd"], cfg["dk"], cfg["dv"]
    nb, nblk = rr // DN_CHUNK, cfg["s"] // rr
    ii = (lambda i: nblk - 1 - i) if rev else (lambda i: i)
    hs = lambda d: pl.BlockSpec((hd, rr, d), lambda i: (0, ii(i), 0))
    ins = [hs(dk), pl.BlockSpec((hd, nb, DN_CHUNK, DN_CHUNK), lambda i: (0, ii(i), 0, 0)), hs(dv), hs(dk), hs(dk),
           pl.BlockSpec((hd, nb, 1, LANES), lambda i: (0, ii(i), 0, 0))]
    st_spec = pl.BlockSpec((nb, hd, dk, dv), lambda i: (ii(i), 0, 0, 0))
    return ins, hs(dv), st_spec, nb, nblk


def _gdn_scan_fwd(qd, aq, u, w, kd, gl, cfg, name):
    ins, o_spec, st_spec, nb, nblk = _gdn_scan_specs(cfg, False)
    hd, dk, dv, c = cfg["hd"], cfg["dk"], cfg["dv"], DN_CHUNK

    def body(qd_r, aq_r, u_r, w_r, kd_r, gl_r, o_r, sts_r, st):
        @pl.when(pl.program_id(0) == 0)
        def _():
            st[...] = jnp.zeros(st.shape, F32)

        s = st[...]
        for j in range(nb):
            rows = slice(j * c, (j + 1) * c)
            sts_r[j] = s
            o, s = _gdn_scan_chunk(qd_r[:, rows, :], aq_r[:, j], u_r[:, rows, :], w_r[:, rows, :], kd_r[:, rows, :],
                                   gl_r[:, j], s)
            o_r[:, rows, :] = o
        st[...] = s

    return pl.pallas_call(
        body, name=name, grid=(nblk,), in_specs=ins, out_specs=[o_spec, st_spec],
        out_shape=[jax.ShapeDtypeStruct((hd, cfg["s"], dv), F32),
                   jax.ShapeDtypeStruct((cfg["s"] // c, hd, dk, dv), F32)],
        scratch_shapes=[pltpu.VMEM((hd, dk, dv), F32)], compiler_params=_cp("arbitrary"),
    )(qd, aq, u, w, kd, gl)


def _gdn_scan_bwd(qd, aq, u, w, kd, gl, states, do, cfg, name):
    ins, o_spec, st_spec, nb, nblk = _gdn_scan_specs(cfg, True)
    hd, dk, dv, c = cfg["hd"], cfg["dk"], cfg["dv"], DN_CHUNK

    def body(qd_r, aq_r, u_r, w_r, kd_r, gl_r, sts_r, do_r, dqd, daq, du, dw, dkd, dgl, dst):
        @pl.when(pl.program_id(0) == 0)
        def _():
            dst[...] = jnp.zeros(dst.shape, F32)

        ds = dst[...]
        for j in reversed(range(nb)):
            rows = slice(j * c, (j + 1) * c)
            _, vjp = jax.vjp(_gdn_scan_chunk, qd_r[:, rows, :], aq_r[:, j], u_r[:, rows, :], w_r[:, rows, :],
                             kd_r[:, rows, :], gl_r[:, j], sts_r[j])
            g0, g1, g2, g3, g4, g5, ds = vjp((do_r[:, rows, :], ds))
            dqd[:, rows, :] = g0
            daq[:, j] = g1
            du[:, rows, :] = g2
            dw[:, rows, :] = g3
            dkd[:, rows, :] = g4
            dgl[:, j] = g5
        dst[...] = ds

    s, nch = cfg["s"], cfg["s"] // c
    return pl.pallas_call(
        body, name=name, grid=(nblk,), in_specs=ins + [st_spec, o_spec], out_specs=ins,
        out_shape=[jax.ShapeDtypeStruct((hd, s, dk), F32), jax.ShapeDtypeStruct((hd, nch, c, c), F32),
                   jax.ShapeDtypeStruct((hd, s, dv), F32), jax.ShapeDtypeStruct((hd, s, dk), F32),
                   jax.ShapeDtypeStruct((hd, s, dk), F32), jax.ShapeDtypeStruct((hd, nch, 1, LANES), F32)],
        scratch_shapes=[pltpu.VMEM((hd, dk, dv), F32)], compiler_params=_cp("arbitrary"),
    )(qd, aq, u, w, kd, gl, states, do)


def _loss_head(h, target, w, name):
    s, d = h.shape
    t = _tile(s, 512, 8)

    def body(h_r, t_r, w_r, loss_r, dh_r, dw_r):
        i = pl.program_id(0)
        (y,), vjp = jax.vjp(_rms_fn, h_r[...], w_r[...])
        err = y - t_r[...]
        part = 0.5 * jnp.sum(jnp.mean(err * err, axis=-1, keepdims=True))
        gh, gw = vjp((err * (1.0 / d),))
        dh_r[...] = gh

        @pl.when(i == 0)
        def _():
            loss_r[...] = jnp.zeros(loss_r.shape, F32) + part
            dw_r[...] = gw

        @pl.when(i > 0)
        def _():
            loss_r[...] += part
            dw_r[...] += gw

    row = pl.BlockSpec((t, d), lambda i: (i, 0))
    return pl.pallas_call(
        body, name=name, grid=(s // t,), in_specs=[row, row, pl.BlockSpec((1, d), lambda i: (0, 0))],
        out_specs=[pl.BlockSpec((8, LANES), lambda i: (0, 0)), row, pl.BlockSpec((1, d), lambda i: (0, 0))],
        out_shape=[jax.ShapeDtypeStruct((8, LANES), F32), jax.ShapeDtypeStruct((s, d), F32), jax.ShapeDtypeStruct((1, d), F32)],
        compiler_params=_cp("arbitrary"),
    )(h, target, w)


def _adamw_math(g, w, m, v):
    m = ADAM_B1 * m + (1.0 - ADAM_B1) * g
    v = ADAM_B2 * v + (1.0 - ADAM_B2) * jnp.square(g)
    m_hat = m / (1.0 - ADAM_B1 ** ADAM_STEP)
    v_hat = v / (1.0 - ADAM_B2 ** ADAM_STEP)
    delta = -ADAM_LR * (m_hat / (jnp.sqrt(v_hat) + ADAM_EPS) + ADAM_WD * w)
    return delta, m, v


def _adamw_sum(parts, w, m, v, name):
    nl, r, c = w.shape
    tr = _tile(r, 64, 8)

    def body(p_r, w_r, m_r, v_r, g_o, d_o, m_o, v_o):
        g = p_r[0].astype(F32)
        for q in range(1, N_DEV):
            g = g + p_r[q].astype(F32)
        delta, mn, vn = _adamw_math(g, w_r[...], m_r[...], v_r[...])
        g_o[...] = g
        d_o[...] = delta
        m_o[...] = mn
        v_o[...] = vn

    blk = pl.BlockSpec((None, tr, c), lambda a, i: (a, i, 0))
    return pl.pallas_call(
        body, name=name, grid=(nl, r // tr),
        in_specs=[pl.BlockSpec((N_DEV, None, tr, c), lambda a, i: (0, a, i, 0)), blk, blk, blk],
        out_specs=[blk] * 4, out_shape=[jax.ShapeDtypeStruct(w.shape, F32)] * 4, compiler_params=_cp("parallel", "parallel"),
    )(parts, w, m, v)


def _sum_parts(parts, name):
    _, r, c = parts.shape
    tr = _tile(r, 512, 8)

    def body(p_r, o_r):
        g = p_r[0]
        for q in range(1, N_DEV):
            g = g + p_r[q]
        o_r[...] = g

    return pl.pallas_call(
        body, name=name, grid=(r // tr,), in_specs=[pl.BlockSpec((N_DEV, tr, c), lambda i: (0, i, 0))],
        out_specs=pl.BlockSpec((tr, c), lambda i: (i, 0)), out_shape=jax.ShapeDtypeStruct((r, c), F32),
        compiler_params=_cp("parallel"),
    )(parts)


def _adamw_flat(g, w, m, v, name):
    r, c = w.shape
    tr = _tile(r, 512, 8)

    def body(g_r, w_r, m_r, v_r, d_o, m_o, v_o):
        delta, mn, vn = _adamw_math(g_r[...], w_r[...], m_r[...], v_r[...])
        d_o[...] = delta
        m_o[...] = mn
        v_o[...] = vn

    blk = pl.BlockSpec((tr, c), lambda i: (i, 0))
    return pl.pallas_call(
        body, name=name, grid=(r // tr,), in_specs=[blk] * 4, out_specs=[blk] * 3,
        out_shape=[jax.ShapeDtypeStruct(w.shape, F32)] * 3, compiler_params=_cp("parallel"),
    )(g, w, m, v)


def _exchange(arrays, *, scatter, name):
    nt = len(arrays)
    blocks = [a.shape[1:] if scatter else a.shape for a in arrays]

    def body(*refs):
        srcs, dsts = refs[:nt], refs[nt:2 * nt]
        send_sems, recv_sems, local_sems = refs[2 * nt:]
        x, y, c = lax.axis_index("x"), lax.axis_index("y"), lax.axis_index("c")
        me = 4 * x + 2 * y + c
        copies = []
        for t in range(nt):
            own = srcs[t].at[me] if scatter else srcs[t]
            loc = pltpu.make_async_copy(own, dsts[t].at[me], local_sems.at[t])
            loc.start()
            copies.append(loc)
        remote = []
        for k in range(1, N_DEV):
            px = (1 - x) if (k & 4) else x
            py = (1 - y) if (k & 2) else y
            pc = (1 - c) if (k & 1) else c
            peer = 4 * px + 2 * py + pc
            for t in range(nt):
                src = srcs[t].at[peer] if scatter else srcs[t]
                cp = pltpu.make_async_remote_copy(
                    src_ref=src, dst_ref=dsts[t].at[me], send_sem=send_sems.at[t, k - 1], recv_sem=recv_sems.at[t, k - 1],
                    device_id=(px, py, pc), device_id_type=MESH)
                cp.start()
                arrive = pltpu.make_async_remote_copy(
                    src_ref=src, dst_ref=dsts[t].at[peer], send_sem=send_sems.at[t, k - 1], recv_sem=recv_sems.at[t, k - 1],
                    device_id=(px, py, pc), device_id_type=MESH)
                remote.append((cp, arrive))
        for cp, arrive in remote:
            arrive.wait_recv()
        for cp, arrive in remote:
            cp.wait_send()
        for loc in copies:
            loc.wait()

    return pl.pallas_call(
        body, name=name, in_specs=[pl.BlockSpec(memory_space=pl.ANY)] * nt,
        out_specs=[pl.BlockSpec(memory_space=pl.ANY)] * nt,
        out_shape=[jax.ShapeDtypeStruct((N_DEV,) + tuple(b), a.dtype) for a, b in zip(arrays, blocks)],
        scratch_shapes=[pltpu.SemaphoreType.DMA((nt, N_DEV - 1)), pltpu.SemaphoreType.DMA((nt, N_DEV - 1)),
                        pltpu.SemaphoreType.DMA((nt,))],
    )(*arrays)


def _config(x, ffn1_w_out, ssm_conv_b, ssm_dt_bias, ssm_norm, dn_conv_w, dn_dt_bias, dn_norm, dn_w_branch):
    s, d = x.shape[-2], x.shape[-1]
    f = ffn1_w_out.shape[1] * N_DEV
    cs, hs, inner = ssm_conv_b.shape[1], ssm_dt_bias.shape[1], ssm_norm.shape[1]
    gn = (cs - inner) // 2
    hd, dv = dn_dt_bias.shape[1], dn_norm.shape[1]
    cd = dn_conv_w.shape[2] * N_DEV
    vd = hd * dv
    kd = (cd - vd) // 2
    cfg = dict(s=s, d=d, f=f, cs=cs, hs=hs, inner=inner, gn=gn, n=gn // SSM_GROUPS, hg=hs // SSM_GROUPS, p=inner // hs,
               hp=inner // SSM_GROUPS, hd=hd, dv=dv, dk=kd // hd, cd=cd, vd=vd, kd=kd)
    offs, o = {}, 0
    for nm, wd in (("xbc", cs), ("qkv", cd), ("gates", 2 * d), ("zs", inner), ("zd", vd), ("small", SMALL_W)):
        offs[nm] = o
        o += wd
    cfg["offs"], cfg["pw"] = offs, o
    cfg["cw"] = 512
    cfg["small_blk"] = offs["small"] // SMALL_W
    cfg["prep_rows"] = min(s, 8 * DN_CHUNK)
    cfg["scan_rows"] = min(s, 4 * DN_CHUNK)
    cfg["in_split"] = (inner, cs, hs, cd, vd, hd, hd, d, d)
    assert hs + 2 * hd <= SMALL_W and cfg["dk"] == dv and dv == LANES
    assert offs["qkv"] % cfg["cw"] == 0 and offs["gates"] % (2 * d) == 0 and offs["zs"] % cfg["hp"] == 0
    assert offs["zd"] % dv == 0 and all(wd % cfg["cw"] == 0 for wd in (inner, gn, kd, vd)) and inner % cfg["n"] == 0
    return cfg


def _permute_w_in(w, cfg):
    pts = [0]
    for wd in cfg["in_split"]:
        pts.append(pts[-1] + wd)
    z_s, xbc, dt, qkv, z_d, b_d, a_d, g_s, g_d = [w[:, pts[i]:pts[i + 1]] for i in range(9)]
    pad = jnp.zeros((w.shape[0], SMALL_W - dt.shape[1] - b_d.shape[1] - a_d.shape[1]), w.dtype)
    return jnp.concatenate([xbc, qkv, g_s, g_d, z_s, z_d, dt, b_d, a_d, pad], axis=1)


def _unpermute_w_in(g, cfg):
    o, d = cfg["offs"], cfg["d"]
    hs, hd = cfg["hs"], cfg["hd"]
    sm = g[:, o["small"]:]
    return jnp.concatenate([
        g[:, o["zs"]:o["zs"] + cfg["inner"]], g[:, o["xbc"]:o["xbc"] + cfg["cs"]], sm[:, :hs],
        g[:, o["qkv"]:o["qkv"] + cfg["cd"]], g[:, o["zd"]:o["zd"] + cfg["vd"]], sm[:, hs:hs + hd], sm[:, hs + hd:hs + 2 * hd],
        g[:, o["gates"]:o["gates"] + d], g[:, o["gates"] + d:o["gates"] + 2 * d]], axis=1)


def _lane_row(v, off):
    return jnp.pad(v.astype(F32), (off, SMALL_W - off - v.shape[0]))[None]


def _pack(arrs):
    flat = []
    for a in arrs:
        v = a.reshape(-1)
        flat.append(jnp.pad(v, (0, (-v.shape[0]) % LANES)))
    v = jnp.concatenate(flat)
    v = jnp.pad(v, (0, (-v.shape[0]) % (8 * LANES)))
    return v.reshape(-1, LANES)


def _unpack(packed, shapes):
    v, out, o = packed.reshape(-1), [], 0
    for sh in shapes:
        n = math.prod(sh)
        out.append(v[o:o + n].reshape(sh))
        o += n + (-n) % LANES
    return out


def _cols_gathered(g):
    nd, nl, r, c = g.shape
    return jnp.transpose(g, (1, 2, 0, 3)).reshape(nl, r, nd * c)


def _rows_gathered(g):
    nd, nl, r, c = g.shape
    return jnp.transpose(g, (1, 0, 2, 3)).reshape(nl, nd * r, c)


def _cols_scatter(g):
    nl, r, c = g.shape
    return jnp.transpose(g.reshape(nl, r, N_DEV, c // N_DEV), (2, 0, 1, 3))


def _rows_scatter(g):
    nl, r, c = g.shape
    return jnp.transpose(g.reshape(nl, N_DEV, r // N_DEV, c), (1, 0, 2, 3))


def _ffn_fwd(h, nw, w_in, w_out, cfg, tag):
    s, d, f = cfg["s"], cfg["d"], cfg["f"]
    t = _tile(s, 512, 8)
    xn, = _rw_fwd(_rms_fn, f"{tag}_norm", (1, s // t), [(h, _rows(t, d), None), (nw, _par(d), None)],
                  [(jax.ShapeDtypeStruct((s, d), BF16), _rows(t, d), None)])
    gu = _matmul(xn, w_in, mode="nn", name=f"{tag}_in")
    t2 = _tile(s, 256, 8)
    act, = _rw_fwd(_swiglu_fn, f"{tag}_act", (1, s // t2), [(gu, _rows(t2, 2 * f), [f, f])],
                   [(jax.ShapeDtypeStruct((s, f), BF16), _rows(t2, f), None)])
    h_out = _matmul(act, w_out, mode="nn", name=f"{tag}_out", res=h, scale=0.5)
    return h_out, dict(h=h, xn=xn, gu=gu, act=act)


def _ffn_bwd(gh, r, nw, w_in, w_out, cfg, tag):
    s, d, f = cfg["s"], cfg["d"], cfg["f"]
    d_wout = _matmul(r["act"], gh, mode="tn", name=f"{tag}_dwout", scale=0.5)
    d_act = _matmul(gh, w_out, mode="nt", name=f"{tag}_dact", scale=0.5)
    t2 = _tile(s, 256, 8)
    d_gu, = _rw_bwd(_swiglu_fn, f"{tag}_dgu", (1, s // t2), [(r["gu"], _rows(t2, 2 * f), [f, f])],
                    [(d_act, _rows(t2, f), None)], [(0, jax.ShapeDtypeStruct((s, 2 * f), BF16), _rows(t2, 2 * f), "tile")])
    d_win = _matmul(r["xn"], d_gu, mode="tn", name=f"{tag}_dwin")
    d_xn = _matmul(d_gu, w_in, mode="nt", name=f"{tag}_dxn")
    t = _tile(s, 512, 8)
    d_h, d_nw = _rw_bwd(_rms_fn, f"{tag}_dnorm", (1, s // t), [(r["h"], _rows(t, d), None), (nw, _par(d), None)],
                        [(d_xn, _rows(t, d), None)],
                        [(0, jax.ShapeDtypeStruct((s, d), F32), _rows(t, d), "tile"),
                         (1, jax.ShapeDtypeStruct((1, d), F32), _par(d), "acc_all")],
                        add=(gh, _rows(t, d), 0))
    return d_h, (d_nw, d_win, d_wout)


def _mix_fwd(h, p, cfg, tag):
    s, d, o = cfg["s"], cfg["d"], cfg["offs"]
    t = _tile(s, 512, 8)
    u, = _rw_fwd(_rms_fn, f"{tag}_norm", (1, s // t), [(h, _rows(t, d), None), (p["mix_norm"], _par(d), None)],
                 [(jax.ShapeDtypeStruct((s, d), BF16), _rows(t, d), None)])
    proj = _matmul(u, p["w_in"], mode="nn", name=f"{tag}_in")
    cw, tr = cfg["cw"], _tile(s, 512, 8)
    xbc_c = _conv_fwd(proj, o["xbc"] // cw, p["ssm_conv_w"], p["ssm_conv_b"], name=f"{tag}_sconv", cw=cw, tr=tr)
    qkv_c = _conv_fwd(proj, o["qkv"] // cw, p["dn_conv_w"], jnp.zeros((1, cfg["cd"]), F32), name=f"{tag}_dconv", cw=cw, tr=tr)
    y, s_states = _ssd_fwd(xbc_c, proj, p["ssm_dtb"], p["ssm_alog"], p["ssm_dsk"], cfg, f"{tag}_ssd")
    hp, inner, g = cfg["hp"], cfg["inner"], SSM_GROUPS
    t4 = _tile(s, 512, 8)
    zs_blk = o["zs"] // hp
    y_s, = _rw_fwd(_ssm_out_fn, f"{tag}_sout", (g, s // t4),
                   [(y, _rows(t4, hp, lambda j: j), None), (proj, _rows(t4, hp, lambda j: zs_blk + j), None),
                    (p["ssm_norm"], _par(hp, True), None)],
                   [(jax.ShapeDtypeStruct((s, inner), BF16), _rows(t4, hp, lambda j: j), None)])
    uu, ww, qd, kd, aq, gl = _gdn_prep_fwd(qkv_c, proj, p["dn_alog"], p["dn_dtb"], cfg, f"{tag}_prep")
    o_dn, d_states = _gdn_scan_fwd(qd, aq, uu, ww, kd, gl, cfg, f"{tag}_scan")
    hd, dv = cfg["hd"], cfg["dv"]
    zd_blk = o["zd"] // dv
    o_spec = pl.BlockSpec((None, t4, dv), lambda j, i: (j, i, 0))
    y_d, = _rw_fwd(_dn_out_fn, f"{tag}_dout", (hd, s // t4),
                   [(o_dn, o_spec, None), (proj, _rows(t4, dv, lambda j: zd_blk + j), None), (p["dn_norm"], _par(dv), None)],
                   [(jax.ShapeDtypeStruct((s, cfg["vd"]), BF16), _rows(t4, dv, lambda j: j), None)])
    ps = _matmul(y_s, p["ssm_w_branch"], mode="nn", name=f"{tag}_sbr")
    pd = _matmul(y_d, p["dn_w_branch"], mode="nn", name=f"{tag}_dbr")
    t6 = _tile(s, 256, 8)
    merged, = _rw_fwd(_merge_fn, f"{tag}_merge", (1, s // t6),
                      [(proj, _rows(t6, 2 * d, o["gates"] // (2 * d)), [d, d]), (ps, _rows(t6, d), None), (pd, _rows(t6, d), None)],
                      [(jax.ShapeDtypeStruct((s, d), BF16), _rows(t6, d), None)])
    h_out = _matmul(merged, p["w_out"], mode="nn", name=f"{tag}_out", res=h)
    res = dict(h=h, u=u, proj=proj, xbc_c=xbc_c, qkv_c=qkv_c, y=y, s_states=s_states, y_s=y_s, uu=uu, ww=ww, qd=qd, kd=kd,
               aq=aq, gl=gl, o_dn=o_dn, d_states=d_states, y_d=y_d, ps=ps, pd=pd, merged=merged)
    return h_out, res


def _mix_bwd(gh, r, p, cfg, tag):
    s, d, o = cfg["s"], cfg["d"], cfg["offs"]
    cw, tr = cfg["cw"], _tile(s, 512, 8)
    hp, inner, g, hd, dv, dk = cfg["hp"], cfg["inner"], SSM_GROUPS, cfg["hd"], cfg["dv"], cfg["dk"]
    proj = r["proj"]
    grads = {}
    grads["w_out"] = _matmul(r["merged"], gh, mode="tn", name=f"{tag}_dwout")
    d_merged = _matmul(gh, p["w_out"], mode="nt", name=f"{tag}_dmerged")
    dproj = jax.ShapeDtypeStruct((s, cfg["pw"]), BF16)
    t6 = _tile(s, 256, 8)
    gates_spec = _rows(t6, 2 * d, o["gates"] // (2 * d))
    dproj, d_ps, d_pd = _rw_bwd(
        _merge_fn, f"{tag}_dmerge", (1, s // t6),
        [(proj, gates_spec, [d, d]), (r["ps"], _rows(t6, d), None), (r["pd"], _rows(t6, d), None)],
        [(d_merged, _rows(t6, d), None)],
        [(0, dproj, gates_spec, "tile"), (1, jax.ShapeDtypeStruct((s, d), BF16), _rows(t6, d), "tile"),
         (2, jax.ShapeDtypeStruct((s, d), BF16), _rows(t6, d), "tile")])
    grads["ssm_w_branch"] = _matmul(r["y_s"], d_ps, mode="tn", name=f"{tag}_dwsbr")
    grads["dn_w_branch"] = _matmul(r["y_d"], d_pd, mode="tn", name=f"{tag}_dwdbr")
    d_ys = _matmul(d_ps, p["ssm_w_branch"], mode="nt", name=f"{tag}_dys")
    d_yd = _matmul(d_pd, p["dn_w_branch"], mode="nt", name=f"{tag}_dyd")
    t4 = _tile(s, 512, 8)
    zs_blk, zd_blk = o["zs"] // hp, o["zd"] // dv
    zs_spec = _rows(t4, hp, lambda j: zs_blk + j)
    d_y, dproj, grads["ssm_norm"] = _rw_bwd(
        _ssm_out_fn, f"{tag}_dsout", (g, s // t4),
        [(r["y"], _rows(t4, hp, lambda j: j), None), (proj, zs_spec, None), (p["ssm_norm"], _par(hp, True), None)],
        [(d_ys, _rows(t4, hp, lambda j: j), None)],
        [(0, jax.ShapeDtypeStruct((s, inner), F32), _rows(t4, hp, lambda j: j), "tile"),
         (1, jax.ShapeDtypeStruct(dproj.shape, BF16), zs_spec, "tile"),
         (2, jax.ShapeDtypeStruct((1, inner), F32), _par(hp, True), "acc_row")],
        alias=(dproj, 1))
    o_spec = pl.BlockSpec((None, t4, dv), lambda j, i: (j, i, 0))
    zd_spec = _rows(t4, dv, lambda j: zd_blk + j)
    d_o, dproj, grads["dn_norm"] = _rw_bwd(
        _dn_out_fn, f"{tag}_ddout", (hd, s // t4),
        [(r["o_dn"], o_spec, None), (proj, zd_spec, None), (p["dn_norm"], _par(dv), None)],
        [(d_yd, _rows(t4, dv, lambda j: j), None)],
        [(0, jax.ShapeDtypeStruct((hd, s, dv), F32), o_spec, "tile"),
         (1, jax.ShapeDtypeStruct(dproj.shape, BF16), zd_spec, "tile"),
         (2, jax.ShapeDtypeStruct((1, dv), F32), _par(dv), "acc_all")],
        alias=(dproj, 1))
    d_xs, d_bm, d_cm, dsm_s, g_dtb, g_alog, g_dsk = _ssd_bwd(
        r["xbc_c"], proj, p["ssm_dtb"], p["ssm_alog"], p["ssm_dsk"], r["s_states"], d_y, cfg, f"{tag}_dssd")
    grads["ssm_dt_bias"], grads["ssm_a_log"], grads["ssm_d"] = (v[0, :cfg["hs"]] for v in (g_dtb, g_alog, g_dsk))
    dws, dbs, col = [], [], 0
    for nm, dy in (("xs", d_xs), ("bm", d_bm), ("cm", d_cm)):
        wd = dy.shape[1]
        dproj, dw_, db_ = _conv_bwd(proj, (o["xbc"] + col) // cw, p["ssm_conv_w"][:, col:col + wd],
                                    p["ssm_conv_b"][:, col:col + wd], dy, dproj, (o["xbc"] + col) // cw,
                                    name=f"{tag}_dsconv_{nm}", cw=cw, tr=tr)
        dws.append(dw_)
        dbs.append(db_)
        col += wd
    grads["ssm_conv_w"] = jnp.concatenate(dws, axis=1)
    grads["ssm_conv_b"] = jnp.concatenate(dbs, axis=1)[0]
    cts = _gdn_scan_bwd(r["qd"], r["aq"], r["uu"], r["ww"], r["kd"], r["gl"], r["d_states"], d_o, cfg, f"{tag}_dscan")
    d_qd, d_aq, d_uu, d_ww, d_kd, d_gl = cts
    d_q, d_k, d_v, dsm_d, g_alog_d, g_dtb_d = _gdn_prep_bwd(
        r["qkv_c"], proj, p["dn_alog"], p["dn_dtb"], (d_uu, d_ww, d_qd, d_kd, d_aq, d_gl), cfg, f"{tag}_dprep")
    a0 = cfg["hs"] + hd
    grads["dn_a_log"], grads["dn_dt_bias"] = g_alog_d[0, a0:a0 + hd], g_dtb_d[0, a0:a0 + hd]
    dws, col = [], 0
    zero_b = jnp.zeros((1, cfg["cd"]), F32)
    for nm, dy in (("q", d_q), ("k", d_k), ("v", d_v)):
        wd = dy.shape[1]
        dproj, dw_, _ = _conv_bwd(proj, (o["qkv"] + col) // cw, p["dn_conv_w"][:, col:col + wd], zero_b[:, col:col + wd], dy,
                                  dproj, (o["qkv"] + col) // cw, name=f"{tag}_ddconv_{nm}", cw=cw, tr=tr)
        dws.append(dw_)
        col += wd
    grads["dn_conv_w"] = jnp.concatenate(dws, axis=1)
    sm_spec = _rows(t4, SMALL_W, cfg["small_blk"])
    dproj, = _rw_bwd(lambda a: (a,), f"{tag}_dsmall", (1, s // t4), [(dsm_s, _rows(t4, SMALL_W), None)],
                     [(dsm_s, _rows(t4, SMALL_W), None)], [(0, jax.ShapeDtypeStruct(dproj.shape, BF16), sm_spec, "tile")],
                     add=(dsm_d, _rows(t4, SMALL_W), 0), alias=(dproj, 0))
    grads["w_in"] = _matmul(r["u"], dproj, mode="tn", name=f"{tag}_dwin")
    d_u = _matmul(dproj, p["w_in"], mode="nt", name=f"{tag}_du")
    t = _tile(s, 512, 8)
    d_h, grads["mix_norm"] = _rw_bwd(
        _rms_fn, f"{tag}_dnorm", (1, s // t), [(r["h"], _rows(t, d), None), (p["mix_norm"], _par(d), None)],
        [(d_u, _rows(t, d), None)],
        [(0, jax.ShapeDtypeStruct((s, d), F32), _rows(t, d), "tile"), (1, jax.ShapeDtypeStruct((1, d), F32), _par(d), "acc_all")],
        add=(gh, _rows(t, d), 0))
    return d_h, grads


_BIG = ("ffn1_w_in", "ffn1_w_out", "w_in", "ssm_w_branch", "dn_w_branch", "w_out", "ffn2_w_in", "ffn2_w_out")
_COL_SHARDED = ("ffn1_w_in", "w_in", "ffn2_w_in")
_CONV = ("ssm_conv_w", "dn_conv_w")
_NAMES = ("ffn1_norm", "ffn1_w_in", "ffn1_w_out", "mix_norm", "w_in", "ssm_conv_w", "ssm_conv_b", "ssm_dt_bias", "ssm_a_log",
          "ssm_d", "ssm_norm", "ssm_w_branch", "dn_conv_w", "dn_dt_bias", "dn_a_log", "dn_norm", "dn_w_branch", "w_out",
          "ffn2_norm", "ffn2_w_in", "ffn2_w_out", "final_norm")


def kernel(x, ffn1_norm, ffn1_w_in, ffn1_w_out, mix_norm, w_in, ssm_conv_w, ssm_conv_b, ssm_dt_bias, ssm_a_log, ssm_d, ssm_norm, ssm_w_branch, dn_conv_w, dn_dt_bias, dn_a_log, dn_norm, dn_w_branch, w_out, ffn2_norm, ffn2_w_in, ffn2_w_out, final_norm, loss_target, m_ffn1_norm, m_ffn1_w_in, m_ffn1_w_out, m_mix_norm, m_w_in, m_ssm_conv_w, m_ssm_conv_b, m_ssm_dt_bias, m_ssm_a_log, m_ssm_d, m_ssm_norm, m_ssm_w_branch, m_dn_conv_w, m_dn_dt_bias, m_dn_a_log, m_dn_norm, m_dn_w_branch, m_w_out, m_ffn2_norm, m_ffn2_w_in, m_ffn2_w_out, m_final_norm, v_ffn1_norm, v_ffn1_w_in, v_ffn1_w_out, v_mix_norm, v_w_in, v_ssm_conv_w, v_ssm_conv_b, v_ssm_dt_bias, v_ssm_a_log, v_ssm_d, v_ssm_norm, v_ssm_w_branch, v_dn_conv_w, v_dn_dt_bias, v_dn_a_log, v_dn_norm, v_dn_w_branch, v_w_out, v_ffn2_norm, v_ffn2_w_in, v_ffn2_w_out, v_final_norm):
    args = locals()
    w = {n: args[n] for n in _NAMES}
    mom = {n: args["m_" + n] for n in _NAMES}
    var = {n: args["v_" + n] for n in _NAMES}
    cfg = _config(x, ffn1_w_out, ssm_conv_b, ssm_dt_bias, ssm_norm, dn_conv_w, dn_dt_bias, dn_norm, dn_w_branch)
    depth, s, d = ffn1_norm.shape[0], cfg["s"], cfg["d"]
    me = 4 * lax.axis_index("x") + 2 * lax.axis_index("y") + lax.axis_index("c")

    gathered = _exchange([w[n].astype(BF16) for n in _BIG] + [w[n] for n in _CONV], scatter=False, name="gather_weights")
    full = {}
    for n, g in zip(_BIG + _CONV, gathered):
        full[n] = _cols_gathered(g) if (n in _COL_SHARDED or n in _CONV) else _rows_gathered(g)

    hs, hd = cfg["hs"], cfg["hd"]
    layers = []
    for l in range(depth):
        layers.append(dict(
            ffn1_norm=ffn1_norm[l][None], ffn1_w_in=full["ffn1_w_in"][l], ffn1_w_out=full["ffn1_w_out"][l],
            mix_norm=mix_norm[l][None], w_in=_permute_w_in(full["w_in"][l], cfg),
            ssm_conv_w=full["ssm_conv_w"][l], ssm_conv_b=ssm_conv_b[l][None],
            ssm_dtb=_lane_row(ssm_dt_bias[l], 0), ssm_alog=_lane_row(ssm_a_log[l], 0), ssm_dsk=_lane_row(ssm_d[l], 0),
            ssm_norm=ssm_norm[l][None], ssm_w_branch=full["ssm_w_branch"][l], dn_conv_w=full["dn_conv_w"][l],
            dn_dtb=_lane_row(dn_dt_bias[l], hs + hd), dn_alog=_lane_row(dn_a_log[l], hs + hd), dn_norm=dn_norm[l][None],
            dn_w_branch=full["dn_w_branch"][l], w_out=full["w_out"][l],
            ffn2_norm=ffn2_norm[l][None], ffn2_w_in=full["ffn2_w_in"][l], ffn2_w_out=full["ffn2_w_out"][l]))

    h = x.reshape(s, d)
    saved = []
    for l, p in enumerate(layers):
        h, r1 = _ffn_fwd(h, p["ffn1_norm"], p["ffn1_w_in"], p["ffn1_w_out"], cfg, f"l{l}_ffn1")
        h, rm = _mix_fwd(h, p, cfg, f"l{l}_mix")
        h, r2 = _ffn_fwd(h, p["ffn2_norm"], p["ffn2_w_in"], p["ffn2_w_out"], cfg, f"l{l}_ffn2")
        saved.append((r1, rm, r2))
    loss_blk, gh, g_final = _loss_head(h, loss_target.reshape(s, d), final_norm[None], "loss_head")
    loss = lax.psum(loss_blk[0, 0], ("x", "y", "c"))

    lg = [None] * depth
    for l in reversed(range(depth)):
        p, (r1, rm, r2) = layers[l], saved[l]
        gh, (g_n2, g_win2, g_wout2) = _ffn_bwd(gh, r2, p["ffn2_norm"], p["ffn2_w_in"], p["ffn2_w_out"], cfg, f"l{l}_ffn2")
        gh, gm = _mix_bwd(gh, rm, p, cfg, f"l{l}_mix")
        gh, (g_n1, g_win1, g_wout1) = _ffn_bwd(gh, r1, p["ffn1_norm"], p["ffn1_w_in"], p["ffn1_w_out"], cfg, f"l{l}_ffn1")
        gm["w_in"] = _unpermute_w_in(gm["w_in"], cfg)
        gm.update(ffn1_norm=g_n1[0], ffn1_w_in=g_win1, ffn1_w_out=g_wout1, ffn2_norm=g_n2[0], ffn2_w_in=g_win2,
                  ffn2_w_out=g_wout2, mix_norm=gm["mix_norm"][0], ssm_norm=gm["ssm_norm"][0], dn_norm=gm["dn_norm"][0])
        lg[l] = gm
    grad_x = gh.reshape(x.shape)
    local = {n: jnp.stack([lg[l][n] for l in range(depth)]) for n in _NAMES if n != "final_norm"}
    local["final_norm"] = g_final[0]

    parts = _exchange([_cols_scatter(local[n]) if n in _COL_SHARDED else _rows_scatter(local[n]) for n in _BIG],
                      scatter=True, name="scatter_grads")
    out_g, out_d, out_m, out_v = {}, {}, {}, {}
    for n, pt in zip(_BIG, parts):
        out_g[n], out_d[n], out_m[n], out_v[n] = _adamw_sum(pt, w[n], mom[n], var[n], f"adamw_{n}")

    small = [n for n in _NAMES if n not in _BIG]
    packed, = _exchange([_pack([local[n] for n in small])], scatter=False, name="gather_small_grads")
    total = _unpack(_sum_parts(packed, "sum_small_grads"), [local[n].shape for n in small])
    for n, g in zip(small, total):
        if n in _CONV:
            c = w[n].shape[2]
            g = lax.dynamic_slice_in_dim(g, me * c, c, axis=2)
        out_g[n] = g
    shapes = [w[n].shape for n in small]
    upd = _adamw_flat(_pack([out_g[n] for n in small]), _pack([w[n] for n in small]), _pack([mom[n] for n in small]),
                      _pack([var[n] for n in small]), "adamw_small")
    for dst, pk in zip((out_d, out_m, out_v), upd):
        for n, a in zip(small, _unpack(pk, shapes)):
            dst[n] = a

    return (loss, grad_x, *[out_g[n] for n in _NAMES], *[out_d[n] for n in _NAMES], *[out_m[n] for n in _NAMES],
            *[out_v[n] for n in _NAMES])
```

```python
import functools
import math

import jax
import jax.numpy as jnp
from jax import lax
from jax.experimental import pallas as pl
from jax.experimental.pallas import tpu as pltpu

F32, BF16 = jnp.float32, jnp.bfloat16
MESH = pl.DeviceIdType.MESH

N_DEV = 8
EPS = 1e-6
CONV_K = 4
SSM_GROUPS = 4
SSM_CHUNK = 128
DN_CHUNK = 64
ADAM_LR, ADAM_B1, ADAM_B2, ADAM_EPS, ADAM_WD, ADAM_STEP = 0.001, 0.9, 0.999, 1e-08, 0.01, 10

V7X_VMEM_BYTES = 64 * 1024 * 1024
VMEM_LIMIT = 52 * 1024 * 1024
MATMUL_VMEM_BUDGET = 44 * 1024 * 1024
LANES = 128
SMALL_W = 256
SMALL_R = 128
CONV_HALO = 8
CONV_STRIP = 32


def _tile(n, target, quantum):
    if n <= target:
        return n
    t = (target // quantum) * quantum
    while t >= quantum:
        if n % t == 0:
            return t
        t -= quantum
    return n


def _cp(*sem):
    return pltpu.CompilerParams(dimension_semantics=sem, vmem_limit_bytes=VMEM_LIMIT)


def _softplus(x):
    return jnp.maximum(x, 0.0) + jnp.log1p(jnp.exp(-jnp.abs(x)))


def _silu(x):
    return x * jax.nn.sigmoid(x)


def _bdot(a, b, dims):
    return lax.dot_general(a.astype(BF16), b.astype(BF16), dims, preferred_element_type=F32)


_NN = (((1,), (0,)), ((), ()))
_NT = (((1,), (1,)), ((), ()))
_TN = (((0,), (0,)), ((), ()))
_MM_DIMS = {"nn": _NN, "nt": _NT, "tn": _TN}


def _mm3(spec, a, b):
    ah, bh = a.astype(BF16), b.astype(BF16)
    al, bl = (a - ah.astype(F32)).astype(BF16), (b - bh.astype(F32)).astype(BF16)
    e = lambda x, y: jnp.einsum(spec, x, y, preferred_element_type=F32)
    return e(ah, bh) + (e(ah, bl) + e(al, bh))


@jax.custom_vjp
def _pmm(a, b):
    return _mm3("bij,bjk->bik", a, b)


def _pmm_fwd(a, b):
    return _pmm(a, b), (a, b)


def _pmm_bwd(res, g):
    a, b = res
    return _mm3("bik,bjk->bij", g, b), _mm3("bji,bjk->bik", a, g)


_pmm.defvjp(_pmm_fwd, _pmm_bwd)


@jax.custom_vjp
def _neumann_inv(n):
    c = n.shape[-1]
    r = lax.broadcasted_iota(jnp.int32, (c, c), 0)
    cc = lax.broadcasted_iota(jnp.int32, (c, c), 1)
    t, pw = (r == cc).astype(F32)[None] + n, n
    for _ in range(int(math.log2(c)) - 1):
        pw = _mm3("bij,bjk->bik", pw, pw)
        t = t + _mm3("bij,bjk->bik", t, pw)
    return t


def _neumann_fwd(n):
    t = _neumann_inv(n)
    return t, t


def _neumann_bwd(t, g):
    return (_mm3("bik,bjk->bij", _mm3("bji,bjk->bik", t, g), t),)


_neumann_inv.defvjp(_neumann_fwd, _neumann_bwd)


def _matmul(a, b, *, mode, name, out_dtype=F32, res=None, scale=1.0):
    if mode == "nn":
        (m, k), (k2, n) = a.shape, b.shape
    elif mode == "nt":
        (m, k), (n, k2) = a.shape, b.shape
    else:
        (k, m), (k2, n) = a.shape, b.shape
    assert k == k2, (name, a.shape, b.shape)
    has_res = res is not None
    tn = _tile(n, 1408, LANES)
    shapes = ((512, 4096), (1024, 2048), (1024, 1024), (512, 512)) if mode == "tn" else (
        (1024, 2816), (1024, 2048), (1024, 1408), (1024, 1024), (512, 512))
    for rows, depth in shapes:
        tm, tk = _tile(m, rows, LANES), _tile(k, depth, LANES)
        need = 2 * tk * (tm * a.dtype.itemsize + tn * b.dtype.itemsize) + tm * tn * (4 + 2 * jnp.dtype(out_dtype).itemsize)
        need += 2 * tm * tn * res.dtype.itemsize if has_res else 0
        if need <= MATMUL_VMEM_BUDGET:
            break
    nk = k // tk
    dims = _MM_DIMS[mode]
    a_spec = pl.BlockSpec((tk, tm), lambda i, j, q: (q, i)) if mode == "tn" else pl.BlockSpec((tm, tk), lambda i, j, q: (i, q))
    b_spec = pl.BlockSpec((tn, tk), lambda i, j, q: (j, q)) if mode == "nt" else pl.BlockSpec((tk, tn), lambda i, j, q: (q, j))
    o_spec = pl.BlockSpec((tm, tn), lambda i, j, q: (i, j))

    def body(*refs):
        a_ref, b_ref = refs[0], refs[1]
        res_ref = refs[2] if has_res else None
        o_ref = refs[3 if has_res else 2]
        acc_ref = refs[-1] if nk > 1 else None

        def finish(acc):
            val = acc * scale if scale != 1.0 else acc
            if has_res:
                val = res_ref[...].astype(F32) + val
            o_ref[...] = val.astype(o_ref.dtype)

        if nk == 1:
            finish(_bdot(a_ref[...], b_ref[...], dims))
        else:
            q = pl.program_id(2)

            @pl.when(q == 0)
            def _():
                acc_ref[...] = jnp.zeros(acc_ref.shape, F32)

            acc_ref[...] += _bdot(a_ref[...], b_ref[...], dims)

            @pl.when(q == nk - 1)
            def _():
                finish(acc_ref[...])

    ins = [a, b] + ([res] if has_res else [])
    in_specs = [a_spec, b_spec] + ([o_spec] if has_res else [])
    return pl.pallas_call(
        body, name=name, grid=(m // tm, n // tn, nk), in_specs=in_specs, out_specs=o_spec,
        out_shape=jax.ShapeDtypeStruct((m, n), out_dtype),
        scratch_shapes=[pltpu.VMEM((tm, tn), F32)] if nk > 1 else [],
        compiler_params=_cp("parallel", "parallel", "arbitrary"),
    )(*ins)


def _read(ref, split, rows):
    rows = slice(None) if ref.shape[0] == 1 else rows
    if split is None:
        return [ref[rows, :].astype(F32)]
    out, off = [], 0
    for w in split:
        out.append(ref[rows, off:off + w].astype(F32))
        off += w
    return out


def _write(ref, vals, split, rows):
    if split is None:
        ref[rows, :] = vals[0].astype(ref.dtype)
        return
    off = 0
    for w, v in zip(split, vals):
        ref[rows, off:off + w] = v.astype(ref.dtype)
        off += w


def _rw_fwd(fn, name, grid, ins, outs):
    n_in = len(ins)

    def body(*refs):
        rows = slice(None)
        args = []
        for r, (_, _, split) in zip(refs[:n_in], ins):
            args += _read(r, split, rows)
        vals = list(fn(*args))
        for r, (_, _, split) in zip(refs[n_in:], outs):
            n = 1 if split is None else len(split)
            _write(r, vals[:n], split, rows)
            vals = vals[n:]

    return pl.pallas_call(
        body, name=name, grid=grid, in_specs=[s for _, s, _ in ins], out_specs=[s for _, s, _ in outs],
        out_shape=[o for o, _, _ in outs], compiler_params=_cp("parallel", "parallel"),
    )(*[a for a, _, _ in ins])


def _rw_bwd(fn, name, grid, ins, cts, grads, add=None, alias=None):
    n_in, n_ct = len(ins), len(cts)
    n_fixed = n_in + n_ct + (1 if add is not None else 0) + (1 if alias is not None else 0)
    arg_pos, pos = [], 0
    for _, _, split in ins:
        n = 1 if split is None else len(split)
        arg_pos.append((pos, n))
        pos += n

    def body(*refs):
        out_refs = refs[n_fixed:]
        rows = slice(None)
        wrt = []
        for idx, _, _, _ in grads:
            p, n = arg_pos[idx]
            wrt += list(range(p, p + n))
        args = []
        for r, (_, _, split) in zip(refs[:n_in], ins):
            args += _read(r, split, rows)
        ct_vals = []
        for r, (_, _, split) in zip(refs[n_in:n_in + n_ct], cts):
            ct_vals += _read(r, split, rows)

        def f(*w):
            full = list(args)
            for p, v in zip(wrt, w):
                full[p] = v
            return tuple(fn(*full))

        _, vjp = jax.vjp(f, *[args[p] for p in wrt])
        g = list(vjp(tuple(ct_vals)))
        first_row = pl.program_id(1) == 0
        first_all = jnp.logical_and(pl.program_id(0) == 0, first_row)
        for gi, (idx, _, _, mode) in enumerate(grads):
            n = arg_pos[idx][1]
            vals, g = g[:n], g[n:]
            o = out_refs[gi]
            if mode == "tile":
                if add is not None and add[2] == gi:
                    vals = [vals[0] + refs[n_in + n_ct][...].astype(F32)]
                _write(o, vals, ins[idx][2], rows)
            else:
                first = first_row if mode == "acc_row" else first_all

                @pl.when(first)
                def _(o=o, val=vals[0]):
                    o[...] = val

                @pl.when(jnp.logical_not(first))
                def _(o=o, val=vals[0]):
                    o[...] += val

    arrays = [a for a, _, _ in ins] + [a for a, _, _ in cts]
    in_specs = [s for _, s, _ in ins] + [s for _, s, _ in cts]
    if add is not None:
        arrays.append(add[0])
        in_specs.append(add[1])
    aliases = {}
    if alias is not None:
        aliases = {len(arrays): alias[1]}
        arrays.append(alias[0])
        in_specs.append(pl.BlockSpec(memory_space=pl.ANY))
    return pl.pallas_call(
        body, name=name, grid=grid, in_specs=in_specs, out_specs=[s for _, _, s, _ in grads],
        out_shape=[o for _, o, _, _ in grads], input_output_aliases=aliases,
        compiler_params=_cp("arbitrary", "arbitrary"),
    )(*arrays)


def _rows(t, w, col=0):
    if callable(col):
        return pl.BlockSpec((t, w), lambda j, i: (i, col(j)))
    return pl.BlockSpec((t, w), lambda j, i: (i, col))


def _par(w, per_col=False):
    return pl.BlockSpec((1, w), (lambda j, i: (0, j)) if per_col else (lambda j, i: (0, 0)))


def _rms_fn(x, w):
    return (x * lax.rsqrt(jnp.mean(x * x, axis=-1, keepdims=True) + EPS) * w,)


def _swiglu_fn(gate, up):
    return (_silu(gate) * up,)


def _ssm_out_fn(y, z, w):
    yg = y * _silu(z)
    return (yg * lax.rsqrt(jnp.mean(yg * yg, axis=-1, keepdims=True) + EPS) * w,)


def _dn_out_fn(o, z, w):
    return (o * lax.rsqrt(jnp.mean(o * o, axis=-1, keepdims=True) + EPS) * w * _silu(z),)


def _merge_fn(gs, gd, ps, pd):
    return (jax.nn.sigmoid(gs) * ps + jax.nn.sigmoid(gd) * pd,)


def _dsmall(a, b, dproj, col_blk, name):
    s = a.shape[0]
    t = _tile(s, 1024, 8)

    def body(a_r, b_r, _, o_r):
        o_r[:, :SMALL_R] = (a_r[...] + b_r[...]).astype(o_r.dtype)
        o_r[:, SMALL_R:] = jnp.zeros((t, SMALL_W - SMALL_R), o_r.dtype)

    row = pl.BlockSpec((t, SMALL_R), lambda i: (i, 0))
    return pl.pallas_call(
        body, name=name, grid=(s // t,), in_specs=[row, row, pl.BlockSpec(memory_space=pl.ANY)],
        out_specs=pl.BlockSpec((t, SMALL_W), lambda i: (i, col_blk)),
        out_shape=jax.ShapeDtypeStruct(dproj.shape, dproj.dtype), input_output_aliases={2: 0},
        compiler_params=_cp("parallel"),
    )(a, b, dproj)


def _conv_fwd(x, x_col0, w, b, *, name, cw, tr):
    s, c = x.shape[0], w.shape[1]
    nr, ncol, hb = s // tr, c // cw, tr // CONV_HALO
    rs, lo = CONV_STRIP, CONV_HALO - (CONV_K - 1)

    def body(x_ref, prev_ref, w_ref, b_ref, o_ref, buf):
        i = pl.program_id(1)
        buf[0:CONV_HALO, :] = jnp.where(i > 0, prev_ref[...], 0.0)
        buf[CONV_HALO:, :] = x_ref[...]
        taps = [w_ref[q:q + 1, :] for q in range(CONV_K)]
        bias = b_ref[...]

        def strip(k, carry):
            r0 = pl.multiple_of(k * rs, rs)
            ext = buf[pl.ds(r0, rs + CONV_HALO), :]
            acc = bias + taps[0] * ext[lo:lo + rs]
            for q in range(1, CONV_K):
                acc = acc + taps[q] * ext[lo + q:lo + q + rs]
            o_ref[pl.ds(r0, rs), :] = _silu(acc)
            return carry

        lax.fori_loop(0, tr // rs, strip, 0)

    return pl.pallas_call(
        body, name=name, grid=(ncol, nr),
        in_specs=[pl.BlockSpec((tr, cw), lambda j, i: (i, x_col0 + j)),
                  pl.BlockSpec((CONV_HALO, cw), lambda j, i: (jnp.maximum(i * hb - 1, 0), x_col0 + j)),
                  pl.BlockSpec((CONV_K, cw), lambda j, i: (0, j)), pl.BlockSpec((1, cw), lambda j, i: (0, j))],
        out_specs=pl.BlockSpec((tr, cw), lambda j, i: (i, j)),
        out_shape=jax.ShapeDtypeStruct((s, c), F32),
        scratch_shapes=[pltpu.VMEM((CONV_HALO + tr, cw), F32)],
        compiler_params=_cp("parallel", "parallel"),
    )(x, x, w, b)


def _conv_bwd(x, x_col0, w, b, dy, dproj, out_col0, *, name, cw, tr):
    s, c = dy.shape
    nr, ncol, hb = s // tr, c // cw, tr // CONV_HALO
    last_hb = s // CONV_HALO - 1
    ext = tr + CONV_HALO
    rs, lo = CONV_STRIP, CONV_HALO - (CONV_K - 1)
    fresh = isinstance(dproj, jax.ShapeDtypeStruct)

    def body(x_ref, prev_ref, next_ref, dy_ref, dyn_ref, w_ref, b_ref, *rest):
        dx_ref, dw_ref, db_ref, xbuf, gbuf = rest[-5:]
        i = pl.program_id(1)
        xbuf[0:CONV_HALO, :] = jnp.where(i > 0, prev_ref[...], 0.0)
        xbuf[CONV_HALO:CONV_HALO + tr, :] = x_ref[...]
        xbuf[CONV_HALO + tr:, :] = next_ref[...]
        taps = [w_ref[q:q + 1, :] for q in range(CONV_K)]
        bias = b_ref[...]

        def dpre(xe, dy, n):
            pre = bias + taps[0] * xe[lo:lo + n]
            for q in range(1, CONV_K):
                pre = pre + taps[q] * xe[lo + q:lo + q + n]
            sg = jax.nn.sigmoid(pre)
            return dy * (sg * (1.0 + pre * (1.0 - sg)))

        def strip1(k, carry):
            r0 = pl.multiple_of(k * rs, rs)
            gbuf[pl.ds(r0, rs), :] = dpre(xbuf[pl.ds(r0, rs + CONV_HALO), :], dy_ref[pl.ds(r0, rs), :].astype(F32), rs)
            return carry

        lax.fori_loop(0, tr // rs, strip1, 0)
        gbuf[tr:, :] = dpre(xbuf[tr:, :], jnp.where(i < nr - 1, dyn_ref[...].astype(F32), 0.0), CONV_HALO)

        def fold(v):
            acc = v[0:8]
            for a in range(1, rs // 8):
                acc = acc + v[8 * a:8 * a + 8]
            return acc

        def strip2(k, carry):
            r0 = pl.multiple_of(k * rs, rs)
            ge = gbuf[pl.ds(r0, rs + CONV_HALO), :]
            x_own = xbuf[pl.ds(r0 + CONV_HALO, rs), :]
            dx = jnp.zeros((rs, cw), F32)
            new = []
            for q in range(CONV_K):
                g_q = ge[CONV_K - 1 - q:CONV_K - 1 - q + rs]
                dx = dx + taps[q] * g_q
                new.append(carry[q] + fold(x_own * g_q))
            dx_ref[pl.ds(r0, rs), :] = dx.astype(dx_ref.dtype)
            return tuple(new) + (carry[CONV_K] + fold(ge[0:rs]),)

        sums = lax.fori_loop(0, tr // rs, strip2, tuple(jnp.zeros((8, cw), F32) for _ in range(CONV_K + 1)))
        dws = [jnp.sum(sums[q], axis=0, keepdims=True) for q in range(CONV_K)]
        dbv = jnp.sum(sums[CONV_K], axis=0, keepdims=True)

        @pl.when(i == 0)
        def _():
            for q in range(CONV_K):
                dw_ref[q:q + 1, :] = dws[q]
            db_ref[...] = dbv

        @pl.when(i > 0)
        def _():
            for q in range(CONV_K):
                dw_ref[q:q + 1, :] += dws[q]
            db_ref[...] += dbv

    xmap = lambda j, i: (i, x_col0 + j)
    ins = [x, x, x, dy, dy, w, b]
    in_specs = [pl.BlockSpec((tr, cw), xmap),
                pl.BlockSpec((CONV_HALO, cw), lambda j, i: (jnp.maximum(i * hb - 1, 0), x_col0 + j)),
                pl.BlockSpec((CONV_HALO, cw), lambda j, i: (jnp.minimum((i + 1) * hb, last_hb), x_col0 + j)),
                pl.BlockSpec((tr, cw), lambda j, i: (i, j)),
                pl.BlockSpec((CONV_HALO, cw), lambda j, i: (jnp.minimum((i + 1) * hb, last_hb), j)),
                pl.BlockSpec((CONV_K, cw), lambda j, i: (0, j)), pl.BlockSpec((1, cw), lambda j, i: (0, j))]
    aliases = {}
    if not fresh:
        aliases = {len(ins): 0}
        ins.append(dproj)
        in_specs.append(pl.BlockSpec(memory_space=pl.ANY))
    return pl.pallas_call(
        body, name=name, grid=(ncol, nr), in_specs=in_specs,
        out_specs=[pl.BlockSpec((tr, cw), lambda j, i: (i, out_col0 + j)),
                   pl.BlockSpec((CONV_K, cw), lambda j, i: (0, j)), pl.BlockSpec((1, cw), lambda j, i: (0, j))],
        out_shape=[jax.ShapeDtypeStruct(dproj.shape, dproj.dtype), jax.ShapeDtypeStruct((CONV_K, c), F32),
                   jax.ShapeDtypeStruct((1, c), F32)],
        scratch_shapes=[pltpu.VMEM((CONV_HALO + ext, cw), F32), pltpu.VMEM((ext, cw), F32)],
        input_output_aliases=aliases, compiler_params=_cp("arbitrary", "arbitrary"),
    )(*ins)


def _ssd_chunk(xs, bm, cm, small, dtb, alog, dsk, st, g, *, hg, p):
    l, w = small.shape
    per = LANES // p
    lane_w = lax.broadcasted_iota(jnp.int32, (1, w), 1)
    lane = lax.broadcasted_iota(jnp.int32, (1, LANES), 1)
    r = lax.broadcasted_iota(jnp.int32, (l, l), 0)
    c = lax.broadcasted_iota(jnp.int32, (l, l), 1)
    tri = r >= c
    eye = (r == c).astype(F32)
    last = (lax.broadcasted_iota(jnp.int32, (l, 1), 0) == l - 1).astype(F32)
    dt_all = _softplus(small + dtb)
    acum_all = _pmm(tri.astype(F32)[None], (dt_all * (-jnp.exp(alog)))[None])[0]
    cb = _bdot(cm, bm, _NT)
    ys, sts = [], []
    for q, (x_q, st_q) in enumerate(zip(xs, st)):
        dt_e = jnp.zeros((l, LANES), F32)
        acum_e = jnp.zeros((l, LANES), F32)
        d_e = jnp.zeros((1, LANES), F32)
        decays, masks = [], []
        for jj in range(per):
            mh = (lane_w == g * hg + q * per + jj).astype(F32)
            mj = jnp.logical_and(lane >= jj * p, lane < (jj + 1) * p).astype(F32)
            ac = jnp.sum(acum_all * mh, axis=1, keepdims=True)
            dt_e = dt_e + jnp.sum(dt_all * mh, axis=1, keepdims=True) * mj
            acum_e = acum_e + ac * mj
            d_e = d_e + jnp.sum(dsk * mh, axis=1, keepdims=True) * mj
            seg = ac - jnp.sum(ac * eye, axis=0, keepdims=True)
            decays.append(jnp.where(tri, jnp.exp(jnp.where(tri, seg, 0.0)), 0.0))
            masks.append(mj)
        xdt = x_q * dt_e
        y = x_q * d_e
        for dec, mj in zip(decays, masks):
            y = y + _bdot(cb * dec, xdt * mj, _NN)
        a_last = jnp.sum(acum_e * last, axis=0, keepdims=True)
        ys.append(y + jnp.exp(acum_e) * _bdot(cm, st_q, _NN))
        sts.append(st_q * jnp.exp(a_last) + _bdot(bm, xdt * jnp.exp(a_last - acum_e), _TN))
    return tuple(ys) + tuple(sts)


def _pieces(ref, lead=()):
    return [ref[lead + (slice(None), slice(q * LANES, (q + 1) * LANES))] for q in range(ref.shape[-1] // LANES)]


def _ssd_specs(cfg, rev):
    l, hp, n, g = SSM_CHUNK, cfg["hp"], cfg["n"], SSM_GROUPS
    nc = cfg["s"] // l
    cc = (lambda c: nc - 1 - c) if rev else (lambda c: c)
    nb = cfg["inner"] // n
    par = pl.BlockSpec((1, SMALL_R), lambda c, q: (0, 0))
    specs = [pl.BlockSpec((l, hp), lambda c, q: (cc(c), q)),
             pl.BlockSpec((l, n), lambda c, q: (cc(c), nb + q)),
             pl.BlockSpec((l, n), lambda c, q: (cc(c), nb + g + q)),
             pl.BlockSpec((l, SMALL_R), lambda c, q: (cc(c), cfg["small_rblk"])), par, par, par]
    st_spec = pl.BlockSpec((None, None, n, hp), lambda c, q: (cc(c), q, 0, 0))
    y_spec = pl.BlockSpec((l, hp), lambda c, q: (cc(c), q))
    return specs, st_spec, y_spec, nc


def _ssd_fwd(xbc_c, proj, dtb, alog, dsk, cfg, name):
    specs, st_spec, y_spec, nc = _ssd_specs(cfg, False)
    fn = functools.partial(_ssd_chunk, hg=cfg["hg"], p=cfg["p"])
    npc = cfg["hp"] // LANES

    def body(xs, bm, cm, sm, dtb_r, alog_r, dsk_r, y_ref, sts_ref, st):
        c, g = pl.program_id(0), pl.program_id(1)

        @pl.when(c == 0)
        def _():
            st[g] = jnp.zeros(st.shape[1:], F32)

        sts_ref[...] = st[g]
        out = fn(_pieces(xs), bm[...], cm[...], sm[...], dtb_r[...], alog_r[...], dsk_r[...], _pieces(st, (g,)), g)
        for q in range(npc):
            y_ref[:, q * LANES:(q + 1) * LANES] = out[q]
            st[g, :, q * LANES:(q + 1) * LANES] = out[npc + q]

    return pl.pallas_call(
        body, name=name, grid=(nc, SSM_GROUPS), in_specs=specs, out_specs=[y_spec, st_spec],
        out_shape=[jax.ShapeDtypeStruct((cfg["s"], cfg["inner"]), F32),
                   jax.ShapeDtypeStruct((nc, SSM_GROUPS, cfg["n"], cfg["hp"]), F32)],
        scratch_shapes=[pltpu.VMEM((SSM_GROUPS, cfg["n"], cfg["hp"]), F32)],
        compiler_params=_cp("arbitrary", "arbitrary"),
    )(xbc_c, xbc_c, xbc_c, proj, dtb, alog, dsk)


def _ssd_bwd(xbc_c, proj, dtb, alog, dsk, states, dy, cfg, name):
    specs, st_spec, y_spec, nc = _ssd_specs(cfg, True)
    l, n, gn = SSM_CHUNK, cfg["n"], SSM_GROUPS * cfg["n"]
    fn = functools.partial(_ssd_chunk, hg=cfg["hg"], p=cfg["p"])
    npc = cfg["hp"] // LANES
    rc = lambda c: nc - 1 - c

    def body(xs, bm, cm, sm, dtb_r, alog_r, dsk_r, sts_ref, dy_ref, dxs, dbm, dcm, dsm, ddtb, dalog, ddsk, dst):
        c, g = pl.program_id(0), pl.program_id(1)

        @pl.when(c == 0)
        def _():
            dst[g] = jnp.zeros(dst.shape[1:], F32)

        def f(*a):
            return fn(a[:npc], *a[npc:npc + 6], a[npc + 6:], g)

        _, vjp = jax.vjp(f, *_pieces(xs), bm[...], cm[...], sm[...], dtb_r[...], alog_r[...], dsk_r[...], *_pieces(sts_ref))
        grads = vjp(tuple(_pieces(dy_ref)) + tuple(_pieces(dst, (g,))))
        gb, gc, gs, g1, g2, g3 = grads[npc:npc + 6]
        for q in range(npc):
            dxs[:, q * LANES:(q + 1) * LANES] = grads[q]
            dst[g, :, q * LANES:(q + 1) * LANES] = grads[npc + 6 + q]
        dbm[...] = gb
        dcm[...] = gc

        @pl.when(g == 0)
        def _():
            dsm[...] = gs

        @pl.when(g > 0)
        def _():
            dsm[...] += gs

        first = jnp.logical_and(c == 0, g == 0)

        @pl.when(first)
        def _():
            ddtb[...] = g1
            dalog[...] = g2
            ddsk[...] = g3

        @pl.when(jnp.logical_not(first))
        def _():
            ddtb[...] += g1
            dalog[...] += g2
            ddsk[...] += g3

    par = pl.BlockSpec((1, SMALL_R), lambda c, q: (0, 0))
    return pl.pallas_call(
        body, name=name, grid=(nc, SSM_GROUPS), in_specs=specs + [st_spec, y_spec],
        out_specs=[y_spec, pl.BlockSpec((l, n), lambda c, q: (rc(c), q)), pl.BlockSpec((l, n), lambda c, q: (rc(c), q)),
                   pl.BlockSpec((l, SMALL_R), lambda c, q: (rc(c), 0)), par, par, par],
        out_shape=[jax.ShapeDtypeStruct((cfg["s"], cfg["inner"]), F32), jax.ShapeDtypeStruct((cfg["s"], gn), F32),
                   jax.ShapeDtypeStruct((cfg["s"], gn), F32), jax.ShapeDtypeStruct((cfg["s"], SMALL_R), F32),
                   jax.ShapeDtypeStruct((1, SMALL_R), F32), jax.ShapeDtypeStruct((1, SMALL_R), F32),
                   jax.ShapeDtypeStruct((1, SMALL_R), F32)],
        scratch_shapes=[pltpu.VMEM((SSM_GROUPS, cfg["n"], cfg["hp"]), F32)],
        compiler_params=_cp("arbitrary", "arbitrary"),
    )(xbc_c, xbc_c, xbc_c, proj, dtb, alog, dsk, states, dy)


def _gdn_prep(q, k, v, small, alog, dtb, h, *, boff, aoff):
    c = DN_CHUNK
    rr, dk = q.shape
    nb, w_ = rr // c, small.shape[1]
    lane = lax.broadcasted_iota(jnp.int32, (1, w_), 1)
    g_all = -jnp.exp(alog) * _softplus(small + dtb)
    r = lax.broadcasted_iota(jnp.int32, (c, c), 0)
    cc = lax.broadcasted_iota(jnp.int32, (c, c), 1)
    incl, strict = (r >= cc)[None], (r > cc)[None]
    eye = (r == cc).astype(F32)[None]
    tri_b = jnp.broadcast_to((r >= cc).astype(F32)[None], (nb, c, c))
    gcum_all = _pmm(tri_b, g_all.reshape(nb, c, w_))
    gc = jnp.sum(gcum_all * (lane == aoff + h).astype(F32)[None], axis=2, keepdims=True)
    beta = jnp.sum(jax.nn.sigmoid(small) * (lane == boff + h).astype(F32), axis=1, keepdims=True).reshape(nb, c, 1)
    qn = (q * lax.rsqrt(jnp.sum(q * q, axis=1, keepdims=True) + EPS) * (dk ** -0.5)).reshape(nb, c, dk)
    kn = (k * lax.rsqrt(jnp.sum(k * k, axis=1, keepdims=True) + EPS)).reshape(nb, c, dk)
    v3 = v.reshape(nb, c, v.shape[1])
    g_row = jnp.sum(gc * eye, axis=1, keepdims=True)
    decay = jnp.where(incl, jnp.exp(jnp.where(incl, gc - g_row, 0.0)), 0.0)
    kb = kn * beta
    kk = jnp.einsum("bik,bjk->bij", kb.astype(BF16), kn.astype(BF16), preferred_element_type=F32)
    neg_m = jnp.where(strict, -kk * decay, 0.0)
    t_inv = _neumann_inv(neg_m)
    eg = jnp.exp(gc)
    u = _pmm(t_inv, v3 * beta)
    w = _pmm(t_inv, kb * eg)
    a_qk = jnp.einsum("bik,bjk->bij", qn.astype(BF16), kn.astype(BF16), preferred_element_type=F32) * decay
    last = (lax.broadcasted_iota(jnp.int32, (1, c, 1), 1) == c - 1).astype(F32)
    g_last = jnp.sum(gc * last, axis=1, keepdims=True)
    q_dec = qn * eg
    k_dec = kn * jnp.exp(g_last - gc)
    gl = jnp.broadcast_to(g_last, (nb, 1, LANES))
    return (u.reshape(rr, -1), w.reshape(rr, dk), q_dec.reshape(rr, dk), k_dec.reshape(rr, dk), a_qk, gl)


def _gdn_prep_specs(cfg):
    rr, hd, dk, dv = cfg["prep_rows"], cfg["hd"], cfg["dk"], cfg["dv"]
    nb = rr // DN_CHUNK
    ins = [pl.BlockSpec((rr, dk), lambda i, h: (i, h)), pl.BlockSpec((rr, dk), lambda i, h: (i, hd + h)),
           pl.BlockSpec((rr, dv), lambda i, h: (i, (2 * hd * dk) // dv + h)),
           pl.BlockSpec((rr, SMALL_R), lambda i, h: (i, cfg["small_rblk"])),
           pl.BlockSpec((1, SMALL_R), lambda i, h: (0, 0)), pl.BlockSpec((1, SMALL_R), lambda i, h: (0, 0))]
    hs = lambda d: pl.BlockSpec((None, rr, d), lambda i, h: (h, i, 0))
    outs = [hs(dv), hs(dk), hs(dk), hs(dk), pl.BlockSpec((None, nb, DN_CHUNK, DN_CHUNK), lambda i, h: (h, i, 0, 0)),
            pl.BlockSpec((None, nb, 1, LANES), lambda i, h: (h, i, 0, 0))]
    s, nch = cfg["s"], cfg["s"] // DN_CHUNK
    shapes = [jax.ShapeDtypeStruct((hd, s, dv), F32)] + [jax.ShapeDtypeStruct((hd, s, dk), F32)] * 3 + [
        jax.ShapeDtypeStruct((hd, nch, DN_CHUNK, DN_CHUNK), F32), jax.ShapeDtypeStruct((hd, nch, 1, LANES), F32)]
    return ins, outs, shapes


def _gdn_prep_fwd(qkv_c, proj, alog, dtb, cfg, name):
    ins, outs, shapes = _gdn_prep_specs(cfg)
    fn = functools.partial(_gdn_prep, boff=cfg["hs"], aoff=cfg["hs"] + cfg["hd"])

    def body(q, k, v, sm, al, db, *o):
        vals = fn(q[...], k[...], v[...], sm[...], al[...], db[...], pl.program_id(1))
        for ref, val in zip(o, vals):
            ref[...] = val

    return pl.pallas_call(
        body, name=name, grid=(cfg["s"] // cfg["prep_rows"], cfg["hd"]), in_specs=ins, out_specs=outs, out_shape=shapes,
        compiler_params=_cp("parallel", "parallel"),
    )(qkv_c, qkv_c, qkv_c, proj, alog, dtb)


def _gdn_prep_bwd(qkv_c, proj, alog, dtb, cts, cfg, name):
    ins, outs, _ = _gdn_prep_specs(cfg)
    rr, hd, dk, dv, s = cfg["prep_rows"], cfg["hd"], cfg["dk"], cfg["dv"], cfg["s"]
    fn = functools.partial(_gdn_prep, boff=cfg["hs"], aoff=cfg["hs"] + cfg["hd"])

    def body(q, k, v, sm, al, db, c0, c1, c2, c3, c4, c5, dq, dkk, dvv, dsm, dal, ddb):
        i, h = pl.program_id(0), pl.program_id(1)
        f = lambda *a: fn(*a, h)
        _, vjp = jax.vjp(f, q[...], k[...], v[...], sm[...], al[...], db[...])
        gq, gk, gv, gs, ga, gd = vjp((c0[...], c1[...], c2[...], c3[...], c4[...], c5[...]))
        dq[...] = gq
        dkk[...] = gk
        dvv[...] = gv

        @pl.when(h == 0)
        def _():
            dsm[...] = gs

        @pl.when(h > 0)
        def _():
            dsm[...] += gs

        first = jnp.logical_and(i == 0, h == 0)

        @pl.when(first)
        def _():
            dal[...] = ga
            ddb[...] = gd

        @pl.when(jnp.logical_not(first))
        def _():
            dal[...] += ga
            ddb[...] += gd

    par = pl.BlockSpec((1, SMALL_R), lambda i, h: (0, 0))
    return pl.pallas_call(
        body, name=name, grid=(s // rr, hd), in_specs=ins + outs,
        out_specs=[pl.BlockSpec((rr, dk), lambda i, h: (i, h)), pl.BlockSpec((rr, dk), lambda i, h: (i, h)),
                   pl.BlockSpec((rr, dv), lambda i, h: (i, h)), pl.BlockSpec((rr, SMALL_R), lambda i, h: (i, 0)), par, par],
        out_shape=[jax.ShapeDtypeStruct((s, hd * dk), F32), jax.ShapeDtypeStruct((s, hd * dk), F32),
                   jax.ShapeDtypeStruct((s, hd * dv), F32), jax.ShapeDtypeStruct((s, SMALL_R), F32),
                   jax.ShapeDtypeStruct((1, SMALL_R), F32), jax.ShapeDtypeStruct((1, SMALL_R), F32)],
        compiler_params=_cp("arbitrary", "arbitrary"),
    )(qkv_c, qkv_c, qkv_c, proj, alog, dtb, *cts)


def _gdn_scan_chunk(qd, aq, u, w, kd, gl, st):
    bm = lambda spec, a, b: jnp.einsum(spec, a.astype(BF16), b.astype(BF16), preferred_element_type=F32)
    v_new = u - bm("hck,hkv->hcv", w, st)
    o = bm("hck,hkv->hcv", qd, st) + bm("hij,hjv->hiv", aq, v_new)
    st_new = st * jnp.exp(gl) + bm("hck,hcv->hkv", kd, v_new)
    return o, st_new


def _gdn_scan_specs(cfg, rev):
    rr, hd, dk, dv = cfg["scan_rows"], cfg["hd"], cfg["dk"], cfg["dv"]
    nb, nblk = rr // DN_CHUNK, cfg["s"] // rr
    ii = (lambda i: nblk - 1 - i) if rev else (lambda i: i)
    hs = lambda d: pl.BlockSpec((hd, rr, d), lambda i: (0, ii(i), 0))
    ins = [hs(dk), pl.BlockSpec((hd, nb, DN_CHUNK, DN_CHUNK), lambda i: (0, ii(i), 0, 0)), hs(dv), hs(dk), hs(dk),
           pl.BlockSpec((hd, nb, 1, LANES), lambda i: (0, ii(i), 0, 0))]
    st_spec = pl.BlockSpec((nb, hd, dk, dv), lambda i: (ii(i), 0, 0, 0))
    return ins, hs(dv), st_spec, nb, nblk


def _gdn_scan_fwd(qd, aq, u, w, kd, gl, cfg, name):
    ins, o_spec, st_spec, nb, nblk = _gdn_scan_specs(cfg, False)
    hd, dk, dv, c = cfg["hd"], cfg["dk"], cfg["dv"], DN_CHUNK

    def body(qd_r, aq_r, u_r, w_r, kd_r, gl_r, o_r, sts_r, st):
        @pl.when(pl.program_id(0) == 0)
        def _():
            st[...] = jnp.zeros(st.shape, F32)

        s = st[...]
        for j in range(nb):
            rows = slice(j * c, (j + 1) * c)
            sts_r[j] = s
            o, s = _gdn_scan_chunk(qd_r[:, rows, :], aq_r[:, j], u_r[:, rows, :], w_r[:, rows, :], kd_r[:, rows, :],
                                   gl_r[:, j], s)
            o_r[:, rows, :] = o
        st[...] = s

    return pl.pallas_call(
        body, name=name, grid=(nblk,), in_specs=ins, out_specs=[o_spec, st_spec],
        out_shape=[jax.ShapeDtypeStruct((hd, cfg["s"], dv), F32),
                   jax.ShapeDtypeStruct((cfg["s"] // c, hd, dk, dv), F32)],
        scratch_shapes=[pltpu.VMEM((hd, dk, dv), F32)], compiler_params=_cp("arbitrary"),
    )(qd, aq, u, w, kd, gl)


def _gdn_scan_bwd(qd, aq, u, w, kd, gl, states, do, cfg, name):
    ins, o_spec, st_spec, nb, nblk = _gdn_scan_specs(cfg, True)
    hd, dk, dv, c = cfg["hd"], cfg["dk"], cfg["dv"], DN_CHUNK

    def body(qd_r, aq_r, u_r, w_r, kd_r, gl_r, sts_r, do_r, dqd, daq, du, dw, dkd, dgl, dst):
        @pl.when(pl.program_id(0) == 0)
        def _():
            dst[...] = jnp.zeros(dst.shape, F32)

        ds = dst[...]
        for j in reversed(range(nb)):
            rows = slice(j * c, (j + 1) * c)
            _, vjp = jax.vjp(_gdn_scan_chunk, qd_r[:, rows, :], aq_r[:, j], u_r[:, rows, :], w_r[:, rows, :],
                             kd_r[:, rows, :], gl_r[:, j], sts_r[j])
            g0, g1, g2, g3, g4, g5, ds = vjp((do_r[:, rows, :], ds))
            dqd[:, rows, :] = g0
            daq[:, j] = g1
            du[:, rows, :] = g2
            dw[:, rows, :] = g3
            dkd[:, rows, :] = g4
            dgl[:, j] = g5
        dst[...] = ds

    s, nch = cfg["s"], cfg["s"] // c
    return pl.pallas_call(
        body, name=name, grid=(nblk,), in_specs=ins + [st_spec, o_spec], out_specs=ins,
        out_shape=[jax.ShapeDtypeStruct((hd, s, dk), F32), jax.ShapeDtypeStruct((hd, nch, c, c), F32),
                   jax.ShapeDtypeStruct((hd, s, dv), F32), jax.ShapeDtypeStruct((hd, s, dk), F32),
                   jax.ShapeDtypeStruct((hd, s, dk), F32), jax.ShapeDtypeStruct((hd, nch, 1, LANES), F32)],
        scratch_shapes=[pltpu.VMEM((hd, dk, dv), F32)], compiler_params=_cp("arbitrary"),
    )(qd, aq, u, w, kd, gl, states, do)


def _loss_head(h, target, w, name):
    s, d = h.shape
    t = _tile(s, 512, 8)

    def body(h_r, t_r, w_r, loss_r, dh_r, dw_r):
        i = pl.program_id(0)
        (y,), vjp = jax.vjp(_rms_fn, h_r[...], w_r[...])
        err = y - t_r[...]
        part = 0.5 * jnp.sum(jnp.mean(err * err, axis=-1, keepdims=True))
        gh, gw = vjp((err * (1.0 / d),))
        dh_r[...] = gh

        @pl.when(i == 0)
        def _():
            loss_r[...] = jnp.zeros(loss_r.shape, F32) + part
            dw_r[...] = gw

        @pl.when(i > 0)
        def _():
            loss_r[...] += part
            dw_r[...] += gw

    row = pl.BlockSpec((t, d), lambda i: (i, 0))
    return pl.pallas_call(
        body, name=name, grid=(s // t,), in_specs=[row, row, pl.BlockSpec((1, d), lambda i: (0, 0))],
        out_specs=[pl.BlockSpec((8, LANES), lambda i: (0, 0)), row, pl.BlockSpec((1, d), lambda i: (0, 0))],
        out_shape=[jax.ShapeDtypeStruct((8, LANES), F32), jax.ShapeDtypeStruct((s, d), F32), jax.ShapeDtypeStruct((1, d), F32)],
        compiler_params=_cp("arbitrary"),
    )(h, target, w)


def _adamw_math(g, w, m, v):
    m = ADAM_B1 * m + (1.0 - ADAM_B1) * g
    v = ADAM_B2 * v + (1.0 - ADAM_B2) * jnp.square(g)
    m_hat = m / (1.0 - ADAM_B1 ** ADAM_STEP)
    v_hat = v / (1.0 - ADAM_B2 ** ADAM_STEP)
    delta = -ADAM_LR * (m_hat / (jnp.sqrt(v_hat) + ADAM_EPS) + ADAM_WD * w)
    return delta, m, v


def _adamw_sum(parts, w, m, v, name):
    nl, r, c = w.shape
    tr = _tile(r, 64, 8)

    def body(p_r, w_r, m_r, v_r, g_o, d_o, m_o, v_o):
        g = p_r[0].astype(F32)
        for q in range(1, N_DEV):
            g = g + p_r[q].astype(F32)
        delta, mn, vn = _adamw_math(g, w_r[...], m_r[...], v_r[...])
        g_o[...] = g
        d_o[...] = delta
        m_o[...] = mn
        v_o[...] = vn

    blk = pl.BlockSpec((None, tr, c), lambda a, i: (a, i, 0))
    return pl.pallas_call(
        body, name=name, grid=(nl, r // tr),
        in_specs=[pl.BlockSpec((N_DEV, None, tr, c), lambda a, i: (0, a, i, 0)), blk, blk, blk],
        out_specs=[blk] * 4, out_shape=[jax.ShapeDtypeStruct(w.shape, F32)] * 4, compiler_params=_cp("parallel", "parallel"),
    )(parts, w, m, v)


def _sum_parts(parts, name):
    _, r, c = parts.shape
    tr = _tile(r, 512, 8)

    def body(p_r, o_r):
        g = p_r[0]
        for q in range(1, N_DEV):
            g = g + p_r[q]
        o_r[...] = g

    return pl.pallas_call(
        body, name=name, grid=(r // tr,), in_specs=[pl.BlockSpec((N_DEV, tr, c), lambda i: (0, i, 0))],
        out_specs=pl.BlockSpec((tr, c), lambda i: (i, 0)), out_shape=jax.ShapeDtypeStruct((r, c), F32),
        compiler_params=_cp("parallel"),
    )(parts)


def _adamw_flat(g, w, m, v, name):
    r, c = w.shape
    tr = _tile(r, 512, 8)

    def body(g_r, w_r, m_r, v_r, d_o, m_o, v_o):
        delta, mn, vn = _adamw_math(g_r[...], w_r[...], m_r[...], v_r[...])
        d_o[...] = delta
        m_o[...] = mn
        v_o[...] = vn

    blk = pl.BlockSpec((tr, c), lambda i: (i, 0))
    return pl.pallas_call(
        body, name=name, grid=(r // tr,), in_specs=[blk] * 4, out_specs=[blk] * 3,
        out_shape=[jax.ShapeDtypeStruct(w.shape, F32)] * 3, compiler_params=_cp("parallel"),
    )(g, w, m, v)


def _exchange(arrays, *, scatter, name):
    nt = len(arrays)
    blocks = [a.shape[1:] if scatter else a.shape for a in arrays]

    def body(*refs):
        srcs, dsts = refs[:nt], refs[nt:2 * nt]
        send_sems, recv_sems, local_sems = refs[2 * nt:]
        x, y, c = lax.axis_index("x"), lax.axis_index("y"), lax.axis_index("c")
        me = 4 * x + 2 * y + c
        copies = []
        for t in range(nt):
            own = srcs[t].at[me] if scatter else srcs[t]
            loc = pltpu.make_async_copy(own, dsts[t].at[me], local_sems.at[t])
            loc.start()
            copies.append(loc)
        remote = []
        for k in range(1, N_DEV):
            px = (1 - x) if (k & 4) else x
            py = (1 - y) if (k & 2) else y
            pc = (1 - c) if (k & 1) else c
            peer = 4 * px + 2 * py + pc
            for t in range(nt):
                src = srcs[t].at[peer] if scatter else srcs[t]
                cp = pltpu.make_async_remote_copy(
                    src_ref=src, dst_ref=dsts[t].at[me], send_sem=send_sems.at[t, k - 1], recv_sem=recv_sems.at[t, k - 1],
                    device_id=(px, py, pc), device_id_type=MESH)
                cp.start()
                arrive = pltpu.make_async_remote_copy(
                    src_ref=src, dst_ref=dsts[t].at[peer], send_sem=send_sems.at[t, k - 1], recv_sem=recv_sems.at[t, k - 1],
                    device_id=(px, py, pc), device_id_type=MESH)
                remote.append((cp, arrive))
        for cp, arrive in remote:
            arrive.wait_recv()
        for cp, arrive in remote:
            cp.wait_send()
        for loc in copies:
            loc.wait()

    return pl.pallas_call(
        body, name=name, in_specs=[pl.BlockSpec(memory_space=pl.ANY)] * nt,
        out_specs=[pl.BlockSpec(memory_space=pl.ANY)] * nt,
        out_shape=[jax.ShapeDtypeStruct((N_DEV,) + tuple(b), a.dtype) for a, b in zip(arrays, blocks)],
        scratch_shapes=[pltpu.SemaphoreType.DMA((nt, N_DEV - 1)), pltpu.SemaphoreType.DMA((nt, N_DEV - 1)),
                        pltpu.SemaphoreType.DMA((nt,))],
    )(*arrays)


def _config(x, ffn1_w_out, ssm_conv_b, ssm_dt_bias, ssm_norm, dn_conv_w, dn_dt_bias, dn_norm, dn_w_branch):
    s, d = x.shape[-2], x.shape[-1]
    f = ffn1_w_out.shape[1] * N_DEV
    cs, hs, inner = ssm_conv_b.shape[1], ssm_dt_bias.shape[1], ssm_norm.shape[1]
    gn = (cs - inner) // 2
    hd, dv = dn_dt_bias.shape[1], dn_norm.shape[1]
    cd = dn_conv_w.shape[2] * N_DEV
    vd = hd * dv
    kd = (cd - vd) // 2
    cfg = dict(s=s, d=d, f=f, cs=cs, hs=hs, inner=inner, gn=gn, n=gn // SSM_GROUPS, hg=hs // SSM_GROUPS, p=inner // hs,
               hp=inner // SSM_GROUPS, hd=hd, dv=dv, dk=kd // hd, cd=cd, vd=vd, kd=kd)
    offs, o = {}, 0
    for nm, wd in (("xbc", cs), ("qkv", cd), ("gates", 2 * d), ("zs", inner), ("zd", vd), ("small", SMALL_W)):
        offs[nm] = o
        o += wd
    cfg["offs"], cfg["pw"] = offs, o
    cfg["cw"] = 512
    cfg["small_blk"], cfg["small_rblk"] = offs["small"] // SMALL_W, offs["small"] // SMALL_R
    cfg["prep_rows"] = min(s, 8 * DN_CHUNK)
    cfg["scan_rows"] = min(s, 4 * DN_CHUNK)
    cfg["in_split"] = (inner, cs, hs, cd, vd, hd, hd, d, d)
    assert hs + 2 * hd <= SMALL_R and cfg["dk"] == dv and dv == LANES and LANES % cfg["p"] == 0 and cfg["hp"] % LANES == 0
    assert offs["qkv"] % cfg["cw"] == 0 and offs["gates"] % (2 * d) == 0 and offs["zs"] % cfg["hp"] == 0
    assert offs["zd"] % dv == 0 and all(wd % cfg["cw"] == 0 for wd in (inner, gn, kd, vd)) and inner % cfg["n"] == 0
    return cfg


def _permute_w_in(w, cfg):
    pts = [0]
    for wd in cfg["in_split"]:
        pts.append(pts[-1] + wd)
    z_s, xbc, dt, qkv, z_d, b_d, a_d, g_s, g_d = [w[:, pts[i]:pts[i + 1]] for i in range(9)]
    pad = jnp.zeros((w.shape[0], SMALL_W - dt.shape[1] - b_d.shape[1] - a_d.shape[1]), w.dtype)
    return jnp.concatenate([xbc, qkv, g_s, g_d, z_s, z_d, dt, b_d, a_d, pad], axis=1)


def _unpermute_w_in(g, cfg):
    o, d = cfg["offs"], cfg["d"]
    hs, hd = cfg["hs"], cfg["hd"]
    sm = g[:, o["small"]:]
    return jnp.concatenate([
        g[:, o["zs"]:o["zs"] + cfg["inner"]], g[:, o["xbc"]:o["xbc"] + cfg["cs"]], sm[:, :hs],
        g[:, o["qkv"]:o["qkv"] + cfg["cd"]], g[:, o["zd"]:o["zd"] + cfg["vd"]], sm[:, hs:hs + hd], sm[:, hs + hd:hs + 2 * hd],
        g[:, o["gates"]:o["gates"] + d], g[:, o["gates"] + d:o["gates"] + 2 * d]], axis=1)


def _lane_row(v, off):
    return jnp.pad(v.astype(F32), (off, SMALL_R - off - v.shape[0]))[None]


def _pack(arrs):
    flat = []
    for a in arrs:
        v = a.reshape(-1)
        flat.append(jnp.pad(v, (0, (-v.shape[0]) % LANES)))
    v = jnp.concatenate(flat)
    v = jnp.pad(v, (0, (-v.shape[0]) % (8 * LANES)))
    return v.reshape(-1, LANES)


def _unpack(packed, shapes):
    v, out, o = packed.reshape(-1), [], 0
    for sh in shapes:
        n = math.prod(sh)
        out.append(v[o:o + n].reshape(sh))
        o += n + (-n) % LANES
    return out


def _cols_gathered(g):
    nd, nl, r, c = g.shape
    return jnp.transpose(g, (1, 2, 0, 3)).reshape(nl, r, nd * c)


def _rows_gathered(g):
    nd, nl, r, c = g.shape
    return jnp.transpose(g, (1, 0, 2, 3)).reshape(nl, nd * r, c)


def _cols_scatter(g):
    nl, r, c = g.shape
    return jnp.transpose(g.reshape(nl, r, N_DEV, c // N_DEV), (2, 0, 1, 3))


def _rows_scatter(g):
    nl, r, c = g.shape
    return jnp.transpose(g.reshape(nl, N_DEV, r // N_DEV, c), (1, 0, 2, 3))


def _ffn_fwd(h, nw, w_in, w_out, cfg, tag):
    s, d, f = cfg["s"], cfg["d"], cfg["f"]
    t = _tile(s, 512, 8)
    xn, = _rw_fwd(_rms_fn, f"{tag}_norm", (1, s // t), [(h, _rows(t, d), None), (nw, _par(d), None)],
                  [(jax.ShapeDtypeStruct((s, d), BF16), _rows(t, d), None)])
    gu = _matmul(xn, w_in, mode="nn", name=f"{tag}_in")
    t2 = _tile(s, 256, 8)
    act, = _rw_fwd(_swiglu_fn, f"{tag}_act", (1, s // t2), [(gu, _rows(t2, 2 * f), [f, f])],
                   [(jax.ShapeDtypeStruct((s, f), BF16), _rows(t2, f), None)])
    h_out = _matmul(act, w_out, mode="nn", name=f"{tag}_out", res=h, scale=0.5)
    return h_out, dict(h=h, xn=xn, gu=gu, act=act)


def _ffn_bwd(gh, r, nw, w_in, w_out, cfg, tag):
    s, d, f = cfg["s"], cfg["d"], cfg["f"]
    d_wout = _matmul(r["act"], gh, mode="tn", name=f"{tag}_dwout", scale=0.5)
    d_act = _matmul(gh, w_out, mode="nt", name=f"{tag}_dact", scale=0.5)
    t2 = _tile(s, 256, 8)
    d_gu, = _rw_bwd(_swiglu_fn, f"{tag}_dgu", (1, s // t2), [(r["gu"], _rows(t2, 2 * f), [f, f])],
                    [(d_act, _rows(t2, f), None)], [(0, jax.ShapeDtypeStruct((s, 2 * f), BF16), _rows(t2, 2 * f), "tile")])
    d_win = _matmul(r["xn"], d_gu, mode="tn", name=f"{tag}_dwin")
    d_xn = _matmul(d_gu, w_in, mode="nt", name=f"{tag}_dxn")
    t = _tile(s, 512, 8)
    d_h, d_nw = _rw_bwd(_rms_fn, f"{tag}_dnorm", (1, s // t), [(r["h"], _rows(t, d), None), (nw, _par(d), None)],
                        [(d_xn, _rows(t, d), None)],
                        [(0, jax.ShapeDtypeStruct((s, d), F32), _rows(t, d), "tile"),
                         (1, jax.ShapeDtypeStruct((1, d), F32), _par(d), "acc_all")],
                        add=(gh, _rows(t, d), 0))
    return d_h, (d_nw, d_win, d_wout)


def _mix_fwd(h, p, cfg, tag):
    s, d, o = cfg["s"], cfg["d"], cfg["offs"]
    t = _tile(s, 512, 8)
    u, = _rw_fwd(_rms_fn, f"{tag}_norm", (1, s // t), [(h, _rows(t, d), None), (p["mix_norm"], _par(d), None)],
                 [(jax.ShapeDtypeStruct((s, d), BF16), _rows(t, d), None)])
    proj = _matmul(u, p["w_in"], mode="nn", name=f"{tag}_in")
    cw, tr = cfg["cw"], _tile(s, 512, 8)
    xbc_c = _conv_fwd(proj, o["xbc"] // cw, p["ssm_conv_w"], p["ssm_conv_b"], name=f"{tag}_sconv", cw=cw, tr=tr)
    qkv_c = _conv_fwd(proj, o["qkv"] // cw, p["dn_conv_w"], jnp.zeros((1, cfg["cd"]), F32), name=f"{tag}_dconv", cw=cw, tr=tr)
    y, s_states = _ssd_fwd(xbc_c, proj, p["ssm_dtb"], p["ssm_alog"], p["ssm_dsk"], cfg, f"{tag}_ssd")
    hp, inner, g = cfg["hp"], cfg["inner"], SSM_GROUPS
    t4 = _tile(s, 512, 8)
    zs_blk = o["zs"] // hp
    y_s, = _rw_fwd(_ssm_out_fn, f"{tag}_sout", (g, s // t4),
                   [(y, _rows(t4, hp, lambda j: j), None), (proj, _rows(t4, hp, lambda j: zs_blk + j), None),
                    (p["ssm_norm"], _par(hp, True), None)],
                   [(jax.ShapeDtypeStruct((s, inner), BF16), _rows(t4, hp, lambda j: j), None)])
    uu, ww, qd, kd, aq, gl = _gdn_prep_fwd(qkv_c, proj, p["dn_alog"], p["dn_dtb"], cfg, f"{tag}_prep")
    o_dn, d_states = _gdn_scan_fwd(qd, aq, uu, ww, kd, gl, cfg, f"{tag}_scan")
    hd, dv = cfg["hd"], cfg["dv"]
    zd_blk = o["zd"] // dv
    o_spec = pl.BlockSpec((None, t4, dv), lambda j, i: (j, i, 0))
    y_d, = _rw_fwd(_dn_out_fn, f"{tag}_dout", (hd, s // t4),
                   [(o_dn, o_spec, None), (proj, _rows(t4, dv, lambda j: zd_blk + j), None), (p["dn_norm"], _par(dv), None)],
                   [(jax.ShapeDtypeStruct((s, cfg["vd"]), BF16), _rows(t4, dv, lambda j: j), None)])
    ps = _matmul(y_s, p["ssm_w_branch"], mode="nn", name=f"{tag}_sbr")
    pd = _matmul(y_d, p["dn_w_branch"], mode="nn", name=f"{tag}_dbr")
    t6 = _tile(s, 256, 8)
    merged, = _rw_fwd(_merge_fn, f"{tag}_merge", (1, s // t6),
                      [(proj, _rows(t6, 2 * d, o["gates"] // (2 * d)), [d, d]), (ps, _rows(t6, d), None), (pd, _rows(t6, d), None)],
                      [(jax.ShapeDtypeStruct((s, d), BF16), _rows(t6, d), None)])
    h_out = _matmul(merged, p["w_out"], mode="nn", name=f"{tag}_out", res=h)
    res = dict(h=h, u=u, proj=proj, xbc_c=xbc_c, qkv_c=qkv_c, y=y, s_states=s_states, y_s=y_s, uu=uu, ww=ww, qd=qd, kd=kd,
               aq=aq, gl=gl, o_dn=o_dn, d_states=d_states, y_d=y_d, ps=ps, pd=pd, merged=merged)
    return h_out, res


def _mix_bwd(gh, r, p, cfg, tag):
    s, d, o = cfg["s"], cfg["d"], cfg["offs"]
    cw, tr = cfg["cw"], _tile(s, 512, 8)
    hp, inner, g, hd, dv, dk = cfg["hp"], cfg["inner"], SSM_GROUPS, cfg["hd"], cfg["dv"], cfg["dk"]
    proj = r["proj"]
    grads = {}
    grads["w_out"] = _matmul(r["merged"], gh, mode="tn", name=f"{tag}_dwout")
    d_merged = _matmul(gh, p["w_out"], mode="nt", name=f"{tag}_dmerged")
    dproj = jax.ShapeDtypeStruct((s, cfg["pw"]), BF16)
    t6 = _tile(s, 256, 8)
    gates_spec = _rows(t6, 2 * d, o["gates"] // (2 * d))
    dproj, d_ps, d_pd = _rw_bwd(
        _merge_fn, f"{tag}_dmerge", (1, s // t6),
        [(proj, gates_spec, [d, d]), (r["ps"], _rows(t6, d), None), (r["pd"], _rows(t6, d), None)],
        [(d_merged, _rows(t6, d), None)],
        [(0, dproj, gates_spec, "tile"), (1, jax.ShapeDtypeStruct((s, d), BF16), _rows(t6, d), "tile"),
         (2, jax.ShapeDtypeStruct((s, d), BF16), _rows(t6, d), "tile")])
    grads["ssm_w_branch"] = _matmul(r["y_s"], d_ps, mode="tn", name=f"{tag}_dwsbr")
    grads["dn_w_branch"] = _matmul(r["y_d"], d_pd, mode="tn", name=f"{tag}_dwdbr")
    d_ys = _matmul(d_ps, p["ssm_w_branch"], mode="nt", name=f"{tag}_dys")
    d_yd = _matmul(d_pd, p["dn_w_branch"], mode="nt", name=f"{tag}_dyd")
    t4 = _tile(s, 512, 8)
    zs_blk, zd_blk = o["zs"] // hp, o["zd"] // dv
    zs_spec = _rows(t4, hp, lambda j: zs_blk + j)
    d_y, dproj, grads["ssm_norm"] = _rw_bwd(
        _ssm_out_fn, f"{tag}_dsout", (g, s // t4),
        [(r["y"], _rows(t4, hp, lambda j: j), None), (proj, zs_spec, None), (p["ssm_norm"], _par(hp, True), None)],
        [(d_ys, _rows(t4, hp, lambda j: j), None)],
        [(0, jax.ShapeDtypeStruct((s, inner), F32), _rows(t4, hp, lambda j: j), "tile"),
         (1, jax.ShapeDtypeStruct(dproj.shape, BF16), zs_spec, "tile"),
         (2, jax.ShapeDtypeStruct((1, inner), F32), _par(hp, True), "acc_row")],
        alias=(dproj, 1))
    o_spec = pl.BlockSpec((None, t4, dv), lambda j, i: (j, i, 0))
    zd_spec = _rows(t4, dv, lambda j: zd_blk + j)
    d_o, dproj, grads["dn_norm"] = _rw_bwd(
        _dn_out_fn, f"{tag}_ddout", (hd, s // t4),
        [(r["o_dn"], o_spec, None), (proj, zd_spec, None), (p["dn_norm"], _par(dv), None)],
        [(d_yd, _rows(t4, dv, lambda j: j), None)],
        [(0, jax.ShapeDtypeStruct((hd, s, dv), F32), o_spec, "tile"),
         (1, jax.ShapeDtypeStruct(dproj.shape, BF16), zd_spec, "tile"),
         (2, jax.ShapeDtypeStruct((1, dv), F32), _par(dv), "acc_all")],
        alias=(dproj, 1))
    d_xs, d_bm, d_cm, dsm_s, g_dtb, g_alog, g_dsk = _ssd_bwd(
        r["xbc_c"], proj, p["ssm_dtb"], p["ssm_alog"], p["ssm_dsk"], r["s_states"], d_y, cfg, f"{tag}_dssd")
    grads["ssm_dt_bias"], grads["ssm_a_log"], grads["ssm_d"] = (v[0, :cfg["hs"]] for v in (g_dtb, g_alog, g_dsk))
    dws, dbs, col = [], [], 0
    for nm, dy in (("xs", d_xs), ("bm", d_bm), ("cm", d_cm)):
        wd = dy.shape[1]
        dproj, dw_, db_ = _conv_bwd(proj, (o["xbc"] + col) // cw, p["ssm_conv_w"][:, col:col + wd],
                                    p["ssm_conv_b"][:, col:col + wd], dy, dproj, (o["xbc"] + col) // cw,
                                    name=f"{tag}_dsconv_{nm}", cw=cw, tr=tr)
        dws.append(dw_)
        dbs.append(db_)
        col += wd
    grads["ssm_conv_w"] = jnp.concatenate(dws, axis=1)
    grads["ssm_conv_b"] = jnp.concatenate(dbs, axis=1)[0]
    cts = _gdn_scan_bwd(r["qd"], r["aq"], r["uu"], r["ww"], r["kd"], r["gl"], r["d_states"], d_o, cfg, f"{tag}_dscan")
    d_qd, d_aq, d_uu, d_ww, d_kd, d_gl = cts
    d_q, d_k, d_v, dsm_d, g_alog_d, g_dtb_d = _gdn_prep_bwd(
        r["qkv_c"], proj, p["dn_alog"], p["dn_dtb"], (d_uu, d_ww, d_qd, d_kd, d_aq, d_gl), cfg, f"{tag}_dprep")
    a0 = cfg["hs"] + hd
    grads["dn_a_log"], grads["dn_dt_bias"] = g_alog_d[0, a0:a0 + hd], g_dtb_d[0, a0:a0 + hd]
    dws, col = [], 0
    zero_b = jnp.zeros((1, cfg["cd"]), F32)
    for nm, dy in (("q", d_q), ("k", d_k), ("v", d_v)):
        wd = dy.shape[1]
        dproj, dw_, _ = _conv_bwd(proj, (o["qkv"] + col) // cw, p["dn_conv_w"][:, col:col + wd], zero_b[:, col:col + wd], dy,
                                  dproj, (o["qkv"] + col) // cw, name=f"{tag}_ddconv_{nm}", cw=cw, tr=tr)
        dws.append(dw_)
        col += wd
    grads["dn_conv_w"] = jnp.concatenate(dws, axis=1)
    dproj = _dsmall(dsm_s, dsm_d, dproj, cfg["small_blk"], f"{tag}_dsmall")
    grads["w_in"] = _matmul(r["u"], dproj, mode="tn", name=f"{tag}_dwin")
    d_u = _matmul(dproj, p["w_in"], mode="nt", name=f"{tag}_du")
    t = _tile(s, 512, 8)
    d_h, grads["mix_norm"] = _rw_bwd(
        _rms_fn, f"{tag}_dnorm", (1, s // t), [(r["h"], _rows(t, d), None), (p["mix_norm"], _par(d), None)],
        [(d_u, _rows(t, d), None)],
        [(0, jax.ShapeDtypeStruct((s, d), F32), _rows(t, d), "tile"), (1, jax.ShapeDtypeStruct((1, d), F32), _par(d), "acc_all")],
        add=(gh, _rows(t, d), 0))
    return d_h, grads


_BIG = ("ffn1_w_in", "ffn1_w_out", "w_in", "ssm_w_branch", "dn_w_branch", "w_out", "ffn2_w_in", "ffn2_w_out")
_COL_SHARDED = ("ffn1_w_in", "w_in", "ffn2_w_in")
_CONV = ("ssm_conv_w", "dn_conv_w")
_NAMES = ("ffn1_norm", "ffn1_w_in", "ffn1_w_out", "mix_norm", "w_in", "ssm_conv_w", "ssm_conv_b", "ssm_dt_bias", "ssm_a_log",
          "ssm_d", "ssm_norm", "ssm_w_branch", "dn_conv_w", "dn_dt_bias", "dn_a_log", "dn_norm", "dn_w_branch", "w_out",
          "ffn2_norm", "ffn2_w_in", "ffn2_w_out", "final_norm")


def kernel(x, ffn1_norm, ffn1_w_in, ffn1_w_out, mix_norm, w_in, ssm_conv_w, ssm_conv_b, ssm_dt_bias, ssm_a_log, ssm_d, ssm_norm, ssm_w_branch, dn_conv_w, dn_dt_bias, dn_a_log, dn_norm, dn_w_branch, w_out, ffn2_norm, ffn2_w_in, ffn2_w_out, final_norm, loss_target, m_ffn1_norm, m_ffn1_w_in, m_ffn1_w_out, m_mix_norm, m_w_in, m_ssm_conv_w, m_ssm_conv_b, m_ssm_dt_bias, m_ssm_a_log, m_ssm_d, m_ssm_norm, m_ssm_w_branch, m_dn_conv_w, m_dn_dt_bias, m_dn_a_log, m_dn_norm, m_dn_w_branch, m_w_out, m_ffn2_norm, m_ffn2_w_in, m_ffn2_w_out, m_final_norm, v_ffn1_norm, v_ffn1_w_in, v_ffn1_w_out, v_mix_norm, v_w_in, v_ssm_conv_w, v_ssm_conv_b, v_ssm_dt_bias, v_ssm_a_log, v_ssm_d, v_ssm_norm, v_ssm_w_branch, v_dn_conv_w, v_dn_dt_bias, v_dn_a_log, v_dn_norm, v_dn_w_branch, v_w_out, v_ffn2_norm, v_ffn2_w_in, v_ffn2_w_out, v_final_norm):
    args = locals()
    w = {n: args[n] for n in _NAMES}
    mom = {n: args["m_" + n] for n in _NAMES}
    var = {n: args["v_" + n] for n in _NAMES}
    cfg = _config(x, ffn1_w_out, ssm_conv_b, ssm_dt_bias, ssm_norm, dn_conv_w, dn_dt_bias, dn_norm, dn_w_branch)
    depth, s, d = ffn1_norm.shape[0], cfg["s"], cfg["d"]
    me = 4 * lax.axis_index("x") + 2 * lax.axis_index("y") + lax.axis_index("c")

    gathered = _exchange([w[n].astype(BF16) for n in _BIG] + [w[n] for n in _CONV], scatter=False, name="gather_weights")
    full = {}
    for n, g in zip(_BIG + _CONV, gathered):
        full[n] = _cols_gathered(g) if (n in _COL_SHARDED or n in _CONV) else _rows_gathered(g)

    hs, hd = cfg["hs"], cfg["hd"]
    layers = []
    for l in range(depth):
        layers.append(dict(
            ffn1_norm=ffn1_norm[l][None], ffn1_w_in=full["ffn1_w_in"][l], ffn1_w_out=full["ffn1_w_out"][l],
            mix_norm=mix_norm[l][None], w_in=_permute_w_in(full["w_in"][l], cfg),
            ssm_conv_w=full["ssm_conv_w"][l], ssm_conv_b=ssm_conv_b[l][None],
            ssm_dtb=_lane_row(ssm_dt_bias[l], 0), ssm_alog=_lane_row(ssm_a_log[l], 0), ssm_dsk=_lane_row(ssm_d[l], 0),
            ssm_norm=ssm_norm[l][None], ssm_w_branch=full["ssm_w_branch"][l], dn_conv_w=full["dn_conv_w"][l],
            dn_dtb=_lane_row(dn_dt_bias[l], hs + hd), dn_alog=_lane_row(dn_a_log[l], hs + hd), dn_norm=dn_norm[l][None],
            dn_w_branch=full["dn_w_branch"][l], w_out=full["w_out"][l],
            ffn2_norm=ffn2_norm[l][None], ffn2_w_in=full["ffn2_w_in"][l], ffn2_w_out=full["ffn2_w_out"][l]))

    h = x.reshape(s, d)
    saved = []
    for l, p in enumerate(layers):
        h, r1 = _ffn_fwd(h, p["ffn1_norm"], p["ffn1_w_in"], p["ffn1_w_out"], cfg, f"l{l}_ffn1")
        h, rm = _mix_fwd(h, p, cfg, f"l{l}_mix")
        h, r2 = _ffn_fwd(h, p["ffn2_norm"], p["ffn2_w_in"], p["ffn2_w_out"], cfg, f"l{l}_ffn2")
        saved.append((r1, rm, r2))
    loss_blk, gh, g_final = _loss_head(h, loss_target.reshape(s, d), final_norm[None], "loss_head")
    loss = lax.psum(loss_blk[0, 0], ("x", "y", "c"))

    lg = [None] * depth
    for l in reversed(range(depth)):
        p, (r1, rm, r2) = layers[l], saved[l]
        gh, (g_n2, g_win2, g_wout2) = _ffn_bwd(gh, r2, p["ffn2_norm"], p["ffn2_w_in"], p["ffn2_w_out"], cfg, f"l{l}_ffn2")
        gh, gm = _mix_bwd(gh, rm, p, cfg, f"l{l}_mix")
        gh, (g_n1, g_win1, g_wout1) = _ffn_bwd(gh, r1, p["ffn1_norm"], p["ffn1_w_in"], p["ffn1_w_out"], cfg, f"l{l}_ffn1")
        gm["w_in"] = _unpermute_w_in(gm["w_in"], cfg)
        gm.update(ffn1_norm=g_n1[0], ffn1_w_in=g_win1, ffn1_w_out=g_wout1, ffn2_norm=g_n2[0], ffn2_w_in=g_win2,
                  ffn2_w_out=g_wout2, mix_norm=gm["mix_norm"][0], ssm_norm=gm["ssm_norm"][0], dn_norm=gm["dn_norm"][0])
        lg[l] = gm
    grad_x = gh.reshape(x.shape)
    local = {n: jnp.stack([lg[l][n] for l in range(depth)]) for n in _NAMES if n != "final_norm"}
    local["final_norm"] = g_final[0]

    parts = _exchange([(_cols_scatter(local[n]) if n in _COL_SHARDED else _rows_scatter(local[n])).astype(BF16) for n in _BIG],
                      scatter=True, name="scatter_grads")
    out_g, out_d, out_m, out_v = {}, {}, {}, {}
    for n, pt in zip(_BIG, parts):
        out_g[n], out_d[n], out_m[n], out_v[n] = _adamw_sum(pt, w[n], mom[n], var[n], f"adamw_{n}")

    small = [n for n in _NAMES if n not in _BIG]
    packed, = _exchange([_pack([local[n] for n in small])], scatter=False, name="gather_small_grads")
    total = _unpack(_sum_parts(packed, "sum_small_grads"), [local[n].shape for n in small])
    for n, g in zip(small, total):
        if n in _CONV:
            c = w[n].shape[2]
            g = lax.dynamic_slice_in_dim(g, me * c, c, axis=2)
        out_g[n] = g
    shapes = [w[n].shape for n in small]
    upd = _adamw_flat(_pack([out_g[n] for n in small]), _pack([w[n] for n in small]), _pack([mom[n] for n in small]),
                      _pack([var[n] for n in small]), "adamw_small")
    for dst, pk in zip((out_d, out_m, out_v), upd):
        for n, a in zip(small, _unpack(pk, shapes)):
            dst[n] = a

    return (loss, grad_x, *[out_g[n] for n in _NAMES], *[out_d[n] for n in _NAMES], *[out_m[n] for n in _NAMES],
            *[out_v[n] for n in _NAMES])
```

```python
import functools
import math

import jax
import jax.numpy as jnp
from jax import lax
from jax.experimental import pallas as pl
from jax.experimental.pallas import tpu as pltpu

F32, BF16 = jnp.float32, jnp.bfloat16
MESH = pl.DeviceIdType.MESH

N_DEV = 8
N_CHIP = 4
EPS = 1e-6
CONV_K = 4
SSM_GROUPS = 4
SSM_CHUNK = 128
DN_CHUNK = 64
ADAM_LR, ADAM_B1, ADAM_B2, ADAM_EPS, ADAM_WD, ADAM_STEP = 0.001, 0.9, 0.999, 1e-08, 0.01, 10

V7X_VMEM_BYTES = 64 * 1024 * 1024
VMEM_LIMIT = 52 * 1024 * 1024
MATMUL_VMEM_BUDGET = 44 * 1024 * 1024
LANES = 128
SMALL_W = 256
SMALL_R = 128
CONV_HALO = 8
HALO_BLK = 16
CONV_STRIP = 32


def _tile(n, target, quantum):
    if n <= target:
        return n
    t = (target // quantum) * quantum
    while t >= quantum:
        if n % t == 0:
            return t
        t -= quantum
    return n


def _cp(*sem):
    return pltpu.CompilerParams(dimension_semantics=sem, vmem_limit_bytes=VMEM_LIMIT)


def _softplus(x):
    return jnp.maximum(x, 0.0) + jnp.log1p(jnp.exp(-jnp.abs(x)))


def _silu(x):
    return x * jax.nn.sigmoid(x)


def _bdot(a, b, dims):
    return lax.dot_general(a.astype(BF16), b.astype(BF16), dims, preferred_element_type=F32)


_NN = (((1,), (0,)), ((), ()))
_NT = (((1,), (1,)), ((), ()))
_TN = (((0,), (0,)), ((), ()))
_MM_DIMS = {"nn": _NN, "nt": _NT, "tn": _TN}


def _mm3(spec, a, b):
    ah, bh = a.astype(BF16), b.astype(BF16)
    al, bl = (a - ah.astype(F32)).astype(BF16), (b - bh.astype(F32)).astype(BF16)
    e = lambda x, y: jnp.einsum(spec, x, y, preferred_element_type=F32)
    return e(ah, bh) + (e(ah, bl) + e(al, bh))


@jax.custom_vjp
def _pmm(a, b):
    return _mm3("bij,bjk->bik", a, b)


def _pmm_fwd(a, b):
    return _pmm(a, b), (a, b)


def _pmm_bwd(res, g):
    a, b = res
    return _mm3("bik,bjk->bij", g, b), _mm3("bji,bjk->bik", a, g)


_pmm.defvjp(_pmm_fwd, _pmm_bwd)


@jax.custom_vjp
def _neumann_inv(n):
    c = n.shape[-1]
    r = lax.broadcasted_iota(jnp.int32, (c, c), 0)
    cc = lax.broadcasted_iota(jnp.int32, (c, c), 1)
    t, pw = (r == cc).astype(F32)[None] + n, n
    for _ in range(int(math.log2(c)) - 1):
        pw = _mm3("bij,bjk->bik", pw, pw)
        t = t + _mm3("bij,bjk->bik", t, pw)
    return t


def _neumann_fwd(n):
    t = _neumann_inv(n)
    return t, t


def _neumann_bwd(t, g):
    return (_mm3("bik,bjk->bij", _mm3("bji,bjk->bik", t, g), t),)


_neumann_inv.defvjp(_neumann_fwd, _neumann_bwd)


def _matmul(a, b, *, mode, name, out_dtype=F32, res=None, scale=1.0):
    if mode == "nn":
        (m, k), (k2, n) = a.shape, b.shape
    elif mode == "nt":
        (m, k), (n, k2) = a.shape, b.shape
    else:
        (k, m), (k2, n) = a.shape, b.shape
    assert k == k2, (name, a.shape, b.shape)
    has_res = res is not None
    tn = _tile(n, 1408, LANES)
    shapes = ((512, 4096), (1024, 2048), (1024, 1024), (512, 512)) if mode == "tn" else (
        (1024, 2816), (1024, 2048), (1024, 1408), (1024, 1024), (512, 512))
    for rows, depth in shapes:
        tm, tk = _tile(m, rows, LANES), _tile(k, depth, LANES)
        need = 2 * tk * (tm * a.dtype.itemsize + tn * b.dtype.itemsize) + tm * tn * (4 + 2 * jnp.dtype(out_dtype).itemsize)
        need += 2 * tm * tn * res.dtype.itemsize if has_res else 0
        if need <= MATMUL_VMEM_BUDGET:
            break
    nk = k // tk
    dims = _MM_DIMS[mode]
    a_spec = pl.BlockSpec((tk, tm), lambda i, j, q: (q, i)) if mode == "tn" else pl.BlockSpec((tm, tk), lambda i, j, q: (i, q))
    b_spec = pl.BlockSpec((tn, tk), lambda i, j, q: (j, q)) if mode == "nt" else pl.BlockSpec((tk, tn), lambda i, j, q: (q, j))
    o_spec = pl.BlockSpec((tm, tn), lambda i, j, q: (i, j))

    def body(*refs):
        a_ref, b_ref = refs[0], refs[1]
        res_ref = refs[2] if has_res else None
        o_ref = refs[3 if has_res else 2]
        acc_ref = refs[-1] if nk > 1 else None

        def finish(acc):
            val = acc * scale if scale != 1.0 else acc
            if has_res:
                val = res_ref[...].astype(F32) + val
            o_ref[...] = val.astype(o_ref.dtype)

        if nk == 1:
            finish(_bdot(a_ref[...], b_ref[...], dims))
        else:
            q = pl.program_id(2)

            @pl.when(q == 0)
            def _():
                acc_ref[...] = jnp.zeros(acc_ref.shape, F32)

            acc_ref[...] += _bdot(a_ref[...], b_ref[...], dims)

            @pl.when(q == nk - 1)
            def _():
                finish(acc_ref[...])

    ins = [a, b] + ([res] if has_res else [])
    in_specs = [a_spec, b_spec] + ([o_spec] if has_res else [])
    return pl.pallas_call(
        body, name=name, grid=(m // tm, n // tn, nk), in_specs=in_specs, out_specs=o_spec,
        out_shape=jax.ShapeDtypeStruct((m, n), out_dtype),
        scratch_shapes=[pltpu.VMEM((tm, tn), F32)] if nk > 1 else [],
        compiler_params=_cp("parallel", "parallel", "arbitrary"),
    )(*ins)


def _read(ref, split, rows):
    rows = slice(None) if ref.shape[0] == 1 else rows
    if split is None:
        return [ref[rows, :].astype(F32)]
    out, off = [], 0
    for w in split:
        out.append(ref[rows, off:off + w].astype(F32))
        off += w
    return out


def _write(ref, vals, split, rows):
    if split is None:
        ref[rows, :] = vals[0].astype(ref.dtype)
        return
    off = 0
    for w, v in zip(split, vals):
        ref[rows, off:off + w] = v.astype(ref.dtype)
        off += w


def _rw_fwd(fn, name, grid, ins, outs):
    n_in = len(ins)

    def body(*refs):
        rows = slice(None)
        args = []
        for r, (_, _, split) in zip(refs[:n_in], ins):
            args += _read(r, split, rows)
        vals = list(fn(*args))
        for r, (_, _, split) in zip(refs[n_in:], outs):
            n = 1 if split is None else len(split)
            _write(r, vals[:n], split, rows)
            vals = vals[n:]

    return pl.pallas_call(
        body, name=name, grid=grid, in_specs=[s for _, s, _ in ins], out_specs=[s for _, s, _ in outs],
        out_shape=[o for o, _, _ in outs], compiler_params=_cp("parallel", "parallel"),
    )(*[a for a, _, _ in ins])


def _rw_bwd(fn, name, grid, ins, cts, grads, add=None, alias=None):
    n_in, n_ct = len(ins), len(cts)
    n_fixed = n_in + n_ct + (1 if add is not None else 0) + (1 if alias is not None else 0)
    arg_pos, pos = [], 0
    for _, _, split in ins:
        n = 1 if split is None else len(split)
        arg_pos.append((pos, n))
        pos += n

    def body(*refs):
        out_refs = refs[n_fixed:]
        rows = slice(None)
        wrt = []
        for idx, _, _, _ in grads:
            p, n = arg_pos[idx]
            wrt += list(range(p, p + n))
        args = []
        for r, (_, _, split) in zip(refs[:n_in], ins):
            args += _read(r, split, rows)
        ct_vals = []
        for r, (_, _, split) in zip(refs[n_in:n_in + n_ct], cts):
            ct_vals += _read(r, split, rows)

        def f(*w):
            full = list(args)
            for p, v in zip(wrt, w):
                full[p] = v
            return tuple(fn(*full))

        _, vjp = jax.vjp(f, *[args[p] for p in wrt])
        g = list(vjp(tuple(ct_vals)))
        first_row = pl.program_id(1) == 0
        first_all = jnp.logical_and(pl.program_id(0) == 0, first_row)
        for gi, (idx, _, _, mode) in enumerate(grads):
            n = arg_pos[idx][1]
            vals, g = g[:n], g[n:]
            o = out_refs[gi]
            if mode == "tile":
                if add is not None and add[2] == gi:
                    vals = [vals[0] + refs[n_in + n_ct][...].astype(F32)]
                _write(o, vals, ins[idx][2], rows)
            else:
                first = first_row if mode == "acc_row" else first_all

                @pl.when(first)
                def _(o=o, val=vals[0]):
                    o[...] = val

                @pl.when(jnp.logical_not(first))
                def _(o=o, val=vals[0]):
                    o[...] += val

    arrays = [a for a, _, _ in ins] + [a for a, _, _ in cts]
    in_specs = [s for _, s, _ in ins] + [s for _, s, _ in cts]
    if add is not None:
        arrays.append(add[0])
        in_specs.append(add[1])
    aliases = {}
    if alias is not None:
        aliases = {len(arrays): alias[1]}
        arrays.append(alias[0])
        in_specs.append(pl.BlockSpec(memory_space=pl.ANY))
    return pl.pallas_call(
        body, name=name, grid=grid, in_specs=in_specs, out_specs=[s for _, _, s, _ in grads],
        out_shape=[o for _, o, _, _ in grads], input_output_aliases=aliases,
        compiler_params=_cp("arbitrary", "arbitrary"),
    )(*arrays)


def _rows(t, w, col=0):
    if callable(col):
        return pl.BlockSpec((t, w), lambda j, i: (i, col(j)))
    return pl.BlockSpec((t, w), lambda j, i: (i, col))


def _par(w, per_col=False):
    return pl.BlockSpec((1, w), (lambda j, i: (0, j)) if per_col else (lambda j, i: (0, 0)))


def _rms_fn(x, w):
    return (x * lax.rsqrt(jnp.mean(x * x, axis=-1, keepdims=True) + EPS) * w,)


def _swiglu_fn(gate, up):
    return (_silu(gate) * up,)


def _ssm_out_fn(y, z, w):
    yg = y * _silu(z)
    return (yg * lax.rsqrt(jnp.mean(yg * yg, axis=-1, keepdims=True) + EPS) * w,)


def _dn_out_fn(o, z, w):
    return (o * lax.rsqrt(jnp.mean(o * o, axis=-1, keepdims=True) + EPS) * w * _silu(z),)


def _merge_fn(gs, gd, ps, pd):
    return (jax.nn.sigmoid(gs) * ps + jax.nn.sigmoid(gd) * pd,)


def _dsmall(a, b, name):
    s = a.shape[0]
    t = _tile(s, 1024, 16)

    def body(a_r, b_r, o_r):
        o_r[:, :SMALL_R] = (a_r[...] + b_r[...]).astype(o_r.dtype)
        o_r[:, SMALL_R:] = jnp.zeros((t, SMALL_W - SMALL_R), o_r.dtype)

    row = pl.BlockSpec((t, SMALL_R), lambda i: (i, 0))
    return pl.pallas_call(
        body, name=name, grid=(s // t,), in_specs=[row, row], out_specs=pl.BlockSpec((t, SMALL_W), lambda i: (i, 0)),
        out_shape=jax.ShapeDtypeStruct((s, SMALL_W), BF16), compiler_params=_cp("parallel"),
    )(a, b)


def _conv_fwd(x, x_col0, w, b, *, name, cw, tr):
    s, c = x.shape[0], w.shape[1]
    nr, ncol, hb = s // tr, c // cw, tr // HALO_BLK
    rs, lo = CONV_STRIP, CONV_HALO - (CONV_K - 1)

    def body(x_ref, prev_ref, w_ref, b_ref, o_ref, buf):
        i = pl.program_id(1)
        buf[0:CONV_HALO, :] = jnp.where(i > 0, prev_ref[HALO_BLK - CONV_HALO:, :].astype(F32), 0.0)
        buf[CONV_HALO:, :] = x_ref[...].astype(F32)
        taps = [w_ref[q:q + 1, :] for q in range(CONV_K)]
        bias = b_ref[...]

        def strip(k, carry):
            r0 = pl.multiple_of(k * rs, rs)
            ext = buf[pl.ds(r0, rs + CONV_HALO), :]
            acc = bias + taps[0] * ext[lo:lo + rs]
            for q in range(1, CONV_K):
                acc = acc + taps[q] * ext[lo + q:lo + q + rs]
            o_ref[pl.ds(r0, rs), :] = _silu(acc).astype(o_ref.dtype)
            return carry

        lax.fori_loop(0, tr // rs, strip, 0)

    return pl.pallas_call(
        body, name=name, grid=(ncol, nr),
        in_specs=[pl.BlockSpec((tr, cw), lambda j, i: (i, x_col0 + j)),
                  pl.BlockSpec((HALO_BLK, cw), lambda j, i: (jnp.maximum(i * hb - 1, 0), x_col0 + j)),
                  pl.BlockSpec((CONV_K, cw), lambda j, i: (0, j)), pl.BlockSpec((1, cw), lambda j, i: (0, j))],
        out_specs=pl.BlockSpec((tr, cw), lambda j, i: (i, j)),
        out_shape=jax.ShapeDtypeStruct((s, c), BF16),
        scratch_shapes=[pltpu.VMEM((CONV_HALO + tr, cw), F32)],
        compiler_params=_cp("parallel", "parallel"),
    )(x, x, w, b)


def _conv_bwd(x, x_col0, w, b, dy, dproj, out_col0, *, name, cw, tr):
    s, c = dy.shape
    nr, ncol = s // tr, c // cw
    hb, last_hb = tr // HALO_BLK, s // HALO_BLK - 1
    hb8, last_hb8 = tr // CONV_HALO, s // CONV_HALO - 1
    ext = tr + CONV_HALO
    rs, lo = CONV_STRIP, CONV_HALO - (CONV_K - 1)
    fresh = isinstance(dproj, jax.ShapeDtypeStruct)

    def body(x_ref, prev_ref, next_ref, dy_ref, dyn_ref, w_ref, b_ref, *rest):
        dx_ref, dw_ref, db_ref, xbuf, gbuf = rest[-5:]
        i = pl.program_id(1)
        xbuf[0:CONV_HALO, :] = jnp.where(i > 0, prev_ref[HALO_BLK - CONV_HALO:, :].astype(F32), 0.0)
        xbuf[CONV_HALO:CONV_HALO + tr, :] = x_ref[...].astype(F32)
        xbuf[CONV_HALO + tr:, :] = next_ref[0:CONV_HALO, :].astype(F32)
        taps = [w_ref[q:q + 1, :] for q in range(CONV_K)]
        bias = b_ref[...]

        def dpre(xe, dy, n):
            pre = bias + taps[0] * xe[lo:lo + n]
            for q in range(1, CONV_K):
                pre = pre + taps[q] * xe[lo + q:lo + q + n]
            sg = jax.nn.sigmoid(pre)
            return dy * (sg * (1.0 + pre * (1.0 - sg)))

        def strip1(k, carry):
            r0 = pl.multiple_of(k * rs, rs)
            gbuf[pl.ds(r0, rs), :] = dpre(xbuf[pl.ds(r0, rs + CONV_HALO), :], dy_ref[pl.ds(r0, rs), :].astype(F32), rs)
            return carry

        lax.fori_loop(0, tr // rs, strip1, 0)
        gbuf[tr:, :] = dpre(xbuf[tr:, :], jnp.where(i < nr - 1, dyn_ref[...].astype(F32), 0.0), CONV_HALO)

        def fold(v):
            acc = v[0:8]
            for a in range(1, rs // 8):
                acc = acc + v[8 * a:8 * a + 8]
            return acc

        def strip2(k, carry):
            r0 = pl.multiple_of(k * rs, rs)
            ge = gbuf[pl.ds(r0, rs + CONV_HALO), :]
            x_own = xbuf[pl.ds(r0 + CONV_HALO, rs), :]
            dx = jnp.zeros((rs, cw), F32)
            new = []
            for q in range(CONV_K):
                g_q = ge[CONV_K - 1 - q:CONV_K - 1 - q + rs]
                dx = dx + taps[q] * g_q
                new.append(carry[q] + fold(x_own * g_q))
            dx_ref[pl.ds(r0, rs), :] = dx.astype(dx_ref.dtype)
            return tuple(new) + (carry[CONV_K] + fold(ge[0:rs]),)

        sums = lax.fori_loop(0, tr // rs, strip2, tuple(jnp.zeros((8, cw), F32) for _ in range(CONV_K + 1)))
        dws = [jnp.sum(sums[q], axis=0, keepdims=True) for q in range(CONV_K)]
        dbv = jnp.sum(sums[CONV_K], axis=0, keepdims=True)

        @pl.when(i == 0)
        def _():
            for q in range(CONV_K):
                dw_ref[q:q + 1, :] = dws[q]
            db_ref[...] = dbv

        @pl.when(i > 0)
        def _():
            for q in range(CONV_K):
                dw_ref[q:q + 1, :] += dws[q]
            db_ref[...] += dbv

    xmap = lambda j, i: (i, x_col0 + j)
    ins = [x, x, x, dy, dy, w, b]
    in_specs = [pl.BlockSpec((tr, cw), xmap),
                pl.BlockSpec((HALO_BLK, cw), lambda j, i: (jnp.maximum(i * hb - 1, 0), x_col0 + j)),
                pl.BlockSpec((HALO_BLK, cw), lambda j, i: (jnp.minimum((i + 1) * hb, last_hb), x_col0 + j)),
                pl.BlockSpec((tr, cw), lambda j, i: (i, j)),
                pl.BlockSpec((CONV_HALO, cw), lambda j, i: (jnp.minimum((i + 1) * hb8, last_hb8), j)),
                pl.BlockSpec((CONV_K, cw), lambda j, i: (0, j)), pl.BlockSpec((1, cw), lambda j, i: (0, j))]
    aliases = {}
    if not fresh:
        aliases = {len(ins): 0}
        ins.append(dproj)
        in_specs.append(pl.BlockSpec(memory_space=pl.ANY))
    return pl.pallas_call(
        body, name=name, grid=(ncol, nr), in_specs=in_specs,
        out_specs=[pl.BlockSpec((tr, cw), lambda j, i: (i, out_col0 + j)),
                   pl.BlockSpec((CONV_K, cw), lambda j, i: (0, j)), pl.BlockSpec((1, cw), lambda j, i: (0, j))],
        out_shape=[jax.ShapeDtypeStruct(dproj.shape, dproj.dtype), jax.ShapeDtypeStruct((CONV_K, c), F32),
                   jax.ShapeDtypeStruct((1, c), F32)],
        scratch_shapes=[pltpu.VMEM((CONV_HALO + ext, cw), F32), pltpu.VMEM((ext, cw), F32)],
        input_output_aliases=aliases, compiler_params=_cp("arbitrary", "arbitrary"),
    )(*ins)


def _ssd_chunk(xs, bm, cm, small, dtb, alog, dsk, st, g, *, hg, p):
    l, w = small.shape
    per = LANES // p
    lane_w = lax.broadcasted_iota(jnp.int32, (1, w), 1)
    lane = lax.broadcasted_iota(jnp.int32, (1, LANES), 1)
    r = lax.broadcasted_iota(jnp.int32, (l, l), 0)
    c = lax.broadcasted_iota(jnp.int32, (l, l), 1)
    tri = r >= c
    eye = (r == c).astype(F32)
    last = (lax.broadcasted_iota(jnp.int32, (l, 1), 0) == l - 1).astype(F32)
    dt_all = _softplus(small + dtb)
    acum_all = _pmm(tri.astype(F32)[None], (dt_all * (-jnp.exp(alog)))[None])[0]
    cb = _bdot(cm, bm, _NT)
    ys, sts = [], []
    for q, (x_q, st_q) in enumerate(zip(xs, st)):
        dt_e = jnp.zeros((l, LANES), F32)
        acum_e = jnp.zeros((l, LANES), F32)
        d_e = jnp.zeros((1, LANES), F32)
        decays, masks = [], []
        for jj in range(per):
            mh = (lane_w == g * hg + q * per + jj).astype(F32)
            mj = jnp.logical_and(lane >= jj * p, lane < (jj + 1) * p).astype(F32)
            ac = jnp.sum(acum_all * mh, axis=1, keepdims=True)
            dt_e = dt_e + jnp.sum(dt_all * mh, axis=1, keepdims=True) * mj
            acum_e = acum_e + ac * mj
            d_e = d_e + jnp.sum(dsk * mh, axis=1, keepdims=True) * mj
            seg = ac - jnp.sum(ac * eye, axis=0, keepdims=True)
            decays.append(jnp.where(tri, jnp.exp(jnp.where(tri, seg, 0.0)), 0.0))
            masks.append(mj)
        xdt = x_q * dt_e
        y = x_q * d_e
        for dec, mj in zip(decays, masks):
            y = y + _bdot(cb * dec, xdt * mj, _NN)
        a_last = jnp.sum(acum_e * last, axis=0, keepdims=True)
        ys.append(y + jnp.exp(acum_e) * _bdot(cm, st_q, _NN))
        sts.append(st_q * jnp.exp(a_last) + _bdot(bm, xdt * jnp.exp(a_last - acum_e), _TN))
    return tuple(ys) + tuple(sts)


def _pieces(ref, lead=()):
    return [ref[lead + (slice(None), slice(q * LANES, (q + 1) * LANES))].astype(F32) for q in range(ref.shape[-1] // LANES)]


def _ssd_specs(cfg, rev):
    l, hp, n, g = SSM_CHUNK, cfg["hp"], cfg["n"], SSM_GROUPS
    nc = cfg["s"] // l
    cc = (lambda c: nc - 1 - c) if rev else (lambda c: c)
    nb = cfg["inner"] // n
    par = pl.BlockSpec((1, SMALL_R), lambda c, q: (0, 0))
    specs = [pl.BlockSpec((l, hp), lambda c, q: (cc(c), q)),
             pl.BlockSpec((l, n), lambda c, q: (cc(c), nb + q)),
             pl.BlockSpec((l, n), lambda c, q: (cc(c), nb + g + q)),
             pl.BlockSpec((l, SMALL_R), lambda c, q: (cc(c), 0)), par, par, par]
    st_spec = pl.BlockSpec((None, None, n, hp), lambda c, q: (cc(c), q, 0, 0))
    y_spec = pl.BlockSpec((l, hp), lambda c, q: (cc(c), q))
    return specs, st_spec, y_spec, nc


def _ssd_fwd(xbc_c, proj, dtb, alog, dsk, cfg, name):
    specs, st_spec, y_spec, nc = _ssd_specs(cfg, False)
    fn = functools.partial(_ssd_chunk, hg=cfg["hg"], p=cfg["p"])
    npc = cfg["hp"] // LANES

    def body(xs, bm, cm, sm, dtb_r, alog_r, dsk_r, y_ref, sts_ref, st):
        c, g = pl.program_id(0), pl.program_id(1)

        @pl.when(c == 0)
        def _():
            st[g] = jnp.zeros(st.shape[1:], F32)

        sts_ref[...] = st[g]
        out = fn(_pieces(xs), bm[...].astype(F32), cm[...].astype(F32), sm[...], dtb_r[...], alog_r[...], dsk_r[...],
                 _pieces(st, (g,)), g)
        for q in range(npc):
            y_ref[:, q * LANES:(q + 1) * LANES] = out[q]
            st[g, :, q * LANES:(q + 1) * LANES] = out[npc + q]

    return pl.pallas_call(
        body, name=name, grid=(nc, SSM_GROUPS), in_specs=specs, out_specs=[y_spec, st_spec],
        out_shape=[jax.ShapeDtypeStruct((cfg["s"], cfg["inner"]), F32),
                   jax.ShapeDtypeStruct((nc, SSM_GROUPS, cfg["n"], cfg["hp"]), F32)],
        scratch_shapes=[pltpu.VMEM((SSM_GROUPS, cfg["n"], cfg["hp"]), F32)],
        compiler_params=_cp("arbitrary", "arbitrary"),
    )(xbc_c, xbc_c, xbc_c, proj, dtb, alog, dsk)


def _ssd_bwd(xbc_c, proj, dtb, alog, dsk, states, dy, cfg, name):
    specs, st_spec, y_spec, nc = _ssd_specs(cfg, True)
    l, n, gn = SSM_CHUNK, cfg["n"], SSM_GROUPS * cfg["n"]
    fn = functools.partial(_ssd_chunk, hg=cfg["hg"], p=cfg["p"])
    npc = cfg["hp"] // LANES
    rc = lambda c: nc - 1 - c

    def body(xs, bm, cm, sm, dtb_r, alog_r, dsk_r, sts_ref, dy_ref, dxs, dbm, dcm, dsm, ddtb, dalog, ddsk, dst):
        c, g = pl.program_id(0), pl.program_id(1)

        @pl.when(c == 0)
        def _():
            dst[g] = jnp.zeros(dst.shape[1:], F32)

        def f(*a):
            return fn(a[:npc], *a[npc:npc + 6], a[npc + 6:], g)

        _, vjp = jax.vjp(f, *_pieces(xs), bm[...].astype(F32), cm[...].astype(F32), sm[...], dtb_r[...], alog_r[...],
                         dsk_r[...], *_pieces(sts_ref))
        grads = vjp(tuple(_pieces(dy_ref)) + tuple(_pieces(dst, (g,))))
        gb, gc, gs, g1, g2, g3 = grads[npc:npc + 6]
        for q in range(npc):
            dxs[:, q * LANES:(q + 1) * LANES] = grads[q]
            dst[g, :, q * LANES:(q + 1) * LANES] = grads[npc + 6 + q]
        dbm[...] = gb
        dcm[...] = gc

        @pl.when(g == 0)
        def _():
            dsm[...] = gs

        @pl.when(g > 0)
        def _():
            dsm[...] += gs

        first = jnp.logical_and(c == 0, g == 0)

        @pl.when(first)
        def _():
            ddtb[...] = g1
            dalog[...] = g2
            ddsk[...] = g3

        @pl.when(jnp.logical_not(first))
        def _():
            ddtb[...] += g1
            dalog[...] += g2
            ddsk[...] += g3

    par = pl.BlockSpec((1, SMALL_R), lambda c, q: (0, 0))
    return pl.pallas_call(
        body, name=name, grid=(nc, SSM_GROUPS), in_specs=specs + [st_spec, y_spec],
        out_specs=[y_spec, pl.BlockSpec((l, n), lambda c, q: (rc(c), q)), pl.BlockSpec((l, n), lambda c, q: (rc(c), q)),
                   pl.BlockSpec((l, SMALL_R), lambda c, q: (rc(c), 0)), par, par, par],
        out_shape=[jax.ShapeDtypeStruct((cfg["s"], cfg["inner"]), F32), jax.ShapeDtypeStruct((cfg["s"], gn), F32),
                   jax.ShapeDtypeStruct((cfg["s"], gn), F32), jax.ShapeDtypeStruct((cfg["s"], SMALL_R), F32),
                   jax.ShapeDtypeStruct((1, SMALL_R), F32), jax.ShapeDtypeStruct((1, SMALL_R), F32),
                   jax.ShapeDtypeStruct((1, SMALL_R), F32)],
        scratch_shapes=[pltpu.VMEM((SSM_GROUPS, cfg["n"], cfg["hp"]), F32)],
        compiler_params=_cp("arbitrary", "arbitrary"),
    )(xbc_c, xbc_c, xbc_c, proj, dtb, alog, dsk, states, dy)


def _gdn_prep(q, k, v, small, alog, dtb, h, *, boff, aoff):
    c = DN_CHUNK
    rr, dk = q.shape
    nb, w_ = rr // c, small.shape[1]
    lane = lax.broadcasted_iota(jnp.int32, (1, w_), 1)
    g_all = -jnp.exp(alog) * _softplus(small + dtb)
    r = lax.broadcasted_iota(jnp.int32, (c, c), 0)
    cc = lax.broadcasted_iota(jnp.int32, (c, c), 1)
    incl, strict = (r >= cc)[None], (r > cc)[None]
    eye = (r == cc).astype(F32)[None]
    tri_b = jnp.broadcast_to((r >= cc).astype(F32)[None], (nb, c, c))
    gcum_all = _pmm(tri_b, g_all.reshape(nb, c, w_))
    gc = jnp.sum(gcum_all * (lane == aoff + h).astype(F32)[None], axis=2, keepdims=True)
    beta = jnp.sum(jax.nn.sigmoid(small) * (lane == boff + h).astype(F32), axis=1, keepdims=True).reshape(nb, c, 1)
    qn = (q * lax.rsqrt(jnp.sum(q * q, axis=1, keepdims=True) + EPS) * (dk ** -0.5)).reshape(nb, c, dk)
    kn = (k * lax.rsqrt(jnp.sum(k * k, axis=1, keepdims=True) + EPS)).reshape(nb, c, dk)
    v3 = v.reshape(nb, c, v.shape[1])
    g_row = jnp.sum(gc * eye, axis=1, keepdims=True)
    decay = jnp.where(incl, jnp.exp(jnp.where(incl, gc - g_row, 0.0)), 0.0)
    kb = kn * beta
    kk = jnp.einsum("bik,bjk->bij", kb.astype(BF16), kn.astype(BF16), preferred_element_type=F32)
    neg_m = jnp.where(strict, -kk * decay, 0.0)
    t_inv = _neumann_inv(neg_m)
    eg = jnp.exp(gc)
    u = _pmm(t_inv, v3 * beta)
    w = _pmm(t_inv, kb * eg)
    a_qk = jnp.einsum("bik,bjk->bij", qn.astype(BF16), kn.astype(BF16), preferred_element_type=F32) * decay
    last = (lax.broadcasted_iota(jnp.int32, (1, c, 1), 1) == c - 1).astype(F32)
    g_last = jnp.sum(gc * last, axis=1, keepdims=True)
    q_dec = qn * eg
    k_dec = kn * jnp.exp(g_last - gc)
    gl = jnp.broadcast_to(g_last, (nb, 1, LANES))
    return (u.reshape(rr, -1), w.reshape(rr, dk), q_dec.reshape(rr, dk), k_dec.reshape(rr, dk), a_qk, gl)


def _gdn_prep_specs(cfg):
    rr, hd, dk, dv = cfg["prep_rows"], cfg["hd"], cfg["dk"], cfg["dv"]
    nb = rr // DN_CHUNK
    ins = [pl.BlockSpec((rr, dk), lambda i, h: (i, h)), pl.BlockSpec((rr, dk), lambda i, h: (i, hd + h)),
           pl.BlockSpec((rr, dv), lambda i, h: (i, (2 * hd * dk) // dv + h)),
           pl.BlockSpec((rr, SMALL_R), lambda i, h: (i, 0)),
           pl.BlockSpec((1, SMALL_R), lambda i, h: (0, 0)), pl.BlockSpec((1, SMALL_R), lambda i, h: (0, 0))]
    hs = lambda d: pl.BlockSpec((None, rr, d), lambda i, h: (h, i, 0))
    outs = [hs(dv), hs(dk), hs(dk), hs(dk), pl.BlockSpec((None, nb, DN_CHUNK, DN_CHUNK), lambda i, h: (h, i, 0, 0)),
            pl.BlockSpec((None, nb, 1, LANES), lambda i, h: (h, i, 0, 0))]
    s, nch = cfg["s"], cfg["s"] // DN_CHUNK
    shapes = [jax.ShapeDtypeStruct((hd, s, dv), F32)] + [jax.ShapeDtypeStruct((hd, s, dk), F32)] * 3 + [
        jax.ShapeDtypeStruct((hd, nch, DN_CHUNK, DN_CHUNK), F32), jax.ShapeDtypeStruct((hd, nch, 1, LANES), F32)]
    return ins, outs, shapes


def _gdn_prep_fwd(qkv_c, proj, alog, dtb, cfg, name):
    ins, outs, shapes = _gdn_prep_specs(cfg)
    fn = functools.partial(_gdn_prep, boff=cfg["hs"], aoff=cfg["hs"] + cfg["hd"])

    def body(q, k, v, sm, al, db, *o):
        vals = fn(q[...].astype(F32), k[...].astype(F32), v[...].astype(F32), sm[...], al[...], db[...], pl.program_id(1))
        for ref, val in zip(o, vals):
            ref[...] = val

    return pl.pallas_call(
        body, name=name, grid=(cfg["s"] // cfg["prep_rows"], cfg["hd"]), in_specs=ins, out_specs=outs, out_shape=shapes,
        compiler_params=_cp("parallel", "parallel"),
    )(qkv_c, qkv_c, qkv_c, proj, alog, dtb)


def _gdn_prep_bwd(qkv_c, proj, alog, dtb, cts, cfg, name):
    ins, outs, _ = _gdn_prep_specs(cfg)
    rr, hd, dk, dv, s = cfg["prep_rows"], cfg["hd"], cfg["dk"], cfg["dv"], cfg["s"]
    fn = functools.partial(_gdn_prep, boff=cfg["hs"], aoff=cfg["hs"] + cfg["hd"])

    def body(q, k, v, sm, al, db, c0, c1, c2, c3, c4, c5, dq, dkk, dvv, dsm, dal, ddb):
        i, h = pl.program_id(0), pl.program_id(1)
        f = lambda *a: fn(*a, h)
        _, vjp = jax.vjp(f, q[...].astype(F32), k[...].astype(F32), v[...].astype(F32), sm[...], al[...], db[...])
        gq, gk, gv, gs, ga, gd = vjp((c0[...], c1[...], c2[...], c3[...], c4[...], c5[...]))
        dq[...] = gq
        dkk[...] = gk
        dvv[...] = gv

        @pl.when(h == 0)
        def _():
            dsm[...] = gs

        @pl.when(h > 0)
        def _():
            dsm[...] += gs

        first = jnp.logical_and(i == 0, h == 0)

        @pl.when(first)
        def _():
            dal[...] = ga
            ddb[...] = gd

        @pl.when(jnp.logical_not(first))
        def _():
            dal[...] += ga
            ddb[...] += gd

    par = pl.BlockSpec((1, SMALL_R), lambda i, h: (0, 0))
    return pl.pallas_call(
        body, name=name, grid=(s // rr, hd), in_specs=ins + outs,
        out_specs=[pl.BlockSpec((rr, dk), lambda i, h: (i, h)), pl.BlockSpec((rr, dk), lambda i, h: (i, h)),
                   pl.BlockSpec((rr, dv), lambda i, h: (i, h)), pl.BlockSpec((rr, SMALL_R), lambda i, h: (i, 0)), par, par],
        out_shape=[jax.ShapeDtypeStruct((s, hd * dk), F32), jax.ShapeDtypeStruct((s, hd * dk), F32),
                   jax.ShapeDtypeStruct((s, hd * dv), F32), jax.ShapeDtypeStruct((s, SMALL_R), F32),
                   jax.ShapeDtypeStruct((1, SMALL_R), F32), jax.ShapeDtypeStruct((1, SMALL_R), F32)],
        compiler_params=_cp("arbitrary", "arbitrary"),
    )(qkv_c, qkv_c, qkv_c, proj, alog, dtb, *cts)


def _gdn_scan_chunk(qd, aq, u, w, kd, gl, st):
    bm = lambda spec, a, b: jnp.einsum(spec, a.astype(BF16), b.astype(BF16), preferred_element_type=F32)
    v_new = u - bm("hck,hkv->hcv", w, st)
    o = bm("hck,hkv->hcv", qd, st) + bm("hij,hjv->hiv", aq, v_new)
    st_new = st * jnp.exp(gl) + bm("hck,hcv->hkv", kd, v_new)
    return o, st_new


def _gdn_scan_specs(cfg, rev):
    rr, hd, dk, dv = cfg["scan_rows"], cfg["hd"], cfg["dk"], cfg["dv"]
    nb, nblk = rr // DN_CHUNK, cfg["s"] // rr
    ii = (lambda i: nblk - 1 - i) if rev else (lambda i: i)
    hs = lambda d: pl.BlockSpec((hd, rr, d), lambda i: (0, ii(i), 0))
    ins = [hs(dk), pl.BlockSpec((hd, nb, DN_CHUNK, DN_CHUNK), lambda i: (0, ii(i), 0, 0)), hs(dv), hs(dk), hs(dk),
           pl.BlockSpec((hd, nb, 1, LANES), lambda i: (0, ii(i), 0, 0))]
    st_spec = pl.BlockSpec((nb, hd, dk, dv), lambda i: (ii(i), 0, 0, 0))
    return ins, hs(dv), st_spec, nb, nblk


def _gdn_scan_fwd(qd, aq, u, w, kd, gl, cfg, name):
    ins, o_spec, st_spec, nb, nblk = _gdn_scan_specs(cfg, False)
    hd, dk, dv, c = cfg["hd"], cfg["dk"], cfg["dv"], DN_CHUNK

    def body(qd_r, aq_r, u_r, w_r, kd_r, gl_r, o_r, sts_r, st):
        @pl.when(pl.program_id(0) == 0)
        def _():
            st[...] = jnp.zeros(st.shape, F32)

        s = st[...]
        for j in range(nb):
            rows = slice(j * c, (j + 1) * c)
            sts_r[j] = s
            o, s = _gdn_scan_chunk(qd_r[:, rows, :], aq_r[:, j], u_r[:, rows, :], w_r[:, rows, :], kd_r[:, rows, :],
                                   gl_r[:, j], s)
            o_r[:, rows, :] = o
        st[...] = s

    return pl.pallas_call(
        body, name=name, grid=(nblk,), in_specs=ins, out_specs=[o_spec, st_spec],
        out_shape=[jax.ShapeDtypeStruct((hd, cfg["s"], dv), F32),
                   jax.ShapeDtypeStruct((cfg["s"] // c, hd, dk, dv), F32)],
        scratch_shapes=[pltpu.VMEM((hd, dk, dv), F32)], compiler_params=_cp("arbitrary"),
    )(qd, aq, u, w, kd, gl)


def _gdn_scan_bwd(qd, aq, u, w, kd, gl, states, do, cfg, name):
    ins, o_spec, st_spec, nb, nblk = _gdn_scan_specs(cfg, True)
    hd, dk, dv, c = cfg["hd"], cfg["dk"], cfg["dv"], DN_CHUNK

    def body(qd_r, aq_r, u_r, w_r, kd_r, gl_r, sts_r, do_r, dqd, daq, du, dw, dkd, dgl, dst):
        @pl.when(pl.program_id(0) == 0)
        def _():
            dst[...] = jnp.zeros(dst.shape, F32)

        ds = dst[...]
        for j in reversed(range(nb)):
            rows = slice(j * c, (j + 1) * c)
            _, vjp = jax.vjp(_gdn_scan_chunk, qd_r[:, rows, :], aq_r[:, j], u_r[:, rows, :], w_r[:, rows, :],
                             kd_r[:, rows, :], gl_r[:, j], sts_r[j])
            g0, g1, g2, g3, g4, g5, ds = vjp((do_r[:, rows, :], ds))
            dqd[:, rows, :] = g0
            daq[:, j] = g1
            du[:, rows, :] = g2
            dw[:, rows, :] = g3
            dkd[:, rows, :] = g4
            dgl[:, j] = g5
        dst[...] = ds

    s, nch = cfg["s"], cfg["s"] // c
    return pl.pallas_call(
        body, name=name, grid=(nblk,), in_specs=ins + [st_spec, o_spec], out_specs=ins,
        out_shape=[jax.ShapeDtypeStruct((hd, s, dk), F32), jax.ShapeDtypeStruct((hd, nch, c, c), F32),
                   jax.ShapeDtypeStruct((hd, s, dv), F32), jax.ShapeDtypeStruct((hd, s, dk), F32),
                   jax.ShapeDtypeStruct((hd, s, dk), F32), jax.ShapeDtypeStruct((hd, nch, 1, LANES), F32)],
        scratch_shapes=[pltpu.VMEM((hd, dk, dv), F32)], compiler_params=_cp("arbitrary"),
    )(qd, aq, u, w, kd, gl, states, do)


def _loss_head(h, target, w, name):
    s, d = h.shape
    t = _tile(s, 512, 8)

    def body(h_r, t_r, w_r, loss_r, dh_r, dw_r):
        i = pl.program_id(0)
        (y,), vjp = jax.vjp(_rms_fn, h_r[...], w_r[...])
        err = y - t_r[...]
        part = 0.5 * jnp.sum(jnp.mean(err * err, axis=-1, keepdims=True))
        gh, gw = vjp((err * (1.0 / d),))
        dh_r[...] = gh

        @pl.when(i == 0)
        def _():
            loss_r[...] = jnp.zeros(loss_r.shape, F32) + part
            dw_r[...] = gw

        @pl.when(i > 0)
        def _():
            loss_r[...] += part
            dw_r[...] += gw

    row = pl.BlockSpec((t, d), lambda i: (i, 0))
    return pl.pallas_call(
        body, name=name, grid=(s // t,), in_specs=[row, row, pl.BlockSpec((1, d), lambda i: (0, 0))],
        out_specs=[pl.BlockSpec((8, LANES), lambda i: (0, 0)), row, pl.BlockSpec((1, d), lambda i: (0, 0))],
        out_shape=[jax.ShapeDtypeStruct((8, LANES), F32), jax.ShapeDtypeStruct((s, d), F32), jax.ShapeDtypeStruct((1, d), F32)],
        compiler_params=_cp("arbitrary"),
    )(h, target, w)


def _adamw_math(g, w, m, v):
    m = ADAM_B1 * m + (1.0 - ADAM_B1) * g
    v = ADAM_B2 * v + (1.0 - ADAM_B2) * jnp.square(g)
    m_hat = m / (1.0 - ADAM_B1 ** ADAM_STEP)
    v_hat = v / (1.0 - ADAM_B2 ** ADAM_STEP)
    delta = -ADAM_LR * (m_hat / (jnp.sqrt(v_hat) + ADAM_EPS) + ADAM_WD * w)
    return delta, m, v


def _pair_sum(mine, theirs, name):
    _, nch, nl, r, c = mine.shape
    tr = _tile(r, 128, 16)

    def body(a_r, b_r, o_r):
        o_r[...] = (a_r[...].astype(F32) + b_r[...].astype(F32)).astype(o_r.dtype)

    return pl.pallas_call(
        body, name=name, grid=(nch, nl, r // tr),
        in_specs=[pl.BlockSpec((None, None, None, tr, c), lambda p, a, i: (0, p, a, i, 0)),
                  pl.BlockSpec((None, None, tr, c), lambda p, a, i: (p, a, i, 0))],
        out_specs=pl.BlockSpec((None, None, tr, c), lambda p, a, i: (p, a, i, 0)),
        out_shape=jax.ShapeDtypeStruct(theirs.shape, theirs.dtype), compiler_params=_cp("parallel", "parallel", "parallel"),
    )(mine, theirs)


def _adamw_sum(parts, w, m, v, name):
    nl, r, c = w.shape
    n_parts = parts.shape[0]
    tr = _tile(r, 64, 8)

    def body(p_r, w_r, m_r, v_r, g_o, d_o, m_o, v_o):
        g = p_r[0].astype(F32)
        for q in range(1, n_parts):
            g = g + p_r[q].astype(F32)
        delta, mn, vn = _adamw_math(g, w_r[...], m_r[...], v_r[...])
        g_o[...] = g
        d_o[...] = delta
        m_o[...] = mn
        v_o[...] = vn

    blk = pl.BlockSpec((None, tr, c), lambda a, i: (a, i, 0))
    return pl.pallas_call(
        body, name=name, grid=(nl, r // tr),
        in_specs=[pl.BlockSpec((n_parts, None, tr, c), lambda a, i: (0, a, i, 0)), blk, blk, blk],
        out_specs=[blk] * 4, out_shape=[jax.ShapeDtypeStruct(w.shape, F32)] * 4, compiler_params=_cp("parallel", "parallel"),
    )(parts, w, m, v)


def _sum_parts(parts, name):
    _, r, c = parts.shape
    tr = _tile(r, 512, 8)

    def body(p_r, o_r):
        g = p_r[0]
        for q in range(1, N_DEV):
            g = g + p_r[q]
        o_r[...] = g

    return pl.pallas_call(
        body, name=name, grid=(r // tr,), in_specs=[pl.BlockSpec((N_DEV, tr, c), lambda i: (0, i, 0))],
        out_specs=pl.BlockSpec((tr, c), lambda i: (i, 0)), out_shape=jax.ShapeDtypeStruct((r, c), F32),
        compiler_params=_cp("parallel"),
    )(parts)


def _adamw_flat(g, w, m, v, name):
    r, c = w.shape
    tr = _tile(r, 512, 8)

    def body(g_r, w_r, m_r, v_r, d_o, m_o, v_o):
        delta, mn, vn = _adamw_math(g_r[...], w_r[...], m_r[...], v_r[...])
        d_o[...] = delta
        m_o[...] = mn
        v_o[...] = vn

    blk = pl.BlockSpec((tr, c), lambda i: (i, 0))
    return pl.pallas_call(
        body, name=name, grid=(r // tr,), in_specs=[blk] * 4, out_specs=[blk] * 3,
        out_shape=[jax.ShapeDtypeStruct(w.shape, F32)] * 3, compiler_params=_cp("parallel"),
    )(g, w, m, v)


def _remote(src, dst, send_sems, recv_sems, idx, to):
    return pltpu.make_async_remote_copy(src_ref=src, dst_ref=dst, send_sem=send_sems.at[idx], recv_sem=recv_sems.at[idx],
                                        device_id=to, device_id_type=MESH)


def _comm_call(body, name, arrays, out_shapes, n_sems):
    nt = len(arrays)
    return pl.pallas_call(
        body, name=name, in_specs=[pl.BlockSpec(memory_space=pl.ANY)] * nt, out_specs=[pl.BlockSpec(memory_space=pl.ANY)] * nt,
        out_shape=out_shapes,
        scratch_shapes=[pltpu.SemaphoreType.DMA((nt, n_sems)), pltpu.SemaphoreType.DMA((nt, n_sems)),
                        pltpu.SemaphoreType.DMA((nt,))],
    )(*arrays)


def _gather_all(arrays, name):
    nt = len(arrays)

    def body(*refs):
        srcs, dsts = refs[:nt], refs[nt:2 * nt]
        send_sems, recv_sems, local_sems = refs[2 * nt:]
        x, y, c = lax.axis_index("x"), lax.axis_index("y"), lax.axis_index("c")
        slot = lambda px, py, pc: 4 * px + 2 * py + pc
        me, sib = slot(x, y, c), (x, y, 1 - c)
        chips = [(1 - x, y), (x, 1 - y), (1 - x, 1 - y)]
        local = [pltpu.make_async_copy(srcs[t], dsts[t].at[me], local_sems.at[t]) for t in range(nt)]
        for cp in local:
            cp.start()
        sends = []
        for t in range(nt):
            sends.append(_remote(srcs[t], dsts[t].at[me], send_sems, recv_sems, (t, 0), sib))
            for j, (px, py) in enumerate(chips):
                sends.append(_remote(srcs[t], dsts[t].at[me], send_sems, recv_sems, (t, 1 + j), (px, py, c)))
        for cp in sends:
            cp.start()
        for j, (px, py) in enumerate(chips):
            landed = slot(px, py, c)
            for t in range(nt):
                _remote(srcs[t], dsts[t].at[landed], send_sems, recv_sems, (t, 1 + j), (px, py, c)).wait_recv()
                fwd = _remote(dsts[t].at[landed], dsts[t].at[landed], send_sems, recv_sems, (t, 4 + j), sib)
                fwd.start()
                sends.append(fwd)
        for t in range(nt):
            _remote(srcs[t], dsts[t].at[slot(x, y, 1 - c)], send_sems, recv_sems, (t, 0), sib).wait_recv()
            for j, (px, py) in enumerate(chips):
                _remote(srcs[t], dsts[t].at[slot(px, py, 1 - c)], send_sems, recv_sems, (t, 4 + j), sib).wait_recv()
        for cp in sends:
            cp.wait_send()
        for cp in local:
            cp.wait()

    return _comm_call(body, name, arrays, [jax.ShapeDtypeStruct((N_DEV,) + a.shape, a.dtype) for a in arrays], N_DEV - 1)


def _sibling_swap(arrays, name):
    nt = len(arrays)

    def body(*refs):
        srcs, dsts = refs[:nt], refs[nt:2 * nt]
        send_sems, recv_sems, _ = refs[2 * nt:]
        sib = (lax.axis_index("x"), lax.axis_index("y"), 1 - lax.axis_index("c"))
        copies = [_remote(srcs[t].at[1], dsts[t], send_sems, recv_sems, (t, 0), sib) for t in range(nt)]
        for cp in copies:
            cp.start()
        for cp in copies:
            cp.wait()

    return _comm_call(body, name, arrays, [jax.ShapeDtypeStruct(a.shape[1:], a.dtype) for a in arrays], 1)


def _chip_scatter(arrays, name):
    nt = len(arrays)

    def body(*refs):
        srcs, dsts = refs[:nt], refs[nt:2 * nt]
        send_sems, recv_sems, local_sems = refs[2 * nt:]
        x, y, c = lax.axis_index("x"), lax.axis_index("y"), lax.axis_index("c")
        mine = 2 * x + y
        local = [pltpu.make_async_copy(srcs[t].at[mine], dsts[t].at[mine], local_sems.at[t]) for t in range(nt)]
        for cp in local:
            cp.start()
        sends, arrivals = [], []
        for j, (px, py) in enumerate([(1 - x, y), (x, 1 - y), (1 - x, 1 - y)]):
            theirs = 2 * px + py
            for t in range(nt):
                sends.append(_remote(srcs[t].at[theirs], dsts[t].at[mine], send_sems, recv_sems, (t, j), (px, py, c)))
                arrivals.append(_remote(srcs[t].at[theirs], dsts[t].at[theirs], send_sems, recv_sems, (t, j), (px, py, c)))
        for cp in sends:
            cp.start()
        for cp in arrivals:
            cp.wait_recv()
        for cp in sends:
            cp.wait_send()
        for cp in local:
            cp.wait()

    return _comm_call(body, name, arrays, [jax.ShapeDtypeStruct(a.shape, a.dtype) for a in arrays], N_CHIP - 1)


def _config(x, ffn1_w_out, ssm_conv_b, ssm_dt_bias, ssm_norm, dn_conv_w, dn_dt_bias, dn_norm, dn_w_branch):
    s, d = x.shape[-2], x.shape[-1]
    f = ffn1_w_out.shape[1] * N_DEV
    cs, hs, inner = ssm_conv_b.shape[1], ssm_dt_bias.shape[1], ssm_norm.shape[1]
    gn = (cs - inner) // 2
    hd, dv = dn_dt_bias.shape[1], dn_norm.shape[1]
    cd = dn_conv_w.shape[2] * N_DEV
    vd = hd * dv
    kd = (cd - vd) // 2
    cfg = dict(s=s, d=d, f=f, cs=cs, hs=hs, inner=inner, gn=gn, n=gn // SSM_GROUPS, hg=hs // SSM_GROUPS, p=inner // hs,
               hp=inner // SSM_GROUPS, hd=hd, dv=dv, dk=kd // hd, cd=cd, vd=vd, kd=kd)
    offs, o = {}, 0
    for nm, wd in (("xbc", cs), ("qkv", cd), ("gates", 2 * d), ("zs", inner), ("zd", vd), ("small", SMALL_W)):
        offs[nm] = o
        o += wd
    cfg["offs"], cfg["pw"] = offs, o
    cfg["cw"] = 512
    cfg["pw_main"] = offs["small"]
    cfg["prep_rows"] = min(s, 8 * DN_CHUNK)
    cfg["scan_rows"] = min(s, 4 * DN_CHUNK)
    cfg["in_split"] = (inner, cs, hs, cd, vd, hd, hd, d, d)
    assert hs + 2 * hd <= SMALL_R and cfg["dk"] == dv and dv == LANES and LANES % cfg["p"] == 0 and cfg["hp"] % LANES == 0
    assert offs["qkv"] % cfg["cw"] == 0 and offs["gates"] % (2 * d) == 0 and offs["zs"] % cfg["hp"] == 0
    assert offs["zd"] % dv == 0 and all(wd % cfg["cw"] == 0 for wd in (inner, gn, kd, vd)) and inner % cfg["n"] == 0
    return cfg


def _permute_w_in(w, cfg):
    pts = [0]
    for wd in cfg["in_split"]:
        pts.append(pts[-1] + wd)
    z_s, xbc, dt, qkv, z_d, b_d, a_d, g_s, g_d = [w[:, pts[i]:pts[i + 1]] for i in range(9)]
    pad = jnp.zeros((w.shape[0], SMALL_W - dt.shape[1] - b_d.shape[1] - a_d.shape[1]), w.dtype)
    return jnp.concatenate([xbc, qkv, g_s, g_d, z_s, z_d, dt, b_d, a_d, pad], axis=1)


def _unpermute_w_in(g, cfg):
    o, d = cfg["offs"], cfg["d"]
    hs, hd = cfg["hs"], cfg["hd"]
    sm = g[:, o["small"]:]
    return jnp.concatenate([
        g[:, o["zs"]:o["zs"] + cfg["inner"]], g[:, o["xbc"]:o["xbc"] + cfg["cs"]], sm[:, :hs],
        g[:, o["qkv"]:o["qkv"] + cfg["cd"]], g[:, o["zd"]:o["zd"] + cfg["vd"]], sm[:, hs:hs + hd], sm[:, hs + hd:hs + 2 * hd],
        g[:, o["gates"]:o["gates"] + d], g[:, o["gates"] + d:o["gates"] + 2 * d]], axis=1)


def _lane_row(v, off):
    return jnp.pad(v.astype(F32), (off, SMALL_R - off - v.shape[0]))[None]


def _pack(arrs):
    flat = []
    for a in arrs:
        v = a.reshape(-1)
        flat.append(jnp.pad(v, (0, (-v.shape[0]) % LANES)))
    v = jnp.concatenate(flat)
    v = jnp.pad(v, (0, (-v.shape[0]) % (8 * LANES)))
    return v.reshape(-1, LANES)


def _unpack(packed, shapes):
    v, out, o = packed.reshape(-1), [], 0
    for sh in shapes:
        n = math.prod(sh)
        out.append(v[o:o + n].reshape(sh))
        o += n + (-n) % LANES
    return out


def _cols_gathered(g):
    nd, nl, r, c = g.shape
    return jnp.transpose(g, (1, 2, 0, 3)).reshape(nl, r, nd * c)


def _rows_gathered(g):
    nd, nl, r, c = g.shape
    return jnp.transpose(g, (1, 0, 2, 3)).reshape(nl, nd * r, c)


def _scatter_layout(g, cols, core):
    nl = g.shape[0]
    if cols:
        r, c = g.shape[1], g.shape[2] // N_DEV
        t = jnp.transpose(g.reshape(nl, r, N_CHIP, 2, c), (3, 2, 0, 1, 4))
    else:
        r, c = g.shape[1] // N_DEV, g.shape[2]
        t = jnp.transpose(g.reshape(nl, N_CHIP, 2, r, c), (2, 1, 0, 3, 4))
    return jnp.where(core == 0, t, t[::-1]).astype(BF16)


def _ffn_fwd(h, nw, w_in, w_out, cfg, tag):
    s, d, f = cfg["s"], cfg["d"], cfg["f"]
    t = _tile(s, 512, 8)
    xn, = _rw_fwd(_rms_fn, f"{tag}_norm", (1, s // t), [(h, _rows(t, d), None), (nw, _par(d), None)],
                  [(jax.ShapeDtypeStruct((s, d), BF16), _rows(t, d), None)])
    gu = _matmul(xn, w_in, mode="nn", name=f"{tag}_in", out_dtype=BF16)
    t2 = _tile(s, 256, 8)
    act, = _rw_fwd(_swiglu_fn, f"{tag}_act", (1, s // t2), [(gu, _rows(t2, 2 * f), [f, f])],
                   [(jax.ShapeDtypeStruct((s, f), BF16), _rows(t2, f), None)])
    h_out = _matmul(act, w_out, mode="nn", name=f"{tag}_out", res=h, scale=0.5)
    return h_out, dict(h=h, xn=xn, gu=gu, act=act)


def _ffn_bwd(gh, r, nw, w_in, w_out, cfg, tag):
    s, d, f = cfg["s"], cfg["d"], cfg["f"]
    d_wout = _matmul(r["act"], gh, mode="tn", name=f"{tag}_dwout", scale=0.5)
    d_act = _matmul(gh, w_out, mode="nt", name=f"{tag}_dact", scale=0.5, out_dtype=BF16)
    t2 = _tile(s, 256, 8)
    d_gu, = _rw_bwd(_swiglu_fn, f"{tag}_dgu", (1, s // t2), [(r["gu"], _rows(t2, 2 * f), [f, f])],
                    [(d_act, _rows(t2, f), None)], [(0, jax.ShapeDtypeStruct((s, 2 * f), BF16), _rows(t2, 2 * f), "tile")])
    d_win = _matmul(r["xn"], d_gu, mode="tn", name=f"{tag}_dwin")
    d_xn = _matmul(d_gu, w_in, mode="nt", name=f"{tag}_dxn", out_dtype=BF16)
    t = _tile(s, 512, 8)
    d_h, d_nw = _rw_bwd(_rms_fn, f"{tag}_dnorm", (1, s // t), [(r["h"], _rows(t, d), None), (nw, _par(d), None)],
                        [(d_xn, _rows(t, d), None)],
                        [(0, jax.ShapeDtypeStruct((s, d), F32), _rows(t, d), "tile"),
                         (1, jax.ShapeDtypeStruct((1, d), F32), _par(d), "acc_all")],
                        add=(gh, _rows(t, d), 0))
    return d_h, (d_nw, d_win, d_wout)


def _mix_fwd(h, p, cfg, tag):
    s, d, o = cfg["s"], cfg["d"], cfg["offs"]
    t = _tile(s, 512, 8)
    u, = _rw_fwd(_rms_fn, f"{tag}_norm", (1, s // t), [(h, _rows(t, d), None), (p["mix_norm"], _par(d), None)],
                 [(jax.ShapeDtypeStruct((s, d), BF16), _rows(t, d), None)])
    proj = _matmul(u, p["w_in"], mode="nn", name=f"{tag}_in", out_dtype=BF16)
    small = _matmul(u, p["w_small"], mode="nn", name=f"{tag}_insmall")
    cw, tr = cfg["cw"], _tile(s, 512, 8)
    xbc_c = _conv_fwd(proj, o["xbc"] // cw, p["ssm_conv_w"], p["ssm_conv_b"], name=f"{tag}_sconv", cw=cw, tr=tr)
    qkv_c = _conv_fwd(proj, o["qkv"] // cw, p["dn_conv_w"], jnp.zeros((1, cfg["cd"]), F32), name=f"{tag}_dconv", cw=cw, tr=tr)
    y, s_states = _ssd_fwd(xbc_c, small, p["ssm_dtb"], p["ssm_alog"], p["ssm_dsk"], cfg, f"{tag}_ssd")
    hp, inner, g = cfg["hp"], cfg["inner"], SSM_GROUPS
    t4 = _tile(s, 512, 8)
    zs_blk = o["zs"] // hp
    y_s, = _rw_fwd(_ssm_out_fn, f"{tag}_sout", (g, s // t4),
                   [(y, _rows(t4, hp, lambda j: j), None), (proj, _rows(t4, hp, lambda j: zs_blk + j), None),
                    (p["ssm_norm"], _par(hp, True), None)],
                   [(jax.ShapeDtypeStruct((s, inner), BF16), _rows(t4, hp, lambda j: j), None)])
    uu, ww, qd, kd, aq, gl = _gdn_prep_fwd(qkv_c, small, p["dn_alog"], p["dn_dtb"], cfg, f"{tag}_prep")
    o_dn, d_states = _gdn_scan_fwd(qd, aq, uu, ww, kd, gl, cfg, f"{tag}_scan")
    hd, dv = cfg["hd"], cfg["dv"]
    zd_blk = o["zd"] // dv
    o_spec = pl.BlockSpec((None, t4, dv), lambda j, i: (j, i, 0))
    y_d, = _rw_fwd(_dn_out_fn, f"{tag}_dout", (hd, s // t4),
                   [(o_dn, o_spec, None), (proj, _rows(t4, dv, lambda j: zd_blk + j), None), (p["dn_norm"], _par(dv), None)],
                   [(jax.ShapeDtypeStruct((s, cfg["vd"]), BF16), _rows(t4, dv, lambda j: j), None)])
    ps = _matmul(y_s, p["ssm_w_branch"], mode="nn", name=f"{tag}_sbr")
    pd = _matmul(y_d, p["dn_w_branch"], mode="nn", name=f"{tag}_dbr")
    t6 = _tile(s, 256, 8)
    merged, = _rw_fwd(_merge_fn, f"{tag}_merge", (1, s // t6),
                      [(proj, _rows(t6, 2 * d, o["gates"] // (2 * d)), [d, d]), (ps, _rows(t6, d), None), (pd, _rows(t6, d), None)],
                      [(jax.ShapeDtypeStruct((s, d), BF16), _rows(t6, d), None)])
    h_out = _matmul(merged, p["w_out"], mode="nn", name=f"{tag}_out", res=h)
    res = dict(h=h, u=u, proj=proj, small=small, xbc_c=xbc_c, qkv_c=qkv_c, y=y, s_states=s_states, y_s=y_s, uu=uu, ww=ww, qd=qd, kd=kd,
               aq=aq, gl=gl, o_dn=o_dn, d_states=d_states, y_d=y_d, ps=ps, pd=pd, merged=merged)
    return h_out, res


def _mix_bwd(gh, r, p, cfg, tag):
    s, d, o = cfg["s"], cfg["d"], cfg["offs"]
    cw, tr = cfg["cw"], _tile(s, 512, 8)
    hp, inner, g, hd, dv, dk = cfg["hp"], cfg["inner"], SSM_GROUPS, cfg["hd"], cfg["dv"], cfg["dk"]
    proj = r["proj"]
    grads = {}
    grads["w_out"] = _matmul(r["merged"], gh, mode="tn", name=f"{tag}_dwout")
    d_merged = _matmul(gh, p["w_out"], mode="nt", name=f"{tag}_dmerged", out_dtype=BF16)
    dproj = jax.ShapeDtypeStruct((s, cfg["pw_main"]), BF16)
    t6 = _tile(s, 256, 8)
    gates_spec = _rows(t6, 2 * d, o["gates"] // (2 * d))
    dproj, d_ps, d_pd = _rw_bwd(
        _merge_fn, f"{tag}_dmerge", (1, s // t6),
        [(proj, gates_spec, [d, d]), (r["ps"], _rows(t6, d), None), (r["pd"], _rows(t6, d), None)],
        [(d_merged, _rows(t6, d), None)],
        [(0, dproj, gates_spec, "tile"), (1, jax.ShapeDtypeStruct((s, d), BF16), _rows(t6, d), "tile"),
         (2, jax.ShapeDtypeStruct((s, d), BF16), _rows(t6, d), "tile")])
    grads["ssm_w_branch"] = _matmul(r["y_s"], d_ps, mode="tn", name=f"{tag}_dwsbr")
    grads["dn_w_branch"] = _matmul(r["y_d"], d_pd, mode="tn", name=f"{tag}_dwdbr")
    d_ys = _matmul(d_ps, p["ssm_w_branch"], mode="nt", name=f"{tag}_dys", out_dtype=BF16)
    d_yd = _matmul(d_pd, p["dn_w_branch"], mode="nt", name=f"{tag}_dyd", out_dtype=BF16)
    t4 = _tile(s, 512, 8)
    zs_blk, zd_blk = o["zs"] // hp, o["zd"] // dv
    zs_spec = _rows(t4, hp, lambda j: zs_blk + j)
    d_y, dproj, grads["ssm_norm"] = _rw_bwd(
        _ssm_out_fn, f"{tag}_dsout", (g, s // t4),
        [(r["y"], _rows(t4, hp, lambda j: j), None), (proj, zs_spec, None), (p["ssm_norm"], _par(hp, True), None)],
        [(d_ys, _rows(t4, hp, lambda j: j), None)],
        [(0, jax.ShapeDtypeStruct((s, inner), F32), _rows(t4, hp, lambda j: j), "tile"),
         (1, jax.ShapeDtypeStruct(dproj.shape, BF16), zs_spec, "tile"),
         (2, jax.ShapeDtypeStruct((1, inner), F32), _par(hp, True), "acc_row")],
        alias=(dproj, 1))
    o_spec = pl.BlockSpec((None, t4, dv), lambda j, i: (j, i, 0))
    zd_spec = _rows(t4, dv, lambda j: zd_blk + j)
    d_o, dproj, grads["dn_norm"] = _rw_bwd(
        _dn_out_fn, f"{tag}_ddout", (hd, s // t4),
        [(r["o_dn"], o_spec, None), (proj, zd_spec, None), (p["dn_norm"], _par(dv), None)],
        [(d_yd, _rows(t4, dv, lambda j: j), None)],
        [(0, jax.ShapeDtypeStruct((hd, s, dv), F32), o_spec, "tile"),
         (1, jax.ShapeDtypeStruct(dproj.shape, BF16), zd_spec, "tile"),
         (2, jax.ShapeDtypeStruct((1, dv), F32), _par(dv), "acc_all")],
        alias=(dproj, 1))
    d_xs, d_bm, d_cm, dsm_s, g_dtb, g_alog, g_dsk = _ssd_bwd(
        r["xbc_c"], r["small"], p["ssm_dtb"], p["ssm_alog"], p["ssm_dsk"], r["s_states"], d_y, cfg, f"{tag}_dssd")
    grads["ssm_dt_bias"], grads["ssm_a_log"], grads["ssm_d"] = (v[0, :cfg["hs"]] for v in (g_dtb, g_alog, g_dsk))
    dws, dbs, col = [], [], 0
    for nm, dy in (("xs", d_xs), ("bm", d_bm), ("cm", d_cm)):
        wd = dy.shape[1]
        dproj, dw_, db_ = _conv_bwd(proj, (o["xbc"] + col) // cw, p["ssm_conv_w"][:, col:col + wd],
                                    p["ssm_conv_b"][:, col:col + wd], dy, dproj, (o["xbc"] + col) // cw,
                                    name=f"{tag}_dsconv_{nm}", cw=cw, tr=tr)
        dws.append(dw_)
        dbs.append(db_)
        col += wd
    grads["ssm_conv_w"] = jnp.concatenate(dws, axis=1)
    grads["ssm_conv_b"] = jnp.concatenate(dbs, axis=1)[0]
    cts = _gdn_scan_bwd(r["qd"], r["aq"], r["uu"], r["ww"], r["kd"], r["gl"], r["d_states"], d_o, cfg, f"{tag}_dscan")
    d_qd, d_aq, d_uu, d_ww, d_kd, d_gl = cts
    d_q, d_k, d_v, dsm_d, g_alog_d, g_dtb_d = _gdn_prep_bwd(
        r["qkv_c"], r["small"], p["dn_alog"], p["dn_dtb"], (d_uu, d_ww, d_qd, d_kd, d_aq, d_gl), cfg, f"{tag}_dprep")
    a0 = cfg["hs"] + hd
    grads["dn_a_log"], grads["dn_dt_bias"] = g_alog_d[0, a0:a0 + hd], g_dtb_d[0, a0:a0 + hd]
    dws, col = [], 0
    zero_b = jnp.zeros((1, cfg["cd"]), F32)
    for nm, dy in (("q", d_q), ("k", d_k), ("v", d_v)):
        wd = dy.shape[1]
        dproj, dw_, _ = _conv_bwd(proj, (o["qkv"] + col) // cw, p["dn_conv_w"][:, col:col + wd], zero_b[:, col:col + wd], dy,
                                  dproj, (o["qkv"] + col) // cw, name=f"{tag}_ddconv_{nm}", cw=cw, tr=tr)
        dws.append(dw_)
        col += wd
    grads["dn_conv_w"] = jnp.concatenate(dws, axis=1)
    d_small = _dsmall(dsm_s, dsm_d, f"{tag}_dsmall")
    grads["w_in"] = jnp.concatenate([_matmul(r["u"], dproj, mode="tn", name=f"{tag}_dwin"),
                                     _matmul(r["u"], d_small, mode="tn", name=f"{tag}_dwinsmall")], axis=1)
    d_u = _matmul(d_small, p["w_small"], mode="nt", name=f"{tag}_dusmall")
    d_u = _matmul(dproj, p["w_in"], mode="nt", name=f"{tag}_du", res=d_u, out_dtype=BF16)
    t = _tile(s, 512, 8)
    d_h, grads["mix_norm"] = _rw_bwd(
        _rms_fn, f"{tag}_dnorm", (1, s // t), [(r["h"], _rows(t, d), None), (p["mix_norm"], _par(d), None)],
        [(d_u, _rows(t, d), None)],
        [(0, jax.ShapeDtypeStruct((s, d), F32), _rows(t, d), "tile"), (1, jax.ShapeDtypeStruct((1, d), F32), _par(d), "acc_all")],
        add=(gh, _rows(t, d), 0))
    return d_h, grads


_BIG = ("ffn1_w_in", "ffn1_w_out", "w_in", "ssm_w_branch", "dn_w_branch", "w_out", "ffn2_w_in", "ffn2_w_out")
_COL_SHARDED = ("ffn1_w_in", "w_in", "ffn2_w_in")
_CONV = ("ssm_conv_w", "dn_conv_w")
_NAMES = ("ffn1_norm", "ffn1_w_in", "ffn1_w_out", "mix_norm", "w_in", "ssm_conv_w", "ssm_conv_b", "ssm_dt_bias", "ssm_a_log",
          "ssm_d", "ssm_norm", "ssm_w_branch", "dn_conv_w", "dn_dt_bias", "dn_a_log", "dn_norm", "dn_w_branch", "w_out",
          "ffn2_norm", "ffn2_w_in", "ffn2_w_out", "final_norm")


def kernel(x, ffn1_norm, ffn1_w_in, ffn1_w_out, mix_norm, w_in, ssm_conv_w, ssm_conv_b, ssm_dt_bias, ssm_a_log, ssm_d, ssm_norm, ssm_w_branch, dn_conv_w, dn_dt_bias, dn_a_log, dn_norm, dn_w_branch, w_out, ffn2_norm, ffn2_w_in, ffn2_w_out, final_norm, loss_target, m_ffn1_norm, m_ffn1_w_in, m_ffn1_w_out, m_mix_norm, m_w_in, m_ssm_conv_w, m_ssm_conv_b, m_ssm_dt_bias, m_ssm_a_log, m_ssm_d, m_ssm_norm, m_ssm_w_branch, m_dn_conv_w, m_dn_dt_bias, m_dn_a_log, m_dn_norm, m_dn_w_branch, m_w_out, m_ffn2_norm, m_ffn2_w_in, m_ffn2_w_out, m_final_norm, v_ffn1_norm, v_ffn1_w_in, v_ffn1_w_out, v_mix_norm, v_w_in, v_ssm_conv_w, v_ssm_conv_b, v_ssm_dt_bias, v_ssm_a_log, v_ssm_d, v_ssm_norm, v_ssm_w_branch, v_dn_conv_w, v_dn_dt_bias, v_dn_a_log, v_dn_norm, v_dn_w_branch, v_w_out, v_ffn2_norm, v_ffn2_w_in, v_ffn2_w_out, v_final_norm):
    args = locals()
    w = {n: args[n] for n in _NAMES}
    mom = {n: args["m_" + n] for n in _NAMES}
    var = {n: args["v_" + n] for n in _NAMES}
    cfg = _config(x, ffn1_w_out, ssm_conv_b, ssm_dt_bias, ssm_norm, dn_conv_w, dn_dt_bias, dn_norm, dn_w_branch)
    depth, s, d = ffn1_norm.shape[0], cfg["s"], cfg["d"]
    me = 4 * lax.axis_index("x") + 2 * lax.axis_index("y") + lax.axis_index("c")

    gathered = _gather_all([w[n].astype(BF16) for n in _BIG] + [w[n] for n in _CONV], "gather_weights")
    full = {}
    for n, g in zip(_BIG + _CONV, gathered):
        full[n] = _cols_gathered(g) if (n in _COL_SHARDED or n in _CONV) else _rows_gathered(g)

    hs, hd = cfg["hs"], cfg["hd"]
    layers = []
    for l in range(depth):
        w_perm = _permute_w_in(full["w_in"][l], cfg)
        layers.append(dict(
            ffn1_norm=ffn1_norm[l][None], ffn1_w_in=full["ffn1_w_in"][l], ffn1_w_out=full["ffn1_w_out"][l],
            mix_norm=mix_norm[l][None], w_in=w_perm[:, :cfg["pw_main"]], w_small=w_perm[:, cfg["pw_main"]:],
            ssm_conv_w=full["ssm_conv_w"][l], ssm_conv_b=ssm_conv_b[l][None],
            ssm_dtb=_lane_row(ssm_dt_bias[l], 0), ssm_alog=_lane_row(ssm_a_log[l], 0), ssm_dsk=_lane_row(ssm_d[l], 0),
            ssm_norm=ssm_norm[l][None], ssm_w_branch=full["ssm_w_branch"][l], dn_conv_w=full["dn_conv_w"][l],
            dn_dtb=_lane_row(dn_dt_bias[l], hs + hd), dn_alog=_lane_row(dn_a_log[l], hs + hd), dn_norm=dn_norm[l][None],
            dn_w_branch=full["dn_w_branch"][l], w_out=full["w_out"][l],
            ffn2_norm=ffn2_norm[l][None], ffn2_w_in=full["ffn2_w_in"][l], ffn2_w_out=full["ffn2_w_out"][l]))

    h = x.reshape(s, d)
    saved = []
    for l, p in enumerate(layers):
        h, r1 = _ffn_fwd(h, p["ffn1_norm"], p["ffn1_w_in"], p["ffn1_w_out"], cfg, f"l{l}_ffn1")
        h, rm = _mix_fwd(h, p, cfg, f"l{l}_mix")
        h, r2 = _ffn_fwd(h, p["ffn2_norm"], p["ffn2_w_in"], p["ffn2_w_out"], cfg, f"l{l}_ffn2")
        saved.append((r1, rm, r2))
    loss_blk, gh, g_final = _loss_head(h, loss_target.reshape(s, d), final_norm[None], "loss_head")
    loss = lax.psum(loss_blk[0, 0], ("x", "y", "c"))

    lg = [None] * depth
    for l in reversed(range(depth)):
        p, (r1, rm, r2) = layers[l], saved[l]
        gh, (g_n2, g_win2, g_wout2) = _ffn_bwd(gh, r2, p["ffn2_norm"], p["ffn2_w_in"], p["ffn2_w_out"], cfg, f"l{l}_ffn2")
        gh, gm = _mix_bwd(gh, rm, p, cfg, f"l{l}_mix")
        gh, (g_n1, g_win1, g_wout1) = _ffn_bwd(gh, r1, p["ffn1_norm"], p["ffn1_w_in"], p["ffn1_w_out"], cfg, f"l{l}_ffn1")
        gm["w_in"] = _unpermute_w_in(gm["w_in"], cfg)
        gm.update(ffn1_norm=g_n1[0], ffn1_w_in=g_win1, ffn1_w_out=g_wout1, ffn2_norm=g_n2[0], ffn2_w_in=g_win2,
                  ffn2_w_out=g_wout2, mix_norm=gm["mix_norm"][0], ssm_norm=gm["ssm_norm"][0], dn_norm=gm["dn_norm"][0])
        lg[l] = gm
    grad_x = gh.reshape(x.shape)
    local = {n: jnp.stack([lg[l][n] for l in range(depth)]) for n in _NAMES if n != "final_norm"}
    local["final_norm"] = g_final[0]

    shares = [_scatter_layout(local[n], n in _COL_SHARDED, lax.axis_index("c")) for n in _BIG]
    from_sibling = _sibling_swap(shares, "swap_grads")
    chip_sums = [_pair_sum(a, b, f"pair_sum_{n}") for n, a, b in zip(_BIG, shares, from_sibling)]
    parts = _chip_scatter(chip_sums, "scatter_grads")
    out_g, out_d, out_m, out_v = {}, {}, {}, {}
    for n, pt in zip(_BIG, parts):
        out_g[n], out_d[n], out_m[n], out_v[n] = _adamw_sum(pt, w[n], mom[n], var[n], f"adamw_{n}")

    small = [n for n in _NAMES if n not in _BIG]
    packed, = _gather_all([_pack([local[n] for n in small])], "gather_small_grads")
    total = _unpack(_sum_parts(packed, "sum_small_grads"), [local[n].shape for n in small])
    for n, g in zip(small, total):
        if n in _CONV:
            c = w[n].shape[2]
            g = lax.dynamic_slice_in_dim(g, me * c, c, axis=2)
        out_g[n] = g
    shapes = [w[n].shape for n in small]
    upd = _adamw_flat(_pack([out_g[n] for n in small]), _pack([w[n] for n in small]), _pack([mom[n] for n in small]),
                      _pack([var[n] for n in small]), "adamw_small")
    for dst, pk in zip((out_d, out_m, out_v), upd):
        for n, a in zip(small, _unpack(pk, shapes)):
            dst[n] = a

    return (loss, grad_x, *[out_g[n] for n in _NAMES], *[out_d[n] for n in _NAMES], *[out_m[n] for n in _NAMES],
            *[out_v[n] for n in _NAMES])
```

```python
import functools
import math

import jax
import jax.numpy as jnp
from jax import lax
from jax.experimental import pallas as pl
from jax.experimental.pallas import tpu as pltpu

F32, BF16 = jnp.float32, jnp.bfloat16
MESH = pl.DeviceIdType.MESH

N_DEV = 8
N_CHIP = 4
EPS = 1e-6
CONV_K = 4
SSM_GROUPS = 4
SSM_CHUNK = 128
DN_CHUNK = 64
ADAM_LR, ADAM_B1, ADAM_B2, ADAM_EPS, ADAM_WD, ADAM_STEP = 0.001, 0.9, 0.999, 1e-08, 0.01, 10

V7X_VMEM_BYTES = 64 * 1024 * 1024
VMEM_LIMIT = 52 * 1024 * 1024
MATMUL_VMEM_BUDGET = 44 * 1024 * 1024
LANES = 128
SMALL_W = 256
SMALL_R = 128
CONV_HALO = 8
HALO_BLK = 16
CONV_STRIP = 32


def _tile(n, target, quantum):
    if n <= target:
        return n
    t = (target // quantum) * quantum
    while t >= quantum:
        if n % t == 0:
            return t
        t -= quantum
    return n


def _cp(*sem):
    return pltpu.CompilerParams(dimension_semantics=sem, vmem_limit_bytes=VMEM_LIMIT)


def _softplus(x):
    return jnp.maximum(x, 0.0) + jnp.log1p(jnp.exp(-jnp.abs(x)))


def _silu(x):
    return x * jax.nn.sigmoid(x)


def _bdot(a, b, dims):
    return lax.dot_general(a.astype(BF16), b.astype(BF16), dims, preferred_element_type=F32)


_NN = (((1,), (0,)), ((), ()))
_NT = (((1,), (1,)), ((), ()))
_TN = (((0,), (0,)), ((), ()))
_MM_DIMS = {"nn": _NN, "nt": _NT, "tn": _TN}


def _mm3(spec, a, b):
    ah, bh = a.astype(BF16), b.astype(BF16)
    al, bl = (a - ah.astype(F32)).astype(BF16), (b - bh.astype(F32)).astype(BF16)
    e = lambda x, y: jnp.einsum(spec, x, y, preferred_element_type=F32)
    return e(ah, bh) + (e(ah, bl) + e(al, bh))


@jax.custom_vjp
def _pmm(a, b):
    return _mm3("bij,bjk->bik", a, b)


def _pmm_fwd(a, b):
    return _pmm(a, b), (a, b)


def _pmm_bwd(res, g):
    a, b = res
    return _mm3("bik,bjk->bij", g, b), _mm3("bji,bjk->bik", a, g)


_pmm.defvjp(_pmm_fwd, _pmm_bwd)


@jax.custom_vjp
def _neumann_inv(n):
    c = n.shape[-1]
    r = lax.broadcasted_iota(jnp.int32, (c, c), 0)
    cc = lax.broadcasted_iota(jnp.int32, (c, c), 1)
    t, pw = (r == cc).astype(F32)[None] + n, n
    for _ in range(int(math.log2(c)) - 1):
        pw = _mm3("bij,bjk->bik", pw, pw)
        t = t + _mm3("bij,bjk->bik", t, pw)
    return t


def _neumann_fwd(n):
    t = _neumann_inv(n)
    return t, t


def _neumann_bwd(t, g):
    return (_mm3("bik,bjk->bij", _mm3("bji,bjk->bik", t, g), t),)


_neumann_inv.defvjp(_neumann_fwd, _neumann_bwd)


def _matmul(a, b, *, mode, name, out_dtype=F32, res=None, scale=1.0):
    if mode == "nn":
        (m, k), (k2, n) = a.shape, b.shape
    elif mode == "nt":
        (m, k), (n, k2) = a.shape, b.shape
    else:
        (k, m), (k2, n) = a.shape, b.shape
    assert k == k2, (name, a.shape, b.shape)
    has_res = res is not None
    tn = _tile(n, 1408, LANES)
    shapes = ((512, 4096), (1024, 2048), (1024, 1024), (512, 512)) if mode == "tn" else (
        (1024, 2816), (1024, 2048), (1024, 1408), (1024, 1024), (512, 512))
    for rows, depth in shapes:
        tm, tk = _tile(m, rows, LANES), _tile(k, depth, LANES)
        need = 2 * tk * (tm * a.dtype.itemsize + tn * b.dtype.itemsize) + tm * tn * (4 + 2 * jnp.dtype(out_dtype).itemsize)
        need += 2 * tm * tn * res.dtype.itemsize if has_res else 0
        if need <= MATMUL_VMEM_BUDGET:
            break
    nk = k // tk
    dims = _MM_DIMS[mode]
    a_spec = pl.BlockSpec((tk, tm), lambda i, j, q: (q, i)) if mode == "tn" else pl.BlockSpec((tm, tk), lambda i, j, q: (i, q))
    b_spec = pl.BlockSpec((tn, tk), lambda i, j, q: (j, q)) if mode == "nt" else pl.BlockSpec((tk, tn), lambda i, j, q: (q, j))
    o_spec = pl.BlockSpec((tm, tn), lambda i, j, q: (i, j))

    def body(*refs):
        a_ref, b_ref = refs[0], refs[1]
        res_ref = refs[2] if has_res else None
        o_ref = refs[3 if has_res else 2]
        acc_ref = refs[-1] if nk > 1 else None

        def finish(acc):
            val = acc * scale if scale != 1.0 else acc
            if has_res:
                val = res_ref[...].astype(F32) + val
            o_ref[...] = val.astype(o_ref.dtype)

        if nk == 1:
            finish(_bdot(a_ref[...], b_ref[...], dims))
        else:
            q = pl.program_id(2)

            @pl.when(q == 0)
            def _():
                acc_ref[...] = jnp.zeros(acc_ref.shape, F32)

            acc_ref[...] += _bdot(a_ref[...], b_ref[...], dims)

            @pl.when(q == nk - 1)
            def _():
                finish(acc_ref[...])

    ins = [a, b] + ([res] if has_res else [])
    in_specs = [a_spec, b_spec] + ([o_spec] if has_res else [])
    return pl.pallas_call(
        body, name=name, grid=(m // tm, n // tn, nk), in_specs=in_specs, out_specs=o_spec,
        out_shape=jax.ShapeDtypeStruct((m, n), out_dtype),
        scratch_shapes=[pltpu.VMEM((tm, tn), F32)] if nk > 1 else [],
        compiler_params=_cp("parallel", "parallel", "arbitrary"),
    )(*ins)


def _read(ref, split, rows):
    rows = slice(None) if ref.shape[0] == 1 else rows
    if split is None:
        return [ref[rows, :].astype(F32)]
    out, off = [], 0
    for w in split:
        out.append(ref[rows, off:off + w].astype(F32))
        off += w
    return out


def _write(ref, vals, split, rows):
    if split is None:
        ref[rows, :] = vals[0].astype(ref.dtype)
        return
    off = 0
    for w, v in zip(split, vals):
        ref[rows, off:off + w] = v.astype(ref.dtype)
        off += w


def _rw_fwd(fn, name, grid, ins, outs):
    n_in = len(ins)

    def body(*refs):
        rows = slice(None)
        args = []
        for r, (_, _, split) in zip(refs[:n_in], ins):
            args += _read(r, split, rows)
        vals = list(fn(*args))
        for r, (_, _, split) in zip(refs[n_in:], outs):
            n = 1 if split is None else len(split)
            _write(r, vals[:n], split, rows)
            vals = vals[n:]

    return pl.pallas_call(
        body, name=name, grid=grid, in_specs=[s for _, s, _ in ins], out_specs=[s for _, s, _ in outs],
        out_shape=[o for o, _, _ in outs], compiler_params=_cp("parallel", "parallel"),
    )(*[a for a, _, _ in ins])


def _rw_bwd(fn, name, grid, ins, cts, grads, add=None, alias=None):
    n_in, n_ct = len(ins), len(cts)
    n_fixed = n_in + n_ct + (1 if add is not None else 0) + (1 if alias is not None else 0)
    arg_pos, pos = [], 0
    for _, _, split in ins:
        n = 1 if split is None else len(split)
        arg_pos.append((pos, n))
        pos += n

    def body(*refs):
        out_refs = refs[n_fixed:]
        rows = slice(None)
        wrt = []
        for idx, _, _, _ in grads:
            p, n = arg_pos[idx]
            wrt += list(range(p, p + n))
        args = []
        for r, (_, _, split) in zip(refs[:n_in], ins):
            args += _read(r, split, rows)
        ct_vals = []
        for r, (_, _, split) in zip(refs[n_in:n_in + n_ct], cts):
            ct_vals += _read(r, split, rows)

        def f(*w):
            full = list(args)
            for p, v in zip(wrt, w):
                full[p] = v
            return tuple(fn(*full))

        _, vjp = jax.vjp(f, *[args[p] for p in wrt])
        g = list(vjp(tuple(ct_vals)))
        first_row = pl.program_id(1) == 0
        first_all = jnp.logical_and(pl.program_id(0) == 0, first_row)
        for gi, (idx, _, _, mode) in enumerate(grads):
            n = arg_pos[idx][1]
            vals, g = g[:n], g[n:]
            o = out_refs[gi]
            if mode == "tile":
                if add is not None and add[2] == gi:
                    vals = [vals[0] + refs[n_in + n_ct][...].astype(F32)]
                _write(o, vals, ins[idx][2], rows)
            else:
                first = first_row if mode == "acc_row" else first_all

                @pl.when(first)
                def _(o=o, val=vals[0]):
                    o[...] = val

                @pl.when(jnp.logical_not(first))
                def _(o=o, val=vals[0]):
                    o[...] += val

    arrays = [a for a, _, _ in ins] + [a for a, _, _ in cts]
    in_specs = [s for _, s, _ in ins] + [s for _, s, _ in cts]
    if add is not None:
        arrays.append(add[0])
        in_specs.append(add[1])
    aliases = {}
    if alias is not None:
        aliases = {len(arrays): alias[1]}
        arrays.append(alias[0])
        in_specs.append(pl.BlockSpec(memory_space=pl.ANY))
    return pl.pallas_call(
        body, name=name, grid=grid, in_specs=in_specs, out_specs=[s for _, _, s, _ in grads],
        out_shape=[o for _, o, _, _ in grads], input_output_aliases=aliases,
        compiler_params=_cp("arbitrary", "arbitrary"),
    )(*arrays)


def _rows(t, w, col=0):
    if callable(col):
        return pl.BlockSpec((t, w), lambda j, i: (i, col(j)))
    return pl.BlockSpec((t, w), lambda j, i: (i, col))


def _par(w, per_col=False):
    return pl.BlockSpec((1, w), (lambda j, i: (0, j)) if per_col else (lambda j, i: (0, 0)))


def _rms_fn(x, w):
    return (x * lax.rsqrt(jnp.mean(x * x, axis=-1, keepdims=True) + EPS) * w,)


def _swiglu_fn(gate, up):
    return (_silu(gate) * up,)


def _ssm_out_fn(y, z, w):
    yg = y * _silu(z)
    return (yg * lax.rsqrt(jnp.mean(yg * yg, axis=-1, keepdims=True) + EPS) * w,)


def _dn_out_fn(o, z, w):
    return (o * lax.rsqrt(jnp.mean(o * o, axis=-1, keepdims=True) + EPS) * w * _silu(z),)


def _merge_fn(gs, gd, ps, pd):
    return (jax.nn.sigmoid(gs) * ps + jax.nn.sigmoid(gd) * pd,)


def _dsmall(a, b, name):
    s = a.shape[0]
    t = _tile(s, 1024, 16)

    def body(a_r, b_r, o_r):
        o_r[:, :SMALL_R] = (a_r[...] + b_r[...]).astype(o_r.dtype)
        o_r[:, SMALL_R:] = jnp.zeros((t, SMALL_W - SMALL_R), o_r.dtype)

    row = pl.BlockSpec((t, SMALL_R), lambda i: (i, 0))
    return pl.pallas_call(
        body, name=name, grid=(s // t,), in_specs=[row, row], out_specs=pl.BlockSpec((t, SMALL_W), lambda i: (i, 0)),
        out_shape=jax.ShapeDtypeStruct((s, SMALL_W), BF16), compiler_params=_cp("parallel"),
    )(a, b)


def _conv_fwd(x, x_col0, w, b, *, name, cw, tr):
    s, c = x.shape[0], w.shape[1]
    nr, ncol, hb = s // tr, c // cw, tr // HALO_BLK
    rs, lo = CONV_STRIP, CONV_HALO - (CONV_K - 1)

    def body(x_ref, prev_ref, w_ref, b_ref, o_ref, buf):
        i = pl.program_id(1)
        buf[0:CONV_HALO, :] = jnp.where(i > 0, prev_ref[HALO_BLK - CONV_HALO:, :].astype(F32), 0.0)
        buf[CONV_HALO:, :] = x_ref[...].astype(F32)
        taps = [w_ref[q:q + 1, :] for q in range(CONV_K)]
        bias = b_ref[...]

        def strip(k, carry):
            r0 = pl.multiple_of(k * rs, rs)
            ext = buf[pl.ds(r0, rs + CONV_HALO), :]
            acc = bias + taps[0] * ext[lo:lo + rs]
            for q in range(1, CONV_K):
                acc = acc + taps[q] * ext[lo + q:lo + q + rs]
            o_ref[pl.ds(r0, rs), :] = (acc * pl.reciprocal(1.0 + jnp.exp(-acc), approx=True)).astype(o_ref.dtype)
            return carry

        lax.fori_loop(0, tr // rs, strip, 0)

    return pl.pallas_call(
        body, name=name, grid=(ncol, nr),
        in_specs=[pl.BlockSpec((tr, cw), lambda j, i: (i, x_col0 + j)),
                  pl.BlockSpec((HALO_BLK, cw), lambda j, i: (jnp.maximum(i * hb - 1, 0), x_col0 + j)),
                  pl.BlockSpec((CONV_K, cw), lambda j, i: (0, j)), pl.BlockSpec((1, cw), lambda j, i: (0, j))],
        out_specs=pl.BlockSpec((tr, cw), lambda j, i: (i, j)),
        out_shape=jax.ShapeDtypeStruct((s, c), BF16),
        scratch_shapes=[pltpu.VMEM((CONV_HALO + tr, cw), F32)],
        compiler_params=_cp("parallel", "parallel"),
    )(x, x, w, b)


def _conv_bwd(x, x_col0, w, b, dy, dproj, out_col0, *, name, cw, tr):
    s, c = dy.shape
    nr, ncol = s // tr, c // cw
    hb, last_hb = tr // HALO_BLK, s // HALO_BLK - 1
    hb8, last_hb8 = tr // CONV_HALO, s // CONV_HALO - 1
    ext = tr + CONV_HALO
    rs, lo = CONV_STRIP, CONV_HALO - (CONV_K - 1)
    fresh = isinstance(dproj, jax.ShapeDtypeStruct)

    def body(x_ref, prev_ref, next_ref, dy_ref, dyn_ref, w_ref, b_ref, *rest):
        dx_ref, dw_ref, db_ref, xbuf, gbuf = rest[-5:]
        i = pl.program_id(1)
        xbuf[0:CONV_HALO, :] = jnp.where(i > 0, prev_ref[HALO_BLK - CONV_HALO:, :].astype(F32), 0.0)
        xbuf[CONV_HALO:CONV_HALO + tr, :] = x_ref[...].astype(F32)
        xbuf[CONV_HALO + tr:, :] = next_ref[0:CONV_HALO, :].astype(F32)
        taps = [w_ref[q:q + 1, :] for q in range(CONV_K)]
        bias = b_ref[...]

        def dpre(xe, dy, n):
            pre = bias + taps[0] * xe[lo:lo + n]
            for q in range(1, CONV_K):
                pre = pre + taps[q] * xe[lo + q:lo + q + n]
            sg = pl.reciprocal(1.0 + jnp.exp(-pre), approx=True)
            return dy * (sg * (1.0 + pre * (1.0 - sg)))

        def strip1(k, carry):
            r0 = pl.multiple_of(k * rs, rs)
            gbuf[pl.ds(r0, rs), :] = dpre(xbuf[pl.ds(r0, rs + CONV_HALO), :], dy_ref[pl.ds(r0, rs), :].astype(F32), rs)
            return carry

        lax.fori_loop(0, tr // rs, strip1, 0)
        gbuf[tr:, :] = dpre(xbuf[tr:, :], jnp.where(i < nr - 1, dyn_ref[...].astype(F32), 0.0), CONV_HALO)

        def fold(v):
            acc = v[0:8]
            for a in range(1, rs // 8):
                acc = acc + v[8 * a:8 * a + 8]
            return acc

        def strip2(k, carry):
            r0 = pl.multiple_of(k * rs, rs)
            ge = gbuf[pl.ds(r0, rs + CONV_HALO), :]
            x_own = xbuf[pl.ds(r0 + CONV_HALO, rs), :]
            dx = jnp.zeros((rs, cw), F32)
            new = []
            for q in range(CONV_K):
                g_q = ge[CONV_K - 1 - q:CONV_K - 1 - q + rs]
                dx = dx + taps[q] * g_q
                new.append(carry[q] + fold(x_own * g_q))
            dx_ref[pl.ds(r0, rs), :] = dx.astype(dx_ref.dtype)
            return tuple(new) + (carry[CONV_K] + fold(ge[0:rs]),)

        sums = lax.fori_loop(0, tr // rs, strip2, tuple(jnp.zeros((8, cw), F32) for _ in range(CONV_K + 1)))
        dws = [jnp.sum(sums[q], axis=0, keepdims=True) for q in range(CONV_K)]
        dbv = jnp.sum(sums[CONV_K], axis=0, keepdims=True)

        @pl.when(i == 0)
        def _():
            for q in range(CONV_K):
                dw_ref[q:q + 1, :] = dws[q]
            db_ref[...] = dbv

        @pl.when(i > 0)
        def _():
            for q in range(CONV_K):
                dw_ref[q:q + 1, :] += dws[q]
            db_ref[...] += dbv

    xmap = lambda j, i: (i, x_col0 + j)
    ins = [x, x, x, dy, dy, w, b]
    in_specs = [pl.BlockSpec((tr, cw), xmap),
                pl.BlockSpec((HALO_BLK, cw), lambda j, i: (jnp.maximum(i * hb - 1, 0), x_col0 + j)),
                pl.BlockSpec((HALO_BLK, cw), lambda j, i: (jnp.minimum((i + 1) * hb, last_hb), x_col0 + j)),
                pl.BlockSpec((tr, cw), lambda j, i: (i, j)),
                pl.BlockSpec((CONV_HALO, cw), lambda j, i: (jnp.minimum((i + 1) * hb8, last_hb8), j)),
                pl.BlockSpec((CONV_K, cw), lambda j, i: (0, j)), pl.BlockSpec((1, cw), lambda j, i: (0, j))]
    aliases = {}
    if not fresh:
        aliases = {len(ins): 0}
        ins.append(dproj)
        in_specs.append(pl.BlockSpec(memory_space=pl.ANY))
    return pl.pallas_call(
        body, name=name, grid=(ncol, nr), in_specs=in_specs,
        out_specs=[pl.BlockSpec((tr, cw), lambda j, i: (i, out_col0 + j)),
                   pl.BlockSpec((CONV_K, cw), lambda j, i: (0, j)), pl.BlockSpec((1, cw), lambda j, i: (0, j))],
        out_shape=[jax.ShapeDtypeStruct(dproj.shape, dproj.dtype), jax.ShapeDtypeStruct((CONV_K, c), F32),
                   jax.ShapeDtypeStruct((1, c), F32)],
        scratch_shapes=[pltpu.VMEM((CONV_HALO + ext, cw), F32), pltpu.VMEM((ext, cw), F32)],
        input_output_aliases=aliases, compiler_params=_cp("arbitrary", "arbitrary"),
    )(*ins)


def _ssd_gates(small, dtb, alog):
    l = small.shape[0]
    r = lax.broadcasted_iota(jnp.int32, (l, l), 0)
    c = lax.broadcasted_iota(jnp.int32, (l, l), 1)
    dt_all = _softplus(small + dtb)
    return dt_all, _pmm((r >= c).astype(F32)[None], (dt_all * (-jnp.exp(alog)))[None])[0]


def _ssd_chunk(xs, bm, cm, dt_all, acum_all, dsk, st, g, *, hg, p):
    l, w = dt_all.shape
    per = LANES // p
    lane_w = lax.broadcasted_iota(jnp.int32, (1, w), 1)
    lane = lax.broadcasted_iota(jnp.int32, (1, LANES), 1)
    r = lax.broadcasted_iota(jnp.int32, (l, l), 0)
    c = lax.broadcasted_iota(jnp.int32, (l, l), 1)
    tri = r >= c
    eye = (r == c).astype(F32)
    last = (lax.broadcasted_iota(jnp.int32, (l, 1), 0) == l - 1).astype(F32)
    cb = _bdot(cm, bm, _NT)
    ys, sts = [], []
    for q, (x_q, st_q) in enumerate(zip(xs, st)):
        dt_e = jnp.zeros((l, LANES), F32)
        acum_e = jnp.zeros((l, LANES), F32)
        d_e = jnp.zeros((1, LANES), F32)
        decays, masks = [], []
        for jj in range(per):
            mh = (lane_w == g * hg + q * per + jj).astype(F32)
            mj = jnp.logical_and(lane >= jj * p, lane < (jj + 1) * p).astype(F32)
            ac = jnp.sum(acum_all * mh, axis=1, keepdims=True)
            dt_e = dt_e + jnp.sum(dt_all * mh, axis=1, keepdims=True) * mj
            acum_e = acum_e + ac * mj
            d_e = d_e + jnp.sum(dsk * mh, axis=1, keepdims=True) * mj
            seg = ac - jnp.sum(ac * eye, axis=0, keepdims=True)
            decays.append(jnp.where(tri, jnp.exp(jnp.where(tri, seg, 0.0)), 0.0))
            masks.append(mj)
        xdt = x_q * dt_e
        y = x_q * d_e + _bdot(jnp.concatenate([cb * dec for dec in decays], axis=1),
                              jnp.concatenate([xdt * mj for mj in masks], axis=0), _NN)
        a_last = jnp.sum(acum_e * last, axis=0, keepdims=True)
        ys.append(y + jnp.exp(acum_e) * _bdot(cm, st_q, _NN))
        sts.append(st_q * jnp.exp(a_last) + _bdot(bm, xdt * jnp.exp(a_last - acum_e), _TN))
    return tuple(ys) + tuple(sts)


def _pieces(ref, lead=()):
    return [ref[lead + (slice(None), slice(q * LANES, (q + 1) * LANES))].astype(F32) for q in range(ref.shape[-1] // LANES)]


def _ssd_specs(cfg, rev):
    l, hp, n, g = SSM_CHUNK, cfg["hp"], cfg["n"], SSM_GROUPS
    nc = cfg["s"] // l
    cc = (lambda c: nc - 1 - c) if rev else (lambda c: c)
    nb = cfg["inner"] // n
    par = pl.BlockSpec((1, SMALL_R), lambda c, q: (0, 0))
    specs = [pl.BlockSpec((l, hp), lambda c, q: (cc(c), q)),
             pl.BlockSpec((l, n), lambda c, q: (cc(c), nb + q)),
             pl.BlockSpec((l, n), lambda c, q: (cc(c), nb + g + q)),
             pl.BlockSpec((l, SMALL_R), lambda c, q: (cc(c), 0)), par, par, par]
    st_spec = pl.BlockSpec((None, None, n, hp), lambda c, q: (cc(c), q, 0, 0))
    y_spec = pl.BlockSpec((l, hp), lambda c, q: (cc(c), q))
    return specs, st_spec, y_spec, nc


def _ssd_fwd(xbc_c, proj, dtb, alog, dsk, cfg, name):
    specs, st_spec, y_spec, nc = _ssd_specs(cfg, False)
    fn = functools.partial(_ssd_chunk, hg=cfg["hg"], p=cfg["p"])
    npc = cfg["hp"] // LANES

    def body(xs, bm, cm, sm, dtb_r, alog_r, dsk_r, y_ref, sts_ref, st, dt_s, ac_s):
        c, g = pl.program_id(0), pl.program_id(1)

        @pl.when(c == 0)
        def _():
            st[g] = jnp.zeros(st.shape[1:], F32)

        @pl.when(g == 0)
        def _():
            dt_s[...], ac_s[...] = _ssd_gates(sm[...], dtb_r[...], alog_r[...])

        sts_ref[...] = st[g]
        out = fn(_pieces(xs), bm[...].astype(F32), cm[...].astype(F32), dt_s[...], ac_s[...], dsk_r[...],
                 _pieces(st, (g,)), g)
        for q in range(npc):
            y_ref[:, q * LANES:(q + 1) * LANES] = out[q]
            st[g, :, q * LANES:(q + 1) * LANES] = out[npc + q]

    return pl.pallas_call(
        body, name=name, grid=(nc, SSM_GROUPS), in_specs=specs, out_specs=[y_spec, st_spec],
        out_shape=[jax.ShapeDtypeStruct((cfg["s"], cfg["inner"]), F32),
                   jax.ShapeDtypeStruct((nc, SSM_GROUPS, cfg["n"], cfg["hp"]), F32)],
        scratch_shapes=[pltpu.VMEM((SSM_GROUPS, cfg["n"], cfg["hp"]), F32)] + [pltpu.VMEM((SSM_CHUNK, SMALL_R), F32)] * 2,
        compiler_params=_cp("arbitrary", "arbitrary"),
    )(xbc_c, xbc_c, xbc_c, proj, dtb, alog, dsk)


def _ssd_bwd(xbc_c, proj, dtb, alog, dsk, states, dy, cfg, name):
    specs, st_spec, y_spec, nc = _ssd_specs(cfg, True)
    l, n, gn = SSM_CHUNK, cfg["n"], SSM_GROUPS * cfg["n"]
    fn = functools.partial(_ssd_chunk, hg=cfg["hg"], p=cfg["p"])
    npc = cfg["hp"] // LANES
    rc = lambda c: nc - 1 - c

    def body(xs, bm, cm, sm, dtb_r, alog_r, dsk_r, sts_ref, dy_ref, dxs, dbm, dcm, dsm, ddtb, dalog, ddsk, dst,
             dt_s, ac_s, gdt_s, gac_s):
        c, g = pl.program_id(0), pl.program_id(1)

        @pl.when(c == 0)
        def _():
            dst[g] = jnp.zeros(dst.shape[1:], F32)

        @pl.when(g == 0)
        def _():
            dt_s[...], ac_s[...] = _ssd_gates(sm[...], dtb_r[...], alog_r[...])
            gdt_s[...] = jnp.zeros(gdt_s.shape, F32)
            gac_s[...] = jnp.zeros(gac_s.shape, F32)

        def f(*a):
            return fn(a[:npc], *a[npc:npc + 5], a[npc + 5:], g)

        _, vjp = jax.vjp(f, *_pieces(xs), bm[...].astype(F32), cm[...].astype(F32), dt_s[...], ac_s[...], dsk_r[...],
                         *_pieces(sts_ref))
        grads = vjp(tuple(_pieces(dy_ref)) + tuple(_pieces(dst, (g,))))
        gb, gc, gdt, gac, g3 = grads[npc:npc + 5]
        for q in range(npc):
            dxs[:, q * LANES:(q + 1) * LANES] = grads[q]
            dst[g, :, q * LANES:(q + 1) * LANES] = grads[npc + 5 + q]
        dbm[...] = gb
        dcm[...] = gc
        gdt_s[...] += gdt
        gac_s[...] += gac
        first = jnp.logical_and(c == 0, g == 0)

        @pl.when(first)
        def _():
            ddsk[...] = g3

        @pl.when(jnp.logical_not(first))
        def _():
            ddsk[...] += g3

        @pl.when(g == SSM_GROUPS - 1)
        def _():
            _, vjp_gates = jax.vjp(_ssd_gates, sm[...], dtb_r[...], alog_r[...])
            gs, g1, g2 = vjp_gates((gdt_s[...], gac_s[...]))
            dsm[...] = gs

            @pl.when(c == 0)
            def _():
                ddtb[...] = g1
                dalog[...] = g2

            @pl.when(c > 0)
            def _():
                ddtb[...] += g1
                dalog[...] += g2

    par = pl.BlockSpec((1, SMALL_R), lambda c, q: (0, 0))
    return pl.pallas_call(
        body, name=name, grid=(nc, SSM_GROUPS), in_specs=specs + [st_spec, y_spec],
        out_specs=[y_spec, pl.BlockSpec((l, n), lambda c, q: (rc(c), q)), pl.BlockSpec((l, n), lambda c, q: (rc(c), q)),
                   pl.BlockSpec((l, SMALL_R), lambda c, q: (rc(c), 0)), par, par, par],
        out_shape=[jax.ShapeDtypeStruct((cfg["s"], cfg["inner"]), F32), jax.ShapeDtypeStruct((cfg["s"], gn), F32),
                   jax.ShapeDtypeStruct((cfg["s"], gn), F32), jax.ShapeDtypeStruct((cfg["s"], SMALL_R), F32),
                   jax.ShapeDtypeStruct((1, SMALL_R), F32), jax.ShapeDtypeStruct((1, SMALL_R), F32),
                   jax.ShapeDtypeStruct((1, SMALL_R), F32)],
        scratch_shapes=[pltpu.VMEM((SSM_GROUPS, cfg["n"], cfg["hp"]), F32)] + [pltpu.VMEM((SSM_CHUNK, SMALL_R), F32)] * 4,
        compiler_params=_cp("arbitrary", "arbitrary"),
    )(xbc_c, xbc_c, xbc_c, proj, dtb, alog, dsk, states, dy)


def _gdn_gates(small, alog, dtb):
    c = DN_CHUNK
    rr, w_ = small.shape
    nb = rr // c
    r = lax.broadcasted_iota(jnp.int32, (c, c), 0)
    cc = lax.broadcasted_iota(jnp.int32, (c, c), 1)
    g_all = -jnp.exp(alog) * _softplus(small + dtb)
    tri_b = jnp.broadcast_to((r >= cc).astype(F32)[None], (nb, c, c))
    return _pmm(tri_b, g_all.reshape(nb, c, w_)), jax.nn.sigmoid(small)


def _gdn_prep(q, k, v, gcum_all, beta_all, h, *, boff, aoff):
    c = DN_CHUNK
    rr, dk = q.shape
    nb, w_ = rr // c, beta_all.shape[1]
    lane = lax.broadcasted_iota(jnp.int32, (1, w_), 1)
    r = lax.broadcasted_iota(jnp.int32, (c, c), 0)
    cc = lax.broadcasted_iota(jnp.int32, (c, c), 1)
    incl, strict = (r >= cc)[None], (r > cc)[None]
    eye = (r == cc).astype(F32)[None]
    gc = jnp.sum(gcum_all * (lane == aoff + h).astype(F32)[None], axis=2, keepdims=True)
    beta = jnp.sum(beta_all * (lane == boff + h).astype(F32), axis=1, keepdims=True).reshape(nb, c, 1)
    qn = (q * lax.rsqrt(jnp.sum(q * q, axis=1, keepdims=True) + EPS) * (dk ** -0.5)).reshape(nb, c, dk)
    kn = (k * lax.rsqrt(jnp.sum(k * k, axis=1, keepdims=True) + EPS)).reshape(nb, c, dk)
    v3 = v.reshape(nb, c, v.shape[1])
    g_row = jnp.sum(gc * eye, axis=1, keepdims=True)
    decay = jnp.where(incl, jnp.exp(jnp.where(incl, gc - g_row, 0.0)), 0.0)
    kb = kn * beta
    kk = jnp.einsum("bik,bjk->bij", kb.astype(BF16), kn.astype(BF16), preferred_element_type=F32)
    neg_m = jnp.where(strict, -kk * decay, 0.0)
    t_inv = _neumann_inv(neg_m)
    eg = jnp.exp(gc)
    u = _pmm(t_inv, v3 * beta)
    w = _pmm(t_inv, kb * eg)
    a_qk = jnp.einsum("bik,bjk->bij", qn.astype(BF16), kn.astype(BF16), preferred_element_type=F32) * decay
    last = (lax.broadcasted_iota(jnp.int32, (1, c, 1), 1) == c - 1).astype(F32)
    g_last = jnp.sum(gc * last, axis=1, keepdims=True)
    q_dec = qn * eg
    k_dec = kn * jnp.exp(g_last - gc)
    gl = jnp.broadcast_to(g_last, (nb, 1, LANES))
    return (u.reshape(rr, -1), w.reshape(rr, dk), q_dec.reshape(rr, dk), k_dec.reshape(rr, dk), a_qk, gl)


def _gdn_prep_specs(cfg):
    rr, hd, dk, dv = cfg["prep_rows"], cfg["hd"], cfg["dk"], cfg["dv"]
    nb = rr // DN_CHUNK
    ins = [pl.BlockSpec((rr, dk), lambda i, h: (i, h)), pl.BlockSpec((rr, dk), lambda i, h: (i, hd + h)),
           pl.BlockSpec((rr, dv), lambda i, h: (i, (2 * hd * dk) // dv + h)),
           pl.BlockSpec((rr, SMALL_R), lambda i, h: (i, 0)),
           pl.BlockSpec((1, SMALL_R), lambda i, h: (0, 0)), pl.BlockSpec((1, SMALL_R), lambda i, h: (0, 0))]
    hs = lambda d: pl.BlockSpec((None, rr, d), lambda i, h: (h, i, 0))
    outs = [hs(dv), hs(dk), hs(dk), hs(dk), pl.BlockSpec((None, nb, DN_CHUNK, DN_CHUNK), lambda i, h: (h, i, 0, 0)),
            pl.BlockSpec((None, nb, 1, LANES), lambda i, h: (h, i, 0, 0))]
    s, nch = cfg["s"], cfg["s"] // DN_CHUNK
    shapes = [jax.ShapeDtypeStruct((hd, s, dv), F32)] + [jax.ShapeDtypeStruct((hd, s, dk), F32)] * 3 + [
        jax.ShapeDtypeStruct((hd, nch, DN_CHUNK, DN_CHUNK), F32), jax.ShapeDtypeStruct((hd, nch, 1, LANES), F32)]
    return ins, outs, shapes


def _gdn_prep_fwd(qkv_c, proj, alog, dtb, cfg, name):
    ins, outs, shapes = _gdn_prep_specs(cfg)
    fn = functools.partial(_gdn_prep, boff=cfg["hs"], aoff=cfg["hs"] + cfg["hd"])

    rr = cfg["prep_rows"]

    def body(q, k, v, sm, al, db, *rest):
        o, (gc_s, be_s) = rest[:-2], rest[-2:]

        @pl.when(pl.program_id(1) == 0)
        def _():
            gc_s[...], be_s[...] = _gdn_gates(sm[...], al[...], db[...])

        vals = fn(q[...].astype(F32), k[...].astype(F32), v[...].astype(F32), gc_s[...], be_s[...], pl.program_id(1))
        for ref, val in zip(o, vals):
            ref[...] = val

    return pl.pallas_call(
        body, name=name, grid=(cfg["s"] // rr, cfg["hd"]), in_specs=ins, out_specs=outs, out_shape=shapes,
        scratch_shapes=[pltpu.VMEM((rr // DN_CHUNK, DN_CHUNK, SMALL_R), F32), pltpu.VMEM((rr, SMALL_R), F32)],
        compiler_params=_cp("parallel", "arbitrary"),
    )(qkv_c, qkv_c, qkv_c, proj, alog, dtb)


def _gdn_prep_bwd(qkv_c, proj, alog, dtb, cts, cfg, name):
    ins, outs, _ = _gdn_prep_specs(cfg)
    rr, hd, dk, dv, s = cfg["prep_rows"], cfg["hd"], cfg["dk"], cfg["dv"], cfg["s"]
    fn = functools.partial(_gdn_prep, boff=cfg["hs"], aoff=cfg["hs"] + cfg["hd"])

    def body(q, k, v, sm, al, db, c0, c1, c2, c3, c4, c5, dq, dkk, dvv, dsm, dal, ddb, gc_s, be_s, ggc_s, gbe_s):
        i, h = pl.program_id(0), pl.program_id(1)

        @pl.when(h == 0)
        def _():
            gc_s[...], be_s[...] = _gdn_gates(sm[...], al[...], db[...])
            ggc_s[...] = jnp.zeros(ggc_s.shape, F32)
            gbe_s[...] = jnp.zeros(gbe_s.shape, F32)

        f = lambda *a: fn(*a, h)
        _, vjp = jax.vjp(f, q[...].astype(F32), k[...].astype(F32), v[...].astype(F32), gc_s[...], be_s[...])
        gq, gk, gv, ggc, gbe = vjp((c0[...], c1[...], c2[...], c3[...], c4[...], c5[...]))
        dq[...] = gq
        dkk[...] = gk
        dvv[...] = gv
        ggc_s[...] += ggc
        gbe_s[...] += gbe

        @pl.when(h == hd - 1)
        def _():
            _, vjp_gates = jax.vjp(_gdn_gates, sm[...], al[...], db[...])
            gs, ga, gd = vjp_gates((ggc_s[...], gbe_s[...]))
            dsm[...] = gs

            @pl.when(i == 0)
            def _():
                dal[...] = ga
                ddb[...] = gd

            @pl.when(i > 0)
            def _():
                dal[...] += ga
                ddb[...] += gd

    par = pl.BlockSpec((1, SMALL_R), lambda i, h: (0, 0))
    return pl.pallas_call(
        body, name=name, grid=(s // rr, hd), in_specs=ins + outs,
        out_specs=[pl.BlockSpec((rr, dk), lambda i, h: (i, h)), pl.BlockSpec((rr, dk), lambda i, h: (i, h)),
                   pl.BlockSpec((rr, dv), lambda i, h: (i, h)), pl.BlockSpec((rr, SMALL_R), lambda i, h: (i, 0)), par, par],
        out_shape=[jax.ShapeDtypeStruct((s, hd * dk), F32), jax.ShapeDtypeStruct((s, hd * dk), F32),
                   jax.ShapeDtypeStruct((s, hd * dv), F32), jax.ShapeDtypeStruct((s, SMALL_R), F32),
                   jax.ShapeDtypeStruct((1, SMALL_R), F32), jax.ShapeDtypeStruct((1, SMALL_R), F32)],
        scratch_shapes=[pltpu.VMEM((rr // DN_CHUNK, DN_CHUNK, SMALL_R), F32), pltpu.VMEM((rr, SMALL_R), F32)] * 2,
        compiler_params=_cp("arbitrary", "arbitrary"),
    )(qkv_c, qkv_c, qkv_c, proj, alog, dtb, *cts)


def _gdn_scan_chunk(qd, aq, u, w, kd, gl, st):
    bm = lambda spec, a, b: jnp.einsum(spec, a.astype(BF16), b.astype(BF16), preferred_element_type=F32)
    v_new = u - bm("hck,hkv->hcv", w, st)
    o = bm("hck,hkv->hcv", qd, st) + bm("hij,hjv->hiv", aq, v_new)
    st_new = st * jnp.exp(gl) + bm("hck,hcv->hkv", kd, v_new)
    return o, st_new


def _gdn_scan_specs(cfg, rev):
    rr, hd, dk, dv = cfg["scan_rows"], cfg["hd"], cfg["dk"], cfg["dv"]
    nb, nblk = rr // DN_CHUNK, cfg["s"] // rr
    ii = (lambda i: nblk - 1 - i) if rev else (lambda i: i)
    hs = lambda d: pl.BlockSpec((hd, rr, d), lambda i: (0, ii(i), 0))
    ins = [hs(dk), pl.BlockSpec((hd, nb, DN_CHUNK, DN_CHUNK), lambda i: (0, ii(i), 0, 0)), hs(dv), hs(dk), hs(dk),
           pl.BlockSpec((hd, nb, 1, LANES), lambda i: (0, ii(i), 0, 0))]
    st_spec = pl.BlockSpec((nb, hd, dk, dv), lambda i: (ii(i), 0, 0, 0))
    return ins, hs(dv), st_spec, nb, nblk


def _gdn_scan_fwd(qd, aq, u, w, kd, gl, cfg, name):
    ins, o_spec, st_spec, nb, nblk = _gdn_scan_specs(cfg, False)
    hd, dk, dv, c = cfg["hd"], cfg["dk"], cfg["dv"], DN_CHUNK

    def body(qd_r, aq_r, u_r, w_r, kd_r, gl_r, o_r, sts_r, st):
        @pl.when(pl.program_id(0) == 0)
        def _():
            st[...] = jnp.zeros(st.shape, F32)

        s = st[...]
        for j in range(nb):
            rows = slice(j * c, (j + 1) * c)
            sts_r[j] = s
            o, s = _gdn_scan_chunk(qd_r[:, rows, :], aq_r[:, j], u_r[:, rows, :], w_r[:, rows, :], kd_r[:, rows, :],
                                   gl_r[:, j], s)
            o_r[:, rows, :] = o
        st[...] = s

    return pl.pallas_call(
        body, name=name, grid=(nblk,), in_specs=ins, out_specs=[o_spec, st_spec],
        out_shape=[jax.ShapeDtypeStruct((hd, cfg["s"], dv), F32),
                   jax.ShapeDtypeStruct((cfg["s"] // c, hd, dk, dv), F32)],
        scratch_shapes=[pltpu.VMEM((hd, dk, dv), F32)], compiler_params=_cp("arbitrary"),
    )(qd, aq, u, w, kd, gl)


def _gdn_scan_bwd(qd, aq, u, w, kd, gl, states, do, cfg, name):
    ins, o_spec, st_spec, nb, nblk = _gdn_scan_specs(cfg, True)
    hd, dk, dv, c = cfg["hd"], cfg["dk"], cfg["dv"], DN_CHUNK

    def body(qd_r, aq_r, u_r, w_r, kd_r, gl_r, sts_r, do_r, dqd, daq, du, dw, dkd, dgl, dst):
        @pl.when(pl.program_id(0) == 0)
        def _():
            dst[...] = jnp.zeros(dst.shape, F32)

        ds = dst[...]
        for j in reversed(range(nb)):
            rows = slice(j * c, (j + 1) * c)
            _, vjp = jax.vjp(_gdn_scan_chunk, qd_r[:, rows, :], aq_r[:, j], u_r[:, rows, :], w_r[:, rows, :],
                             kd_r[:, rows, :], gl_r[:, j], sts_r[j])
            g0, g1, g2, g3, g4, g5, ds = vjp((do_r[:, rows, :], ds))
            dqd[:, rows, :] = g0
            daq[:, j] = g1
            du[:, rows, :] = g2
            dw[:, rows, :] = g3
            dkd[:, rows, :] = g4
            dgl[:, j] = g5
        dst[...] = ds

    s, nch = cfg["s"], cfg["s"] // c
    return pl.pallas_call(
        body, name=name, grid=(nblk,), in_specs=ins + [st_spec, o_spec], out_specs=ins,
        out_shape=[jax.ShapeDtypeStruct((hd, s, dk), F32), jax.ShapeDtypeStruct((hd, nch, c, c), F32),
                   jax.ShapeDtypeStruct((hd, s, dv), F32), jax.ShapeDtypeStruct((hd, s, dk), F32),
                   jax.ShapeDtypeStruct((hd, s, dk), F32), jax.ShapeDtypeStruct((hd, nch, 1, LANES), F32)],
        scratch_shapes=[pltpu.VMEM((hd, dk, dv), F32)], compiler_params=_cp("arbitrary"),
    )(qd, aq, u, w, kd, gl, states, do)


def _loss_head(h, target, w, name):
    s, d = h.shape
    t = _tile(s, 512, 8)

    def body(h_r, t_r, w_r, loss_r, dh_r, dw_r):
        i = pl.program_id(0)
        (y,), vjp = jax.vjp(_rms_fn, h_r[...], w_r[...])
        err = y - t_r[...]
        part = 0.5 * jnp.sum(jnp.mean(err * err, axis=-1, keepdims=True))
        gh, gw = vjp((err * (1.0 / d),))
        dh_r[...] = gh

        @pl.when(i == 0)
        def _():
            loss_r[...] = jnp.zeros(loss_r.shape, F32) + part
            dw_r[...] = gw

        @pl.when(i > 0)
        def _():
            loss_r[...] += part
            dw_r[...] += gw

    row = pl.BlockSpec((t, d), lambda i: (i, 0))
    return pl.pallas_call(
        body, name=name, grid=(s // t,), in_specs=[row, row, pl.BlockSpec((1, d), lambda i: (0, 0))],
        out_specs=[pl.BlockSpec((8, LANES), lambda i: (0, 0)), row, pl.BlockSpec((1, d), lambda i: (0, 0))],
        out_shape=[jax.ShapeDtypeStruct((8, LANES), F32), jax.ShapeDtypeStruct((s, d), F32), jax.ShapeDtypeStruct((1, d), F32)],
        compiler_params=_cp("arbitrary"),
    )(h, target, w)


def _adamw_math(g, w, m, v):
    m = ADAM_B1 * m + (1.0 - ADAM_B1) * g
    v = ADAM_B2 * v + (1.0 - ADAM_B2) * jnp.square(g)
    m_hat = m / (1.0 - ADAM_B1 ** ADAM_STEP)
    v_hat = v / (1.0 - ADAM_B2 ** ADAM_STEP)
    delta = -ADAM_LR * (m_hat / (jnp.sqrt(v_hat) + ADAM_EPS) + ADAM_WD * w)
    return delta, m, v


def _pair_sum(mine, theirs, name):
    _, nch, nl, r, c = mine.shape
    tr = _tile(r, 128, 16)

    def body(a_r, b_r, o_r):
        o_r[...] = (a_r[...].astype(F32) + b_r[...].astype(F32)).astype(o_r.dtype)

    return pl.pallas_call(
        body, name=name, grid=(nch, nl, r // tr),
        in_specs=[pl.BlockSpec((None, None, None, tr, c), lambda p, a, i: (0, p, a, i, 0)),
                  pl.BlockSpec((None, None, tr, c), lambda p, a, i: (p, a, i, 0))],
        out_specs=pl.BlockSpec((None, None, tr, c), lambda p, a, i: (p, a, i, 0)),
        out_shape=jax.ShapeDtypeStruct(theirs.shape, theirs.dtype), compiler_params=_cp("parallel", "parallel", "parallel"),
    )(mine, theirs)


def _adamw_sum(parts, w, m, v, name):
    nl, r, c = w.shape
    n_parts = parts.shape[0]
    tr = _tile(r, 64, 8)

    def body(p_r, w_r, m_r, v_r, g_o, d_o, m_o, v_o):
        g = p_r[0].astype(F32)
        for q in range(1, n_parts):
            g = g + p_r[q].astype(F32)
        delta, mn, vn = _adamw_math(g, w_r[...], m_r[...], v_r[...])
        g_o[...] = g
        d_o[...] = delta
        m_o[...] = mn
        v_o[...] = vn

    blk = pl.BlockSpec((None, tr, c), lambda a, i: (a, i, 0))
    return pl.pallas_call(
        body, name=name, grid=(nl, r // tr),
        in_specs=[pl.BlockSpec((n_parts, None, tr, c), lambda a, i: (0, a, i, 0)), blk, blk, blk],
        out_specs=[blk] * 4, out_shape=[jax.ShapeDtypeStruct(w.shape, F32)] * 4, compiler_params=_cp("parallel", "parallel"),
    )(parts, w, m, v)


def _sum_parts(parts, name):
    _, r, c = parts.shape
    tr = _tile(r, 512, 8)

    def body(p_r, o_r):
        g = p_r[0]
        for q in range(1, N_DEV):
            g = g + p_r[q]
        o_r[...] = g

    return pl.pallas_call(
        body, name=name, grid=(r // tr,), in_specs=[pl.BlockSpec((N_DEV, tr, c), lambda i: (0, i, 0))],
        out_specs=pl.BlockSpec((tr, c), lambda i: (i, 0)), out_shape=jax.ShapeDtypeStruct((r, c), F32),
        compiler_params=_cp("parallel"),
    )(parts)


def _adamw_flat(g, w, m, v, name):
    r, c = w.shape
    tr = _tile(r, 512, 8)

    def body(g_r, w_r, m_r, v_r, d_o, m_o, v_o):
        delta, mn, vn = _adamw_math(g_r[...], w_r[...], m_r[...], v_r[...])
        d_o[...] = delta
        m_o[...] = mn
        v_o[...] = vn

    blk = pl.BlockSpec((tr, c), lambda i: (i, 0))
    return pl.pallas_call(
        body, name=name, grid=(r // tr,), in_specs=[blk] * 4, out_specs=[blk] * 3,
        out_shape=[jax.ShapeDtypeStruct(w.shape, F32)] * 3, compiler_params=_cp("parallel"),
    )(g, w, m, v)


def _remote(src, dst, send_sems, recv_sems, idx, to):
    return pltpu.make_async_remote_copy(src_ref=src, dst_ref=dst, send_sem=send_sems.at[idx], recv_sem=recv_sems.at[idx],
                                        device_id=to, device_id_type=MESH)


def _comm_call(body, name, arrays, out_shapes, n_sems):
    nt = len(arrays)
    return pl.pallas_call(
        body, name=name, in_specs=[pl.BlockSpec(memory_space=pl.ANY)] * nt, out_specs=[pl.BlockSpec(memory_space=pl.ANY)] * nt,
        out_shape=out_shapes,
        scratch_shapes=[pltpu.SemaphoreType.DMA((nt, n_sems)), pltpu.SemaphoreType.DMA((nt, n_sems)),
                        pltpu.SemaphoreType.DMA((nt,))],
    )(*arrays)


def _gather_all(arrays, name):
    nt = len(arrays)

    def body(*refs):
        srcs, dsts = refs[:nt], refs[nt:2 * nt]
        send_sems, recv_sems, local_sems = refs[2 * nt:]
        x, y, c = lax.axis_index("x"), lax.axis_index("y"), lax.axis_index("c")
        slot = lambda px, py, pc: 4 * px + 2 * py + pc
        me, sib = slot(x, y, c), (x, y, 1 - c)
        chips = [(1 - x, y), (x, 1 - y), (1 - x, 1 - y)]
        local = [pltpu.make_async_copy(srcs[t], dsts[t].at[me], local_sems.at[t]) for t in range(nt)]
        for cp in local:
            cp.start()
        sends = []
        for t in range(nt):
            sends.append(_remote(srcs[t], dsts[t].at[me], send_sems, recv_sems, (t, 0), sib))
            for j, (px, py) in enumerate(chips):
                sends.append(_remote(srcs[t], dsts[t].at[me], send_sems, recv_sems, (t, 1 + j), (px, py, c)))
        for cp in sends:
            cp.start()
        for j, (px, py) in enumerate(chips):
            landed = slot(px, py, c)
            for t in range(nt):
                _remote(srcs[t], dsts[t].at[landed], send_sems, recv_sems, (t, 1 + j), (px, py, c)).wait_recv()
                fwd = _remote(dsts[t].at[landed], dsts[t].at[landed], send_sems, recv_sems, (t, 4 + j), sib)
                fwd.start()
                sends.append(fwd)
        for t in range(nt):
            _remote(srcs[t], dsts[t].at[slot(x, y, 1 - c)], send_sems, recv_sems, (t, 0), sib).wait_recv()
            for j, (px, py) in enumerate(chips):
                _remote(srcs[t], dsts[t].at[slot(px, py, 1 - c)], send_sems, recv_sems, (t, 4 + j), sib).wait_recv()
        for cp in sends:
            cp.wait_send()
        for cp in local:
            cp.wait()

    return _comm_call(body, name, arrays, [jax.ShapeDtypeStruct((N_DEV,) + a.shape, a.dtype) for a in arrays], N_DEV - 1)


def _sibling_swap(arrays, name):
    nt = len(arrays)

    def body(*refs):
        srcs, dsts = refs[:nt], refs[nt:2 * nt]
        send_sems, recv_sems, _ = refs[2 * nt:]
        sib = (lax.axis_index("x"), lax.axis_index("y"), 1 - lax.axis_index("c"))
        copies = [_remote(srcs[t].at[1], dsts[t], send_sems, recv_sems, (t, 0), sib) for t in range(nt)]
        for cp in copies:
            cp.start()
        for cp in copies:
            cp.wait()

    return _comm_call(body, name, arrays, [jax.ShapeDtypeStruct(a.shape[1:], a.dtype) for a in arrays], 1)


def _chip_scatter(arrays, name):
    nt = len(arrays)

    def body(*refs):
        srcs, dsts = refs[:nt], refs[nt:2 * nt]
        send_sems, recv_sems, local_sems = refs[2 * nt:]
        x, y, c = lax.axis_index("x"), lax.axis_index("y"), lax.axis_index("c")
        mine = 2 * x + y
        local = [pltpu.make_async_copy(srcs[t].at[mine], dsts[t].at[mine], local_sems.at[t]) for t in range(nt)]
        for cp in local:
            cp.start()
        sends, arrivals = [], []
        for j, (px, py) in enumerate([(1 - x, y), (x, 1 - y), (1 - x, 1 - y)]):
            theirs = 2 * px + py
            for t in range(nt):
                sends.append(_remote(srcs[t].at[theirs], dsts[t].at[mine], send_sems, recv_sems, (t, j), (px, py, c)))
                arrivals.append(_remote(srcs[t].at[theirs], dsts[t].at[theirs], send_sems, recv_sems, (t, j), (px, py, c)))
        for cp in sends:
            cp.start()
        for cp in arrivals:
            cp.wait_recv()
        for cp in sends:
            cp.wait_send()
        for cp in local:
            cp.wait()

    return _comm_call(body, name, arrays, [jax.ShapeDtypeStruct(a.shape, a.dtype) for a in arrays], N_CHIP - 1)


def _config(x, ffn1_w_out, ssm_conv_b, ssm_dt_bias, ssm_norm, dn_conv_w, dn_dt_bias, dn_norm, dn_w_branch):
    s, d = x.shape[-2], x.shape[-1]
    f = ffn1_w_out.shape[1] * N_DEV
    cs, hs, inner = ssm_conv_b.shape[1], ssm_dt_bias.shape[1], ssm_norm.shape[1]
    gn = (cs - inner) // 2
    hd, dv = dn_dt_bias.shape[1], dn_norm.shape[1]
    cd = dn_conv_w.shape[2] * N_DEV
    vd = hd * dv
    kd = (cd - vd) // 2
    cfg = dict(s=s, d=d, f=f, cs=cs, hs=hs, inner=inner, gn=gn, n=gn // SSM_GROUPS, hg=hs // SSM_GROUPS, p=inner // hs,
               hp=inner // SSM_GROUPS, hd=hd, dv=dv, dk=kd // hd, cd=cd, vd=vd, kd=kd)
    offs, o = {}, 0
    for nm, wd in (("xbc", cs), ("qkv", cd), ("gates", 2 * d), ("zs", inner), ("zd", vd), ("small", SMALL_W)):
        offs[nm] = o
        o += wd
    cfg["offs"], cfg["pw"] = offs, o
    cfg["cw"] = 512
    cfg["pw_main"] = offs["small"]
    cfg["prep_rows"] = min(s, 8 * DN_CHUNK)
    cfg["scan_rows"] = min(s, 4 * DN_CHUNK)
    cfg["in_split"] = (inner, cs, hs, cd, vd, hd, hd, d, d)
    assert hs + 2 * hd <= SMALL_R and cfg["dk"] == dv and dv == LANES and LANES % cfg["p"] == 0 and cfg["hp"] % LANES == 0
    assert offs["qkv"] % cfg["cw"] == 0 and offs["gates"] % (2 * d) == 0 and offs["zs"] % cfg["hp"] == 0
    assert offs["zd"] % dv == 0 and all(wd % cfg["cw"] == 0 for wd in (inner, gn, kd, vd)) and inner % cfg["n"] == 0
    return cfg


def _permute_w_in(w, cfg):
    pts = [0]
    for wd in cfg["in_split"]:
        pts.append(pts[-1] + wd)
    z_s, xbc, dt, qkv, z_d, b_d, a_d, g_s, g_d = [w[:, pts[i]:pts[i + 1]] for i in range(9)]
    pad = jnp.zeros((w.shape[0], SMALL_W - dt.shape[1] - b_d.shape[1] - a_d.shape[1]), w.dtype)
    return jnp.concatenate([xbc, qkv, g_s, g_d, z_s, z_d, dt, b_d, a_d, pad], axis=1)


def _unpermute_w_in(g, cfg):
    o, d = cfg["offs"], cfg["d"]
    hs, hd = cfg["hs"], cfg["hd"]
    sm = g[:, o["small"]:]
    return jnp.concatenate([
        g[:, o["zs"]:o["zs"] + cfg["inner"]], g[:, o["xbc"]:o["xbc"] + cfg["cs"]], sm[:, :hs],
        g[:, o["qkv"]:o["qkv"] + cfg["cd"]], g[:, o["zd"]:o["zd"] + cfg["vd"]], sm[:, hs:hs + hd], sm[:, hs + hd:hs + 2 * hd],
        g[:, o["gates"]:o["gates"] + d], g[:, o["gates"] + d:o["gates"] + 2 * d]], axis=1)


def _lane_row(v, off):
    return jnp.pad(v.astype(F32), (off, SMALL_R - off - v.shape[0]))[None]


def _pack(arrs):
    flat = []
    for a in arrs:
        v = a.reshape(-1)
        flat.append(jnp.pad(v, (0, (-v.shape[0]) % LANES)))
    v = jnp.concatenate(flat)
    v = jnp.pad(v, (0, (-v.shape[0]) % (8 * LANES)))
    return v.reshape(-1, LANES)


def _unpack(packed, shapes):
    v, out, o = packed.reshape(-1), [], 0
    for sh in shapes:
        n = math.prod(sh)
        out.append(v[o:o + n].reshape(sh))
        o += n + (-n) % LANES
    return out


def _cols_gathered(g):
    nd, nl, r, c = g.shape
    return jnp.transpose(g, (1, 2, 0, 3)).reshape(nl, r, nd * c)


def _rows_gathered(g):
    nd, nl, r, c = g.shape
    return jnp.transpose(g, (1, 0, 2, 3)).reshape(nl, nd * r, c)


def _scatter_layout(g, cols, core):
    nl = g.shape[0]
    if cols:
        r, c = g.shape[1], g.shape[2] // N_DEV
        t = jnp.transpose(g.reshape(nl, r, N_CHIP, 2, c), (3, 2, 0, 1, 4))
    else:
        r, c = g.shape[1] // N_DEV, g.shape[2]
        t = jnp.transpose(g.reshape(nl, N_CHIP, 2, r, c), (2, 1, 0, 3, 4))
    return jnp.where(core == 0, t, t[::-1]).astype(BF16)


def _ffn_fwd(h, nw, w_in, w_out, cfg, tag):
    s, d, f = cfg["s"], cfg["d"], cfg["f"]
    t = _tile(s, 512, 8)
    xn, = _rw_fwd(_rms_fn, f"{tag}_norm", (1, s // t), [(h, _rows(t, d), None), (nw, _par(d), None)],
                  [(jax.ShapeDtypeStruct((s, d), BF16), _rows(t, d), None)])
    gu = _matmul(xn, w_in, mode="nn", name=f"{tag}_in", out_dtype=BF16)
    t2 = _tile(s, 256, 8)
    act, = _rw_fwd(_swiglu_fn, f"{tag}_act", (1, s // t2), [(gu, _rows(t2, 2 * f), [f, f])],
                   [(jax.ShapeDtypeStruct((s, f), BF16), _rows(t2, f), None)])
    h_out = _matmul(act, w_out, mode="nn", name=f"{tag}_out", res=h, scale=0.5)
    return h_out, dict(h=h, xn=xn, gu=gu, act=act)


def _ffn_bwd(gh, r, nw, w_in, w_out, cfg, tag):
    s, d, f = cfg["s"], cfg["d"], cfg["f"]
    d_wout = _matmul(r["act"], gh, mode="tn", name=f"{tag}_dwout", scale=0.5)
    d_act = _matmul(gh, w_out, mode="nt", name=f"{tag}_dact", scale=0.5, out_dtype=BF16)
    t2 = _tile(s, 256, 8)
    d_gu, = _rw_bwd(_swiglu_fn, f"{tag}_dgu", (1, s // t2), [(r["gu"], _rows(t2, 2 * f), [f, f])],
                    [(d_act, _rows(t2, f), None)], [(0, jax.ShapeDtypeStruct((s, 2 * f), BF16), _rows(t2, 2 * f), "tile")])
    d_win = _matmul(r["xn"], d_gu, mode="tn", name=f"{tag}_dwin")
    d_xn = _matmul(d_gu, w_in, mode="nt", name=f"{tag}_dxn", out_dtype=BF16)
    t = _tile(s, 512, 8)
    d_h, d_nw = _rw_bwd(_rms_fn, f"{tag}_dnorm", (1, s // t), [(r["h"], _rows(t, d), None), (nw, _par(d), None)],
                        [(d_xn, _rows(t, d), None)],
                        [(0, jax.ShapeDtypeStruct((s, d), F32), _rows(t, d), "tile"),
                         (1, jax.ShapeDtypeStruct((1, d), F32), _par(d), "acc_all")],
                        add=(gh, _rows(t, d), 0))
    return d_h, (d_nw, d_win, d_wout)


def _mix_fwd(h, p, cfg, tag):
    s, d, o = cfg["s"], cfg["d"], cfg["offs"]
    t = _tile(s, 512, 8)
    u, = _rw_fwd(_rms_fn, f"{tag}_norm", (1, s // t), [(h, _rows(t, d), None), (p["mix_norm"], _par(d), None)],
                 [(jax.ShapeDtypeStruct((s, d), BF16), _rows(t, d), None)])
    proj = _matmul(u, p["w_in"], mode="nn", name=f"{tag}_in", out_dtype=BF16)
    small = _matmul(u, p["w_small"], mode="nn", name=f"{tag}_insmall")
    cw, tr = cfg["cw"], _tile(s, 512, 8)
    xbc_c = _conv_fwd(proj, o["xbc"] // cw, p["ssm_conv_w"], p["ssm_conv_b"], name=f"{tag}_sconv", cw=cw, tr=tr)
    qkv_c = _conv_fwd(proj, o["qkv"] // cw, p["dn_conv_w"], jnp.zeros((1, cfg["cd"]), F32), name=f"{tag}_dconv", cw=cw, tr=tr)
    y, s_states = _ssd_fwd(xbc_c, small, p["ssm_dtb"], p["ssm_alog"], p["ssm_dsk"], cfg, f"{tag}_ssd")
    hp, inner, g = cfg["hp"], cfg["inner"], SSM_GROUPS
    t4 = _tile(s, 512, 8)
    zs_blk = o["zs"] // hp
    y_s, = _rw_fwd(_ssm_out_fn, f"{tag}_sout", (g, s // t4),
                   [(y, _rows(t4, hp, lambda j: j), None), (proj, _rows(t4, hp, lambda j: zs_blk + j), None),
                    (p["ssm_norm"], _par(hp, True), None)],
                   [(jax.ShapeDtypeStruct((s, inner), BF16), _rows(t4, hp, lambda j: j), None)])
    uu, ww, qd, kd, aq, gl = _gdn_prep_fwd(qkv_c, small, p["dn_alog"], p["dn_dtb"], cfg, f"{tag}_prep")
    o_dn, d_states = _gdn_scan_fwd(qd, aq, uu, ww, kd, gl, cfg, f"{tag}_scan")
    hd, dv = cfg["hd"], cfg["dv"]
    zd_blk = o["zd"] // dv
    o_spec = pl.BlockSpec((None, t4, dv), lambda j, i: (j, i, 0))
    y_d, = _rw_fwd(_dn_out_fn, f"{tag}_dout", (hd, s // t4),
                   [(o_dn, o_spec, None), (proj, _rows(t4, dv, lambda j: zd_blk + j), None), (p["dn_norm"], _par(dv), None)],
                   [(jax.ShapeDtypeStruct((s, cfg["vd"]), BF16), _rows(t4, dv, lambda j: j), None)])
    ps = _matmul(y_s, p["ssm_w_branch"], mode="nn", name=f"{tag}_sbr")
    pd = _matmul(y_d, p["dn_w_branch"], mode="nn", name=f"{tag}_dbr")
    t6 = _tile(s, 256, 8)
    merged, = _rw_fwd(_merge_fn, f"{tag}_merge", (1, s // t6),
                      [(proj, _rows(t6, 2 * d, o["gates"] // (2 * d)), [d, d]), (ps, _rows(t6, d), None), (pd, _rows(t6, d), None)],
                      [(jax.ShapeDtypeStruct((s, d), BF16), _rows(t6, d), None)])
    h_out = _matmul(merged, p["w_out"], mode="nn", name=f"{tag}_out", res=h)
    res = dict(h=h, u=u, proj=proj, small=small, xbc_c=xbc_c, qkv_c=qkv_c, y=y, s_states=s_states, y_s=y_s, uu=uu, ww=ww, qd=qd, kd=kd,
               aq=aq, gl=gl, o_dn=o_dn, d_states=d_states, y_d=y_d, ps=ps, pd=pd, merged=merged)
    return h_out, res


def _mix_bwd(gh, r, p, cfg, tag):
    s, d, o = cfg["s"], cfg["d"], cfg["offs"]
    cw, tr = cfg["cw"], _tile(s, 512, 8)
    hp, inner, g, hd, dv, dk = cfg["hp"], cfg["inner"], SSM_GROUPS, cfg["hd"], cfg["dv"], cfg["dk"]
    proj = r["proj"]
    grads = {}
    grads["w_out"] = _matmul(r["merged"], gh, mode="tn", name=f"{tag}_dwout")
    d_merged = _matmul(gh, p["w_out"], mode="nt", name=f"{tag}_dmerged", out_dtype=BF16)
    dproj = jax.ShapeDtypeStruct((s, cfg["pw_main"]), BF16)
    t6 = _tile(s, 256, 8)
    gates_spec = _rows(t6, 2 * d, o["gates"] // (2 * d))
    dproj, d_ps, d_pd = _rw_bwd(
        _merge_fn, f"{tag}_dmerge", (1, s // t6),
        [(proj, gates_spec, [d, d]), (r["ps"], _rows(t6, d), None), (r["pd"], _rows(t6, d), None)],
        [(d_merged, _rows(t6, d), None)],
        [(0, dproj, gates_spec, "tile"), (1, jax.ShapeDtypeStruct((s, d), BF16), _rows(t6, d), "tile"),
         (2, jax.ShapeDtypeStruct((s, d), BF16), _rows(t6, d), "tile")])
    grads["ssm_w_branch"] = _matmul(r["y_s"], d_ps, mode="tn", name=f"{tag}_dwsbr")
    grads["dn_w_branch"] = _matmul(r["y_d"], d_pd, mode="tn", name=f"{tag}_dwdbr")
    d_ys = _matmul(d_ps, p["ssm_w_branch"], mode="nt", name=f"{tag}_dys", out_dtype=BF16)
    d_yd = _matmul(d_pd, p["dn_w_branch"], mode="nt", name=f"{tag}_dyd", out_dtype=BF16)
    t4 = _tile(s, 512, 8)
    zs_blk, zd_blk = o["zs"] // hp, o["zd"] // dv
    zs_spec = _rows(t4, hp, lambda j: zs_blk + j)
    d_y, dproj, grads["ssm_norm"] = _rw_bwd(
        _ssm_out_fn, f"{tag}_dsout", (g, s // t4),
        [(r["y"], _rows(t4, hp, lambda j: j), None), (proj, zs_spec, None), (p["ssm_norm"], _par(hp, True), None)],
        [(d_ys, _rows(t4, hp, lambda j: j), None)],
        [(0, jax.ShapeDtypeStruct((s, inner), F32), _rows(t4, hp, lambda j: j), "tile"),
         (1, jax.ShapeDtypeStruct(dproj.shape, BF16), zs_spec, "tile"),
         (2, jax.ShapeDtypeStruct((1, inner), F32), _par(hp, True), "acc_row")],
        alias=(dproj, 1))
    o_spec = pl.BlockSpec((None, t4, dv), lambda j, i: (j, i, 0))
    zd_spec = _rows(t4, dv, lambda j: zd_blk + j)
    d_o, dproj, grads["dn_norm"] = _rw_bwd(
        _dn_out_fn, f"{tag}_ddout", (hd, s // t4),
        [(r["o_dn"], o_spec, None), (proj, zd_spec, None), (p["dn_norm"], _par(dv), None)],
        [(d_yd, _rows(t4, dv, lambda j: j), None)],
        [(0, jax.ShapeDtypeStruct((hd, s, dv), F32), o_spec, "tile"),
         (1, jax.ShapeDtypeStruct(dproj.shape, BF16), zd_spec, "tile"),
         (2, jax.ShapeDtypeStruct((1, dv), F32), _par(dv), "acc_all")],
        alias=(dproj, 1))
    d_xs, d_bm, d_cm, dsm_s, g_dtb, g_alog, g_dsk = _ssd_bwd(
        r["xbc_c"], r["small"], p["ssm_dtb"], p["ssm_alog"], p["ssm_dsk"], r["s_states"], d_y, cfg, f"{tag}_dssd")
    grads["ssm_dt_bias"], grads["ssm_a_log"], grads["ssm_d"] = (v[0, :cfg["hs"]] for v in (g_dtb, g_alog, g_dsk))
    dws, dbs, col = [], [], 0
    for nm, dy in (("xs", d_xs), ("bm", d_bm), ("cm", d_cm)):
        wd = dy.shape[1]
        dproj, dw_, db_ = _conv_bwd(proj, (o["xbc"] + col) // cw, p["ssm_conv_w"][:, col:col + wd],
                                    p["ssm_conv_b"][:, col:col + wd], dy, dproj, (o["xbc"] + col) // cw,
                                    name=f"{tag}_dsconv_{nm}", cw=cw, tr=tr)
        dws.append(dw_)
        dbs.append(db_)
        col += wd
    grads["ssm_conv_w"] = jnp.concatenate(dws, axis=1)
    grads["ssm_conv_b"] = jnp.concatenate(dbs, axis=1)[0]
    cts = _gdn_scan_bwd(r["qd"], r["aq"], r["uu"], r["ww"], r["kd"], r["gl"], r["d_states"], d_o, cfg, f"{tag}_dscan")
    d_qd, d_aq, d_uu, d_ww, d_kd, d_gl = cts
    d_q, d_k, d_v, dsm_d, g_alog_d, g_dtb_d = _gdn_prep_bwd(
        r["qkv_c"], r["small"], p["dn_alog"], p["dn_dtb"], (d_uu, d_ww, d_qd, d_kd, d_aq, d_gl), cfg, f"{tag}_dprep")
    a0 = cfg["hs"] + hd
    grads["dn_a_log"], grads["dn_dt_bias"] = g_alog_d[0, a0:a0 + hd], g_dtb_d[0, a0:a0 + hd]
    dws, col = [], 0
    zero_b = jnp.zeros((1, cfg["cd"]), F32)
    for nm, dy in (("q", d_q), ("k", d_k), ("v", d_v)):
        wd = dy.shape[1]
        dproj, dw_, _ = _conv_bwd(proj, (o["qkv"] + col) // cw, p["dn_conv_w"][:, col:col + wd], zero_b[:, col:col + wd], dy,
                                  dproj, (o["qkv"] + col) // cw, name=f"{tag}_ddconv_{nm}", cw=cw, tr=tr)
        dws.append(dw_)
        col += wd
    grads["dn_conv_w"] = jnp.concatenate(dws, axis=1)
    d_small = _dsmall(dsm_s, dsm_d, f"{tag}_dsmall")
    grads["w_in"] = jnp.concatenate([_matmul(r["u"], dproj, mode="tn", name=f"{tag}_dwin"),
                                     _matmul(r["u"], d_small, mode="tn", name=f"{tag}_dwinsmall")], axis=1)
    d_u = _matmul(d_small, p["w_small"], mode="nt", name=f"{tag}_dusmall")
    d_u = _matmul(dproj, p["w_in"], mode="nt", name=f"{tag}_du", res=d_u, out_dtype=BF16)
    t = _tile(s, 512, 8)
    d_h, grads["mix_norm"] = _rw_bwd(
        _rms_fn, f"{tag}_dnorm", (1, s // t), [(r["h"], _rows(t, d), None), (p["mix_norm"], _par(d), None)],
        [(d_u, _rows(t, d), None)],
        [(0, jax.ShapeDtypeStruct((s, d), F32), _rows(t, d), "tile"), (1, jax.ShapeDtypeStruct((1, d), F32), _par(d), "acc_all")],
        add=(gh, _rows(t, d), 0))
    return d_h, grads


_BIG = ("ffn1_w_in", "ffn1_w_out", "w_in", "ssm_w_branch", "dn_w_branch", "w_out", "ffn2_w_in", "ffn2_w_out")
_COL_SHARDED = ("ffn1_w_in", "w_in", "ffn2_w_in")
_CONV = ("ssm_conv_w", "dn_conv_w")
_NAMES = ("ffn1_norm", "ffn1_w_in", "ffn1_w_out", "mix_norm", "w_in", "ssm_conv_w", "ssm_conv_b", "ssm_dt_bias", "ssm_a_log",
          "ssm_d", "ssm_norm", "ssm_w_branch", "dn_conv_w", "dn_dt_bias", "dn_a_log", "dn_norm", "dn_w_branch", "w_out",
          "ffn2_norm", "ffn2_w_in", "ffn2_w_out", "final_norm")


def kernel(x, ffn1_norm, ffn1_w_in, ffn1_w_out, mix_norm, w_in, ssm_conv_w, ssm_conv_b, ssm_dt_bias, ssm_a_log, ssm_d, ssm_norm, ssm_w_branch, dn_conv_w, dn_dt_bias, dn_a_log, dn_norm, dn_w_branch, w_out, ffn2_norm, ffn2_w_in, ffn2_w_out, final_norm, loss_target, m_ffn1_norm, m_ffn1_w_in, m_ffn1_w_out, m_mix_norm, m_w_in, m_ssm_conv_w, m_ssm_conv_b, m_ssm_dt_bias, m_ssm_a_log, m_ssm_d, m_ssm_norm, m_ssm_w_branch, m_dn_conv_w, m_dn_dt_bias, m_dn_a_log, m_dn_norm, m_dn_w_branch, m_w_out, m_ffn2_norm, m_ffn2_w_in, m_ffn2_w_out, m_final_norm, v_ffn1_norm, v_ffn1_w_in, v_ffn1_w_out, v_mix_norm, v_w_in, v_ssm_conv_w, v_ssm_conv_b, v_ssm_dt_bias, v_ssm_a_log, v_ssm_d, v_ssm_norm, v_ssm_w_branch, v_dn_conv_w, v_dn_dt_bias, v_dn_a_log, v_dn_norm, v_dn_w_branch, v_w_out, v_ffn2_norm, v_ffn2_w_in, v_ffn2_w_out, v_final_norm):
    args = locals()
    w = {n: args[n] for n in _NAMES}
    mom = {n: args["m_" + n] for n in _NAMES}
    var = {n: args["v_" + n] for n in _NAMES}
    cfg = _config(x, ffn1_w_out, ssm_conv_b, ssm_dt_bias, ssm_norm, dn_conv_w, dn_dt_bias, dn_norm, dn_w_branch)
    depth, s, d = ffn1_norm.shape[0], cfg["s"], cfg["d"]
    me = 4 * lax.axis_index("x") + 2 * lax.axis_index("y") + lax.axis_index("c")

    gathered = _gather_all([w[n].astype(BF16) for n in _BIG] + [w[n] for n in _CONV], "gather_weights")
    full = {}
    for n, g in zip(_BIG + _CONV, gathered):
        full[n] = _cols_gathered(g) if (n in _COL_SHARDED or n in _CONV) else _rows_gathered(g)

    hs, hd = cfg["hs"], cfg["hd"]
    layers = []
    for l in range(depth):
        w_perm = _permute_w_in(full["w_in"][l], cfg)
        layers.append(dict(
            ffn1_norm=ffn1_norm[l][None], ffn1_w_in=full["ffn1_w_in"][l], ffn1_w_out=full["ffn1_w_out"][l],
            mix_norm=mix_norm[l][None], w_in=w_perm[:, :cfg["pw_main"]], w_small=w_perm[:, cfg["pw_main"]:],
            ssm_conv_w=full["ssm_conv_w"][l], ssm_conv_b=ssm_conv_b[l][None],
            ssm_dtb=_lane_row(ssm_dt_bias[l], 0), ssm_alog=_lane_row(ssm_a_log[l], 0), ssm_dsk=_lane_row(ssm_d[l], 0),
            ssm_norm=ssm_norm[l][None], ssm_w_branch=full["ssm_w_branch"][l], dn_conv_w=full["dn_conv_w"][l],
            dn_dtb=_lane_row(dn_dt_bias[l], hs + hd), dn_alog=_lane_row(dn_a_log[l], hs + hd), dn_norm=dn_norm[l][None],
            dn_w_branch=full["dn_w_branch"][l], w_out=full["w_out"][l],
            ffn2_norm=ffn2_norm[l][None], ffn2_w_in=full["ffn2_w_in"][l], ffn2_w_out=full["ffn2_w_out"][l]))

    h = x.reshape(s, d)
    saved = []
    for l, p in enumerate(layers):
        h, r1 = _ffn_fwd(h, p["ffn1_norm"], p["ffn1_w_in"], p["ffn1_w_out"], cfg, f"l{l}_ffn1")
        h, rm = _mix_fwd(h, p, cfg, f"l{l}_mix")
        h, r2 = _ffn_fwd(h, p["ffn2_norm"], p["ffn2_w_in"], p["ffn2_w_out"], cfg, f"l{l}_ffn2")
        saved.append((r1, rm, r2))
    loss_blk, gh, g_final = _loss_head(h, loss_target.reshape(s, d), final_norm[None], "loss_head")
    loss = lax.psum(loss_blk[0, 0], ("x", "y", "c"))

    lg = [None] * depth
    for l in reversed(range(depth)):
        p, (r1, rm, r2) = layers[l], saved[l]
        gh, (g_n2, g_win2, g_wout2) = _ffn_bwd(gh, r2, p["ffn2_norm"], p["ffn2_w_in"], p["ffn2_w_out"], cfg, f"l{l}_ffn2")
        gh, gm = _mix_bwd(gh, rm, p, cfg, f"l{l}_mix")
        gh, (g_n1, g_win1, g_wout1) = _ffn_bwd(gh, r1, p["ffn1_norm"], p["ffn1_w_in"], p["ffn1_w_out"], cfg, f"l{l}_ffn1")
        gm["w_in"] = _unpermute_w_in(gm["w_in"], cfg)
        gm.update(ffn1_norm=g_n1[0], ffn1_w_in=g_win1, ffn1_w_out=g_wout1, ffn2_norm=g_n2[0], ffn2_w_in=g_win2,
                  ffn2_w_out=g_wout2, mix_norm=gm["mix_norm"][0], ssm_norm=gm["ssm_norm"][0], dn_norm=gm["dn_norm"][0])
        lg[l] = gm
    grad_x = gh.reshape(x.shape)
    local = {n: jnp.stack([lg[l][n] for l in range(depth)]) for n in _NAMES if n != "final_norm"}
    local["final_norm"] = g_final[0]

    shares = [_scatter_layout(local[n], n in _COL_SHARDED, lax.axis_index("c")) for n in _BIG]
    from_sibling = _sibling_swap(shares, "swap_grads")
    chip_sums = [_pair_sum(a, b, f"pair_sum_{n}") for n, a, b in zip(_BIG, shares, from_sibling)]
    parts = _chip_scatter(chip_sums, "scatter_grads")
    out_g, out_d, out_m, out_v = {}, {}, {}, {}
    for n, pt in zip(_BIG, parts):
        out_g[n], out_d[n], out_m[n], out_v[n] = _adamw_sum(pt, w[n], mom[n], var[n], f"adamw_{n}")

    small = [n for n in _NAMES if n not in _BIG]
    packed, = _gather_all([_pack([local[n] for n in small])], "gather_small_grads")
    total = _unpack(_sum_parts(packed, "sum_small_grads"), [local[n].shape for n in small])
    for n, g in zip(small, total):
        if n in _CONV:
            c = w[n].shape[2]
            g = lax.dynamic_slice_in_dim(g, me * c, c, axis=2)
        out_g[n] = g
    shapes = [w[n].shape for n in small]
    upd = _adamw_flat(_pack([out_g[n] for n in small]), _pack([w[n] for n in small]), _pack([mom[n] for n in small]),
                      _pack([var[n] for n in small]), "adamw_small")
    for dst, pk in zip((out_d, out_m, out_v), upd):
        for n, a in zip(small, _unpack(pk, shapes)):
            dst[n] = a

    return (loss, grad_x, *[out_g[n] for n in _NAMES], *[out_d[n] for n in _NAMES], *[out_m[n] for n in _NAMES],
            *[out_v[n] for n in _NAMES])
```

```python
import functools
import math

import jax
import jax.numpy as jnp
from jax import lax
from jax.experimental import pallas as pl
from jax.experimental.pallas import tpu as pltpu

F32, BF16 = jnp.float32, jnp.bfloat16
MESH = pl.DeviceIdType.MESH

N_DEV = 8
N_CHIP = 4
EPS = 1e-6
CONV_K = 4
SSM_GROUPS = 4
SSM_CHUNK = 128
DN_CHUNK = 64
ADAM_LR, ADAM_B1, ADAM_B2, ADAM_EPS, ADAM_WD, ADAM_STEP = 0.001, 0.9, 0.999, 1e-08, 0.01, 10

V7X_VMEM_BYTES = 64 * 1024 * 1024
VMEM_LIMIT = 52 * 1024 * 1024
MATMUL_VMEM_BUDGET = 44 * 1024 * 1024
MATMUL_PLANS = ((False, 512, 5632), (True, 1024, 512), (True, 512, 1408), (True, 1024, 1408))
LANES = 128
SMALL_W = 256
SMALL_R = 128
CONV_HALO = 8
HALO_BLK = 16
CONV_STRIP = 32


def _tile(n, target, quantum):
    if n <= target:
        return n
    t = (target // quantum) * quantum
    while t >= quantum:
        if n % t == 0:
            return t
        t -= quantum
    return n


def _cp(*sem):
    return pltpu.CompilerParams(dimension_semantics=sem, vmem_limit_bytes=VMEM_LIMIT)


def _softplus(x):
    return jnp.maximum(x, 0.0) + jnp.log1p(jnp.exp(-jnp.abs(x)))


def _silu(x):
    return x * jax.nn.sigmoid(x)


def _bdot(a, b, dims):
    return lax.dot_general(a.astype(BF16), b.astype(BF16), dims, preferred_element_type=F32)


_NN = (((1,), (0,)), ((), ()))
_NT = (((1,), (1,)), ((), ()))
_TN = (((0,), (0,)), ((), ()))
_MM_DIMS = {"nn": _NN, "nt": _NT, "tn": _TN}


def _mm3(spec, a, b):
    ah, bh = a.astype(BF16), b.astype(BF16)
    al, bl = (a - ah.astype(F32)).astype(BF16), (b - bh.astype(F32)).astype(BF16)
    e = lambda x, y: jnp.einsum(spec, x, y, preferred_element_type=F32)
    return e(ah, bh) + (e(ah, bl) + e(al, bh))


@jax.custom_vjp
def _pmm(a, b):
    return _mm3("bij,bjk->bik", a, b)


def _pmm_fwd(a, b):
    return _pmm(a, b), (a, b)


def _pmm_bwd(res, g):
    a, b = res
    return _mm3("bik,bjk->bij", g, b), _mm3("bji,bjk->bik", a, g)


_pmm.defvjp(_pmm_fwd, _pmm_bwd)


@jax.custom_vjp
def _neumann_inv(n):
    c = n.shape[-1]
    r = lax.broadcasted_iota(jnp.int32, (c, c), 0)
    cc = lax.broadcasted_iota(jnp.int32, (c, c), 1)
    t, pw = (r == cc).astype(F32)[None] + n, n
    for _ in range(int(math.log2(c)) - 1):
        pw = _mm3("bij,bjk->bik", pw, pw)
        t = t + _mm3("bij,bjk->bik", t, pw)
    return t


def _neumann_fwd(n):
    t = _neumann_inv(n)
    return t, t


def _neumann_bwd(t, g):
    return (_mm3("bik,bjk->bij", _mm3("bji,bjk->bik", t, g), t),)


_neumann_inv.defvjp(_neumann_fwd, _neumann_bwd)


def _matmul(a, b, *, mode, name, out_dtype=F32, res=None, scale=1.0):
    if mode == "nn":
        (m, k), (k2, n) = a.shape, b.shape
    elif mode == "nt":
        (m, k), (n, k2) = a.shape, b.shape
    else:
        (k, m), (k2, n) = a.shape, b.shape
    assert k == k2, (name, a.shape, b.shape)
    has_res = res is not None
    rows_first, rows_t, cols_t = MATMUL_PLANS[int(name[1]) % len(MATMUL_PLANS)] if mode != "tn" else (True, 1024, 1408)
    tn = _tile(n, cols_t, LANES)
    shapes = ((512, 4096), (1024, 2048), (1024, 1024), (512, 512)) if mode == "tn" else (
        (rows_t, 2816), (rows_t, 2048), (rows_t, 1408), (rows_t, 1024), (512, 512))
    for rows, depth in shapes:
        tm, tk = _tile(m, rows, LANES), _tile(k, depth, LANES)
        need = 2 * tk * (tm * a.dtype.itemsize + tn * b.dtype.itemsize) + tm * tn * (4 + 2 * jnp.dtype(out_dtype).itemsize)
        need += 2 * tm * tn * res.dtype.itemsize if has_res else 0
        if need <= MATMUL_VMEM_BUDGET:
            break
    nk = k // tk
    dims = _MM_DIMS[mode]
    ij = (lambda g0, g1: (g0, g1)) if rows_first else (lambda g0, g1: (g1, g0))
    a_map = (lambda g0, g1, q: (q, ij(g0, g1)[0])) if mode == "tn" else (lambda g0, g1, q: (ij(g0, g1)[0], q))
    b_map = (lambda g0, g1, q: (ij(g0, g1)[1], q)) if mode == "nt" else (lambda g0, g1, q: (q, ij(g0, g1)[1]))
    a_spec = pl.BlockSpec((tk, tm) if mode == "tn" else (tm, tk), a_map)
    b_spec = pl.BlockSpec((tn, tk) if mode == "nt" else (tk, tn), b_map)
    o_spec = pl.BlockSpec((tm, tn), lambda g0, g1, q: ij(g0, g1))

    def body(*refs):
        a_ref, b_ref = refs[0], refs[1]
        res_ref = refs[2] if has_res else None
        o_ref = refs[3 if has_res else 2]
        acc_ref = refs[-1] if nk > 1 else None

        def finish(acc):
            val = acc * scale if scale != 1.0 else acc
            if has_res:
                val = res_ref[...].astype(F32) + val
            o_ref[...] = val.astype(o_ref.dtype)

        if nk == 1:
            finish(_bdot(a_ref[...], b_ref[...], dims))
        else:
            q = pl.program_id(2)

            @pl.when(q == 0)
            def _():
                acc_ref[...] = jnp.zeros(acc_ref.shape, F32)

            acc_ref[...] += _bdot(a_ref[...], b_ref[...], dims)

            @pl.when(q == nk - 1)
            def _():
                finish(acc_ref[...])

    ins = [a, b] + ([res] if has_res else [])
    in_specs = [a_spec, b_spec] + ([o_spec] if has_res else [])
    return pl.pallas_call(
        body, name=name, grid=(m // tm, n // tn, nk) if rows_first else (n // tn, m // tm, nk), in_specs=in_specs,
        out_specs=o_spec,
        out_shape=jax.ShapeDtypeStruct((m, n), out_dtype),
        scratch_shapes=[pltpu.VMEM((tm, tn), F32)] if nk > 1 else [],
        compiler_params=_cp("parallel", "parallel", "arbitrary"),
    )(*ins)


def _read(ref, split, rows):
    rows = slice(None) if ref.shape[0] == 1 else rows
    if split is None:
        return [ref[rows, :].astype(F32)]
    out, off = [], 0
    for w in split:
        out.append(ref[rows, off:off + w].astype(F32))
        off += w
    return out


def _write(ref, vals, split, rows):
    if split is None:
        ref[rows, :] = vals[0].astype(ref.dtype)
        return
    off = 0
    for w, v in zip(split, vals):
        ref[rows, off:off + w] = v.astype(ref.dtype)
        off += w


def _rw_fwd(fn, name, grid, ins, outs):
    n_in = len(ins)

    def body(*refs):
        rows = slice(None)
        args = []
        for r, (_, _, split) in zip(refs[:n_in], ins):
            args += _read(r, split, rows)
        vals = list(fn(*args))
        for r, (_, _, split) in zip(refs[n_in:], outs):
            n = 1 if split is None else len(split)
            _write(r, vals[:n], split, rows)
            vals = vals[n:]

    return pl.pallas_call(
        body, name=name, grid=grid, in_specs=[s for _, s, _ in ins], out_specs=[s for _, s, _ in outs],
        out_shape=[o for o, _, _ in outs], compiler_params=_cp("parallel", "parallel"),
    )(*[a for a, _, _ in ins])


def _rw_bwd(fn, name, grid, ins, cts, grads, add=None, alias=None):
    n_in, n_ct = len(ins), len(cts)
    n_fixed = n_in + n_ct + (1 if add is not None else 0) + (1 if alias is not None else 0)
    arg_pos, pos = [], 0
    for _, _, split in ins:
        n = 1 if split is None else len(split)
        arg_pos.append((pos, n))
        pos += n

    def body(*refs):
        out_refs = refs[n_fixed:]
        rows = slice(None)
        wrt = []
        for idx, _, _, _ in grads:
            p, n = arg_pos[idx]
            wrt += list(range(p, p + n))
        args = []
        for r, (_, _, split) in zip(refs[:n_in], ins):
            args += _read(r, split, rows)
        ct_vals = []
        for r, (_, _, split) in zip(refs[n_in:n_in + n_ct], cts):
            ct_vals += _read(r, split, rows)

        def f(*w):
            full = list(args)
            for p, v in zip(wrt, w):
                full[p] = v
            return tuple(fn(*full))

        _, vjp = jax.vjp(f, *[args[p] for p in wrt])
        g = list(vjp(tuple(ct_vals)))
        first_row = pl.program_id(1) == 0
        first_all = jnp.logical_and(pl.program_id(0) == 0, first_row)
        for gi, (idx, _, _, mode) in enumerate(grads):
            n = arg_pos[idx][1]
            vals, g = g[:n], g[n:]
            o = out_refs[gi]
            if mode == "tile":
                if add is not None and add[2] == gi:
                    vals = [vals[0] + refs[n_in + n_ct][...].astype(F32)]
                _write(o, vals, ins[idx][2], rows)
            else:
                first = first_row if mode == "acc_row" else first_all

                @pl.when(first)
                def _(o=o, val=vals[0]):
                    o[...] = val

                @pl.when(jnp.logical_not(first))
                def _(o=o, val=vals[0]):
                    o[...] += val

    arrays = [a for a, _, _ in ins] + [a for a, _, _ in cts]
    in_specs = [s for _, s, _ in ins] + [s for _, s, _ in cts]
    if add is not None:
        arrays.append(add[0])
        in_specs.append(add[1])
    aliases = {}
    if alias is not None:
        aliases = {len(arrays): alias[1]}
        arrays.append(alias[0])
        in_specs.append(pl.BlockSpec(memory_space=pl.ANY))
    return pl.pallas_call(
        body, name=name, grid=grid, in_specs=in_specs, out_specs=[s for _, _, s, _ in grads],
        out_shape=[o for _, o, _, _ in grads], input_output_aliases=aliases,
        compiler_params=_cp("arbitrary", "arbitrary"),
    )(*arrays)


def _rows(t, w, col=0):
    if callable(col):
        return pl.BlockSpec((t, w), lambda j, i: (i, col(j)))
    return pl.BlockSpec((t, w), lambda j, i: (i, col))


def _par(w, per_col=False):
    return pl.BlockSpec((1, w), (lambda j, i: (0, j)) if per_col else (lambda j, i: (0, 0)))


def _rms_fn(x, w):
    return (x * lax.rsqrt(jnp.mean(x * x, axis=-1, keepdims=True) + EPS) * w,)


def _swiglu_fn(gate, up):
    return (_silu(gate) * up,)


def _ssm_out_fn(y, z, w):
    yg = y * _silu(z)
    return (yg * lax.rsqrt(jnp.mean(yg * yg, axis=-1, keepdims=True) + EPS) * w,)


def _dn_out_fn(o, z, w):
    return (o * lax.rsqrt(jnp.mean(o * o, axis=-1, keepdims=True) + EPS) * w * _silu(z),)


def _merge_fn(gs, gd, ps, pd):
    return (jax.nn.sigmoid(gs) * ps + jax.nn.sigmoid(gd) * pd,)


def _dsmall(a, b, name):
    s = a.shape[0]
    t = _tile(s, 1024, 16)

    def body(a_r, b_r, o_r):
        o_r[:, :SMALL_R] = (a_r[...] + b_r[...]).astype(o_r.dtype)
        o_r[:, SMALL_R:] = jnp.zeros((t, SMALL_W - SMALL_R), o_r.dtype)

    row = pl.BlockSpec((t, SMALL_R), lambda i: (i, 0))
    return pl.pallas_call(
        body, name=name, grid=(s // t,), in_specs=[row, row], out_specs=pl.BlockSpec((t, SMALL_W), lambda i: (i, 0)),
        out_shape=jax.ShapeDtypeStruct((s, SMALL_W), BF16), compiler_params=_cp("parallel"),
    )(a, b)


def _conv_fwd(x, x_col0, w, b, *, name, cw, tr):
    s, c = x.shape[0], w.shape[1]
    nr, ncol, hb = s // tr, c // cw, tr // HALO_BLK
    rs, lo = CONV_STRIP, CONV_HALO - (CONV_K - 1)

    def body(x_ref, prev_ref, w_ref, b_ref, o_ref, buf):
        i = pl.program_id(1)
        buf[0:CONV_HALO, :] = jnp.where(i > 0, prev_ref[HALO_BLK - CONV_HALO:, :].astype(F32), 0.0)
        buf[CONV_HALO:, :] = x_ref[...].astype(F32)
        taps = [w_ref[q:q + 1, :] for q in range(CONV_K)]
        bias = b_ref[...]

        def strip(k, carry):
            r0 = pl.multiple_of(k * rs, rs)
            ext = buf[pl.ds(r0, rs + CONV_HALO), :]
            acc = bias + taps[0] * ext[lo:lo + rs]
            for q in range(1, CONV_K):
                acc = acc + taps[q] * ext[lo + q:lo + q + rs]
            o_ref[pl.ds(r0, rs), :] = _silu(acc).astype(o_ref.dtype)
            return carry

        lax.fori_loop(0, tr // rs, strip, 0)

    return pl.pallas_call(
        body, name=name, grid=(ncol, nr),
        in_specs=[pl.BlockSpec((tr, cw), lambda j, i: (i, x_col0 + j)),
                  pl.BlockSpec((HALO_BLK, cw), lambda j, i: (jnp.maximum(i * hb - 1, 0), x_col0 + j)),
                  pl.BlockSpec((CONV_K, cw), lambda j, i: (0, j)), pl.BlockSpec((1, cw), lambda j, i: (0, j))],
        out_specs=pl.BlockSpec((tr, cw), lambda j, i: (i, j)),
        out_shape=jax.ShapeDtypeStruct((s, c), BF16),
        scratch_shapes=[pltpu.VMEM((CONV_HALO + tr, cw), F32)],
        compiler_params=_cp("parallel", "parallel"),
    )(x, x, w, b)


def _conv_bwd(x, x_col0, w, b, dy, dproj, out_col0, *, name, cw, tr):
    s, c = dy.shape
    nr, ncol = s // tr, c // cw
    hb, last_hb = tr // HALO_BLK, s // HALO_BLK - 1
    hb8, last_hb8 = tr // CONV_HALO, s // CONV_HALO - 1
    ext = tr + CONV_HALO
    rs, lo = CONV_STRIP, CONV_HALO - (CONV_K - 1)
    fresh = isinstance(dproj, jax.ShapeDtypeStruct)

    def body(x_ref, prev_ref, next_ref, dy_ref, dyn_ref, w_ref, b_ref, *rest):
        dx_ref, dw_ref, db_ref, xbuf, gbuf = rest[-5:]
        i = pl.program_id(1)
        xbuf[0:CONV_HALO, :] = jnp.where(i > 0, prev_ref[HALO_BLK - CONV_HALO:, :].astype(F32), 0.0)
        xbuf[CONV_HALO:CONV_HALO + tr, :] = x_ref[...].astype(F32)
        xbuf[CONV_HALO + tr:, :] = next_ref[0:CONV_HALO, :].astype(F32)
        taps = [w_ref[q:q + 1, :] for q in range(CONV_K)]
        bias = b_ref[...]

        def dpre(xe, dy, n):
            pre = bias + taps[0] * xe[lo:lo + n]
            for q in range(1, CONV_K):
                pre = pre + taps[q] * xe[lo + q:lo + q + n]
            sg = jax.nn.sigmoid(pre)
            return dy * (sg * (1.0 + pre * (1.0 - sg)))

        def strip1(k, carry):
            r0 = pl.multiple_of(k * rs, rs)
            gbuf[pl.ds(r0, rs), :] = dpre(xbuf[pl.ds(r0, rs + CONV_HALO), :], dy_ref[pl.ds(r0, rs), :].astype(F32), rs)
            return carry

        lax.fori_loop(0, tr // rs, strip1, 0)
        gbuf[tr:, :] = dpre(xbuf[tr:, :], jnp.where(i < nr - 1, dyn_ref[...].astype(F32), 0.0), CONV_HALO)

        def fold(v):
            acc = v[0:8]
            for a in range(1, rs // 8):
                acc = acc + v[8 * a:8 * a + 8]
            return acc

        def strip2(k, carry):
            r0 = pl.multiple_of(k * rs, rs)
            ge = gbuf[pl.ds(r0, rs + CONV_HALO), :]
            x_own = xbuf[pl.ds(r0 + CONV_HALO, rs), :]
            dx = jnp.zeros((rs, cw), F32)
            new = []
            for q in range(CONV_K):
                g_q = ge[CONV_K - 1 - q:CONV_K - 1 - q + rs]
                dx = dx + taps[q] * g_q
                new.append(carry[q] + fold(x_own * g_q))
            dx_ref[pl.ds(r0, rs), :] = dx.astype(dx_ref.dtype)
            return tuple(new) + (carry[CONV_K] + fold(ge[0:rs]),)

        sums = lax.fori_loop(0, tr // rs, strip2, tuple(jnp.zeros((8, cw), F32) for _ in range(CONV_K + 1)))
        dws = [jnp.sum(sums[q], axis=0, keepdims=True) for q in range(CONV_K)]
        dbv = jnp.sum(sums[CONV_K], axis=0, keepdims=True)

        @pl.when(i == 0)
        def _():
            for q in range(CONV_K):
                dw_ref[q:q + 1, :] = dws[q]
            db_ref[...] = dbv

        @pl.when(i > 0)
        def _():
            for q in range(CONV_K):
                dw_ref[q:q + 1, :] += dws[q]
            db_ref[...] += dbv

    xmap = lambda j, i: (i, x_col0 + j)
    ins = [x, x, x, dy, dy, w, b]
    in_specs = [pl.BlockSpec((tr, cw), xmap),
                pl.BlockSpec((HALO_BLK, cw), lambda j, i: (jnp.maximum(i * hb - 1, 0), x_col0 + j)),
                pl.BlockSpec((HALO_BLK, cw), lambda j, i: (jnp.minimum((i + 1) * hb, last_hb), x_col0 + j)),
                pl.BlockSpec((tr, cw), lambda j, i: (i, j)),
                pl.BlockSpec((CONV_HALO, cw), lambda j, i: (jnp.minimum((i + 1) * hb8, last_hb8), j)),
                pl.BlockSpec((CONV_K, cw), lambda j, i: (0, j)), pl.BlockSpec((1, cw), lambda j, i: (0, j))]
    aliases = {}
    if not fresh:
        aliases = {len(ins): 0}
        ins.append(dproj)
        in_specs.append(pl.BlockSpec(memory_space=pl.ANY))
    return pl.pallas_call(
        body, name=name, grid=(ncol, nr), in_specs=in_specs,
        out_specs=[pl.BlockSpec((tr, cw), lambda j, i: (i, out_col0 + j)),
                   pl.BlockSpec((CONV_K, cw), lambda j, i: (0, j)), pl.BlockSpec((1, cw), lambda j, i: (0, j))],
        out_shape=[jax.ShapeDtypeStruct(dproj.shape, dproj.dtype), jax.ShapeDtypeStruct((CONV_K, c), F32),
                   jax.ShapeDtypeStruct((1, c), F32)],
        scratch_shapes=[pltpu.VMEM((CONV_HALO + ext, cw), F32), pltpu.VMEM((ext, cw), F32)],
        input_output_aliases=aliases, compiler_params=_cp("arbitrary", "arbitrary"),
    )(*ins)


def _ssd_gates(small, dtb, alog):
    l = small.shape[0]
    r = lax.broadcasted_iota(jnp.int32, (l, l), 0)
    c = lax.broadcasted_iota(jnp.int32, (l, l), 1)
    dt_all = _softplus(small + dtb)
    return dt_all, _pmm((r >= c).astype(F32)[None], (dt_all * (-jnp.exp(alog)))[None])[0]


def _ssd_chunk(xs, bm, cm, dt_all, acum_all, dsk, st, g, *, hg, p):
    l, w = dt_all.shape
    per = LANES // p
    lane_w = lax.broadcasted_iota(jnp.int32, (1, w), 1)
    lane = lax.broadcasted_iota(jnp.int32, (1, LANES), 1)
    r = lax.broadcasted_iota(jnp.int32, (l, l), 0)
    c = lax.broadcasted_iota(jnp.int32, (l, l), 1)
    tri = r >= c
    eye = (r == c).astype(F32)
    last = (lax.broadcasted_iota(jnp.int32, (l, 1), 0) == l - 1).astype(F32)
    cb = _bdot(cm, bm, _NT)
    ys, sts = [], []
    for q, (x_q, st_q) in enumerate(zip(xs, st)):
        dt_e = jnp.zeros((l, LANES), F32)
        acum_e = jnp.zeros((l, LANES), F32)
        d_e = jnp.zeros((1, LANES), F32)
        decays, masks = [], []
        for jj in range(per):
            mh = (lane_w == g * hg + q * per + jj).astype(F32)
            mj = jnp.logical_and(lane >= jj * p, lane < (jj + 1) * p).astype(F32)
            ac = jnp.sum(acum_all * mh, axis=1, keepdims=True)
            dt_e = dt_e + jnp.sum(dt_all * mh, axis=1, keepdims=True) * mj
            acum_e = acum_e + ac * mj
            d_e = d_e + jnp.sum(dsk * mh, axis=1, keepdims=True) * mj
            seg = ac - jnp.sum(ac * eye, axis=0, keepdims=True)
            decays.append(jnp.where(tri, jnp.exp(jnp.where(tri, seg, 0.0)), 0.0))
            masks.append(mj)
        xdt = x_q * dt_e
        y = x_q * d_e + _bdot(jnp.concatenate([cb * dec for dec in decays], axis=1),
                              jnp.concatenate([xdt * mj for mj in masks], axis=0), _NN)
        a_last = jnp.sum(acum_e * last, axis=0, keepdims=True)
        ys.append(y + jnp.exp(acum_e) * _bdot(cm, st_q, _NN))
        sts.append(st_q * jnp.exp(a_last) + _bdot(bm, xdt * jnp.exp(a_last - acum_e), _TN))
    return tuple(ys) + tuple(sts)


def _pieces(ref, lead=()):
    return [ref[lead + (slice(None), slice(q * LANES, (q + 1) * LANES))].astype(F32) for q in range(ref.shape[-1] // LANES)]


def _ssd_specs(cfg, rev):
    l, hp, n, g = SSM_CHUNK, cfg["hp"], cfg["n"], SSM_GROUPS
    nc = cfg["s"] // l
    cc = (lambda c: nc - 1 - c) if rev else (lambda c: c)
    nb = cfg["inner"] // n
    par = pl.BlockSpec((1, SMALL_R), lambda c, q: (0, 0))
    specs = [pl.BlockSpec((l, hp), lambda c, q: (cc(c), q)),
             pl.BlockSpec((l, n), lambda c, q: (cc(c), nb + q)),
             pl.BlockSpec((l, n), lambda c, q: (cc(c), nb + g + q)),
             pl.BlockSpec((l, SMALL_R), lambda c, q: (cc(c), 0)), par, par, par]
    st_spec = pl.BlockSpec((None, None, n, hp), lambda c, q: (cc(c), q, 0, 0))
    y_spec = pl.BlockSpec((l, hp), lambda c, q: (cc(c), q))
    return specs, st_spec, y_spec, nc


def _ssd_fwd(xbc_c, proj, dtb, alog, dsk, cfg, name):
    specs, st_spec, y_spec, nc = _ssd_specs(cfg, False)
    fn = functools.partial(_ssd_chunk, hg=cfg["hg"], p=cfg["p"])
    npc = cfg["hp"] // LANES

    def body(xs, bm, cm, sm, dtb_r, alog_r, dsk_r, y_ref, sts_ref, st, dt_s, ac_s):
        c, g = pl.program_id(0), pl.program_id(1)

        @pl.when(c == 0)
        def _():
            st[g] = jnp.zeros(st.shape[1:], F32)

        @pl.when(g == 0)
        def _():
            dt_s[...], ac_s[...] = _ssd_gates(sm[...], dtb_r[...], alog_r[...])

        sts_ref[...] = st[g]
        out = fn(_pieces(xs), bm[...].astype(F32), cm[...].astype(F32), dt_s[...], ac_s[...], dsk_r[...],
                 _pieces(st, (g,)), g)
        for q in range(npc):
            y_ref[:, q * LANES:(q + 1) * LANES] = out[q]
            st[g, :, q * LANES:(q + 1) * LANES] = out[npc + q]

    return pl.pallas_call(
        body, name=name, grid=(nc, SSM_GROUPS), in_specs=specs, out_specs=[y_spec, st_spec],
        out_shape=[jax.ShapeDtypeStruct((cfg["s"], cfg["inner"]), F32),
                   jax.ShapeDtypeStruct((nc, SSM_GROUPS, cfg["n"], cfg["hp"]), F32)],
        scratch_shapes=[pltpu.VMEM((SSM_GROUPS, cfg["n"], cfg["hp"]), F32)] + [pltpu.VMEM((SSM_CHUNK, SMALL_R), F32)] * 2,
        compiler_params=_cp("arbitrary", "arbitrary"),
    )(xbc_c, xbc_c, xbc_c, proj, dtb, alog, dsk)


def _ssd_bwd(xbc_c, proj, dtb, alog, dsk, states, dy, cfg, name):
    specs, st_spec, y_spec, nc = _ssd_specs(cfg, True)
    l, n, gn = SSM_CHUNK, cfg["n"], SSM_GROUPS * cfg["n"]
    fn = functools.partial(_ssd_chunk, hg=cfg["hg"], p=cfg["p"])
    npc = cfg["hp"] // LANES
    rc = lambda c: nc - 1 - c

    def body(xs, bm, cm, sm, dtb_r, alog_r, dsk_r, sts_ref, dy_ref, dxs, dbm, dcm, dsm, ddtb, dalog, ddsk, dst,
             dt_s, ac_s, gdt_s, gac_s):
        c, g = pl.program_id(0), pl.program_id(1)

        @pl.when(c == 0)
        def _():
            dst[g] = jnp.zeros(dst.shape[1:], F32)

        @pl.when(g == 0)
        def _():
            dt_s[...], ac_s[...] = _ssd_gates(sm[...], dtb_r[...], alog_r[...])
            gdt_s[...] = jnp.zeros(gdt_s.shape, F32)
            gac_s[...] = jnp.zeros(gac_s.shape, F32)

        def f(*a):
            return fn(a[:npc], *a[npc:npc + 5], a[npc + 5:], g)

        _, vjp = jax.vjp(f, *_pieces(xs), bm[...].astype(F32), cm[...].astype(F32), dt_s[...], ac_s[...], dsk_r[...],
                         *_pieces(sts_ref))
        grads = vjp(tuple(_pieces(dy_ref)) + tuple(_pieces(dst, (g,))))
        gb, gc, gdt, gac, g3 = grads[npc:npc + 5]
        for q in range(npc):
            dxs[:, q * LANES:(q + 1) * LANES] = grads[q]
            dst[g, :, q * LANES:(q + 1) * LANES] = grads[npc + 5 + q]
        dbm[...] = gb
        dcm[...] = gc
        gdt_s[...] += gdt
        gac_s[...] += gac
        first = jnp.logical_and(c == 0, g == 0)

        @pl.when(first)
        def _():
            ddsk[...] = g3

        @pl.when(jnp.logical_not(first))
        def _():
            ddsk[...] += g3

        @pl.when(g == SSM_GROUPS - 1)
        def _():
            _, vjp_gates = jax.vjp(_ssd_gates, sm[...], dtb_r[...], alog_r[...])
            gs, g1, g2 = vjp_gates((gdt_s[...], gac_s[...]))
            dsm[...] = gs

            @pl.when(c == 0)
            def _():
                ddtb[...] = g1
                dalog[...] = g2

            @pl.when(c > 0)
            def _():
                ddtb[...] += g1
                dalog[...] += g2

    par = pl.BlockSpec((1, SMALL_R), lambda c, q: (0, 0))
    return pl.pallas_call(
        body, name=name, grid=(nc, SSM_GROUPS), in_specs=specs + [st_spec, y_spec],
        out_specs=[y_spec, pl.BlockSpec((l, n), lambda c, q: (rc(c), q)), pl.BlockSpec((l, n), lambda c, q: (rc(c), q)),
                   pl.BlockSpec((l, SMALL_R), lambda c, q: (rc(c), 0)), par, par, par],
        out_shape=[jax.ShapeDtypeStruct((cfg["s"], cfg["inner"]), F32), jax.ShapeDtypeStruct((cfg["s"], gn), F32),
                   jax.ShapeDtypeStruct((cfg["s"], gn), F32), jax.ShapeDtypeStruct((cfg["s"], SMALL_R), F32),
                   jax.ShapeDtypeStruct((1, SMALL_R), F32), jax.ShapeDtypeStruct((1, SMALL_R), F32),
                   jax.ShapeDtypeStruct((1, SMALL_R), F32)],
        scratch_shapes=[pltpu.VMEM((SSM_GROUPS, cfg["n"], cfg["hp"]), F32)] + [pltpu.VMEM((SSM_CHUNK, SMALL_R), F32)] * 4,
        compiler_params=_cp("arbitrary", "arbitrary"),
    )(xbc_c, xbc_c, xbc_c, proj, dtb, alog, dsk, states, dy)


def _gdn_gates(small, alog, dtb):
    c = DN_CHUNK
    rr, w_ = small.shape
    nb = rr // c
    r = lax.broadcasted_iota(jnp.int32, (c, c), 0)
    cc = lax.broadcasted_iota(jnp.int32, (c, c), 1)
    g_all = -jnp.exp(alog) * _softplus(small + dtb)
    tri_b = jnp.broadcast_to((r >= cc).astype(F32)[None], (nb, c, c))
    return _pmm(tri_b, g_all.reshape(nb, c, w_)), jax.nn.sigmoid(small)


def _gdn_prep(q, k, v, gcum_all, beta_all, h, *, boff, aoff):
    c = DN_CHUNK
    rr, dk = q.shape
    nb, w_ = rr // c, beta_all.shape[1]
    lane = lax.broadcasted_iota(jnp.int32, (1, w_), 1)
    r = lax.broadcasted_iota(jnp.int32, (c, c), 0)
    cc = lax.broadcasted_iota(jnp.int32, (c, c), 1)
    incl, strict = (r >= cc)[None], (r > cc)[None]
    eye = (r == cc).astype(F32)[None]
    gc = jnp.sum(gcum_all * (lane == aoff + h).astype(F32)[None], axis=2, keepdims=True)
    beta = jnp.sum(beta_all * (lane == boff + h).astype(F32), axis=1, keepdims=True).reshape(nb, c, 1)
    qn = (q * lax.rsqrt(jnp.sum(q * q, axis=1, keepdims=True) + EPS) * (dk ** -0.5)).reshape(nb, c, dk)
    kn = (k * lax.rsqrt(jnp.sum(k * k, axis=1, keepdims=True) + EPS)).reshape(nb, c, dk)
    v3 = v.reshape(nb, c, v.shape[1])
    g_row = jnp.sum(gc * eye, axis=1, keepdims=True)
    decay = jnp.where(incl, jnp.exp(jnp.where(incl, gc - g_row, 0.0)), 0.0)
    kb = kn * beta
    kk = jnp.einsum("bik,bjk->bij", kb.astype(BF16), kn.astype(BF16), preferred_element_type=F32)
    neg_m = jnp.where(strict, -kk * decay, 0.0)
    t_inv = _neumann_inv(neg_m)
    eg = jnp.exp(gc)
    u = _pmm(t_inv, v3 * beta)
    w = _pmm(t_inv, kb * eg)
    a_qk = jnp.einsum("bik,bjk->bij", qn.astype(BF16), kn.astype(BF16), preferred_element_type=F32) * decay
    last = (lax.broadcasted_iota(jnp.int32, (1, c, 1), 1) == c - 1).astype(F32)
    g_last = jnp.sum(gc * last, axis=1, keepdims=True)
    q_dec = qn * eg
    k_dec = kn * jnp.exp(g_last - gc)
    gl = jnp.broadcast_to(g_last, (nb, 1, LANES))
    return (u.reshape(rr, -1), w.reshape(rr, dk), q_dec.reshape(rr, dk), k_dec.reshape(rr, dk), a_qk, gl)


def _gdn_prep_specs(cfg):
    rr, hd, dk, dv = cfg["prep_rows"], cfg["hd"], cfg["dk"], cfg["dv"]
    nb = rr // DN_CHUNK
    ins = [pl.BlockSpec((rr, dk), lambda i, h: (i, h)), pl.BlockSpec((rr, dk), lambda i, h: (i, hd + h)),
           pl.BlockSpec((rr, dv), lambda i, h: (i, (2 * hd * dk) // dv + h)),
           pl.BlockSpec((rr, SMALL_R), lambda i, h: (i, 0)),
           pl.BlockSpec((1, SMALL_R), lambda i, h: (0, 0)), pl.BlockSpec((1, SMALL_R), lambda i, h: (0, 0))]
    hs = lambda d: pl.BlockSpec((None, rr, d), lambda i, h: (h, i, 0))
    outs = [hs(dv), hs(dk), hs(dk), hs(dk), pl.BlockSpec((None, nb, DN_CHUNK, DN_CHUNK), lambda i, h: (h, i, 0, 0)),
            pl.BlockSpec((None, nb, 1, LANES), lambda i, h: (h, i, 0, 0))]
    s, nch = cfg["s"], cfg["s"] // DN_CHUNK
    shapes = [jax.ShapeDtypeStruct((hd, s, dv), F32)] + [jax.ShapeDtypeStruct((hd, s, dk), F32)] * 3 + [
        jax.ShapeDtypeStruct((hd, nch, DN_CHUNK, DN_CHUNK), F32), jax.ShapeDtypeStruct((hd, nch, 1, LANES), F32)]
    return ins, outs, shapes


def _gdn_prep_fwd(qkv_c, proj, alog, dtb, cfg, name):
    ins, outs, shapes = _gdn_prep_specs(cfg)
    fn = functools.partial(_gdn_prep, boff=cfg["hs"], aoff=cfg["hs"] + cfg["hd"])

    rr = cfg["prep_rows"]

    def body(q, k, v, sm, al, db, *rest):
        o, (gc_s, be_s) = rest[:-2], rest[-2:]

        @pl.when(pl.program_id(1) == 0)
        def _():
            gc_s[...], be_s[...] = _gdn_gates(sm[...], al[...], db[...])

        vals = fn(q[...].astype(F32), k[...].astype(F32), v[...].astype(F32), gc_s[...], be_s[...], pl.program_id(1))
        for ref, val in zip(o, vals):
            ref[...] = val

    return pl.pallas_call(
        body, name=name, grid=(cfg["s"] // rr, cfg["hd"]), in_specs=ins, out_specs=outs, out_shape=shapes,
        scratch_shapes=[pltpu.VMEM((rr // DN_CHUNK, DN_CHUNK, SMALL_R), F32), pltpu.VMEM((rr, SMALL_R), F32)],
        compiler_params=_cp("parallel", "arbitrary"),
    )(qkv_c, qkv_c, qkv_c, proj, alog, dtb)


def _gdn_prep_bwd(qkv_c, proj, alog, dtb, cts, cfg, name):
    ins, outs, _ = _gdn_prep_specs(cfg)
    rr, hd, dk, dv, s = cfg["prep_rows"], cfg["hd"], cfg["dk"], cfg["dv"], cfg["s"]
    fn = functools.partial(_gdn_prep, boff=cfg["hs"], aoff=cfg["hs"] + cfg["hd"])

    def body(q, k, v, sm, al, db, c0, c1, c2, c3, c4, c5, dq, dkk, dvv, dsm, dal, ddb, gc_s, be_s, ggc_s, gbe_s):
        i, h = pl.program_id(0), pl.program_id(1)

        @pl.when(h == 0)
        def _():
            gc_s[...], be_s[...] = _gdn_gates(sm[...], al[...], db[...])
            ggc_s[...] = jnp.zeros(ggc_s.shape, F32)
            gbe_s[...] = jnp.zeros(gbe_s.shape, F32)

        f = lambda *a: fn(*a, h)
        _, vjp = jax.vjp(f, q[...].astype(F32), k[...].astype(F32), v[...].astype(F32), gc_s[...], be_s[...])
        gq, gk, gv, ggc, gbe = vjp((c0[...], c1[...], c2[...], c3[...], c4[...], c5[...]))
        dq[...] = gq
        dkk[...] = gk
        dvv[...] = gv
        ggc_s[...] += ggc
        gbe_s[...] += gbe

        @pl.when(h == hd - 1)
        def _():
            _, vjp_gates = jax.vjp(_gdn_gates, sm[...], al[...], db[...])
            gs, ga, gd = vjp_gates((ggc_s[...], gbe_s[...]))
            dsm[...] = gs

            @pl.when(i == 0)
            def _():
                dal[...] = ga
                ddb[...] = gd

            @pl.when(i > 0)
            def _():
                dal[...] += ga
                ddb[...] += gd

    par = pl.BlockSpec((1, SMALL_R), lambda i, h: (0, 0))
    return pl.pallas_call(
        body, name=name, grid=(s // rr, hd), in_specs=ins + outs,
        out_specs=[pl.BlockSpec((rr, dk), lambda i, h: (i, h)), pl.BlockSpec((rr, dk), lambda i, h: (i, h)),
                   pl.BlockSpec((rr, dv), lambda i, h: (i, h)), pl.BlockSpec((rr, SMALL_R), lambda i, h: (i, 0)), par, par],
        out_shape=[jax.ShapeDtypeStruct((s, hd * dk), F32), jax.ShapeDtypeStruct((s, hd * dk), F32),
                   jax.ShapeDtypeStruct((s, hd * dv), F32), jax.ShapeDtypeStruct((s, SMALL_R), F32),
                   jax.ShapeDtypeStruct((1, SMALL_R), F32), jax.ShapeDtypeStruct((1, SMALL_R), F32)],
        scratch_shapes=[pltpu.VMEM((rr // DN_CHUNK, DN_CHUNK, SMALL_R), F32), pltpu.VMEM((rr, SMALL_R), F32)] * 2,
        compiler_params=_cp("arbitrary", "arbitrary"),
    )(qkv_c, qkv_c, qkv_c, proj, alog, dtb, *cts)


def _gdn_scan_chunk(qd, aq, u, w, kd, gl, st):
    bm = lambda spec, a, b: jnp.einsum(spec, a.astype(BF16), b.astype(BF16), preferred_element_type=F32)
    v_new = u - bm("hck,hkv->hcv", w, st)
    o = bm("hck,hkv->hcv", qd, st) + bm("hij,hjv->hiv", aq, v_new)
    st_new = st * jnp.exp(gl) + bm("hck,hcv->hkv", kd, v_new)
    return o, st_new


def _gdn_scan_specs(cfg, rev):
    rr, hd, dk, dv = cfg["scan_rows"], cfg["hd"], cfg["dk"], cfg["dv"]
    nb, nblk = rr // DN_CHUNK, cfg["s"] // rr
    ii = (lambda i: nblk - 1 - i) if rev else (lambda i: i)
    hs = lambda d: pl.BlockSpec((hd, rr, d), lambda i: (0, ii(i), 0))
    ins = [hs(dk), pl.BlockSpec((hd, nb, DN_CHUNK, DN_CHUNK), lambda i: (0, ii(i), 0, 0)), hs(dv), hs(dk), hs(dk),
           pl.BlockSpec((hd, nb, 1, LANES), lambda i: (0, ii(i), 0, 0))]
    st_spec = pl.BlockSpec((nb, hd, dk, dv), lambda i: (ii(i), 0, 0, 0))
    return ins, hs(dv), st_spec, nb, nblk


def _gdn_scan_fwd(qd, aq, u, w, kd, gl, cfg, name):
    ins, o_spec, st_spec, nb, nblk = _gdn_scan_specs(cfg, False)
    hd, dk, dv, c = cfg["hd"], cfg["dk"], cfg["dv"], DN_CHUNK

    def body(qd_r, aq_r, u_r, w_r, kd_r, gl_r, o_r, sts_r, st):
        @pl.when(pl.program_id(0) == 0)
        def _():
            st[...] = jnp.zeros(st.shape, F32)

        s = st[...]
        for j in range(nb):
            rows = slice(j * c, (j + 1) * c)
            sts_r[j] = s
            o, s = _gdn_scan_chunk(qd_r[:, rows, :], aq_r[:, j], u_r[:, rows, :], w_r[:, rows, :], kd_r[:, rows, :],
                                   gl_r[:, j], s)
            o_r[:, rows, :] = o
        st[...] = s

    return pl.pallas_call(
        body, name=name, grid=(nblk,), in_specs=ins, out_specs=[o_spec, st_spec],
        out_shape=[jax.ShapeDtypeStruct((hd, cfg["s"], dv), F32),
                   jax.ShapeDtypeStruct((cfg["s"] // c, hd, dk, dv), F32)],
        scratch_shapes=[pltpu.VMEM((hd, dk, dv), F32)], compiler_params=_cp("arbitrary"),
    )(qd, aq, u, w, kd, gl)


def _gdn_scan_bwd(qd, aq, u, w, kd, gl, states, do, cfg, name):
    ins, o_spec, st_spec, nb, nblk = _gdn_scan_specs(cfg, True)
    hd, dk, dv, c = cfg["hd"], cfg["dk"], cfg["dv"], DN_CHUNK

    def body(qd_r, aq_r, u_r, w_r, kd_r, gl_r, sts_r, do_r, dqd, daq, du, dw, dkd, dgl, dst):
        @pl.when(pl.program_id(0) == 0)
        def _():
            dst[...] = jnp.zeros(dst.shape, F32)

        ds = dst[...]
        for j in reversed(range(nb)):
            rows = slice(j * c, (j + 1) * c)
            _, vjp = jax.vjp(_gdn_scan_chunk, qd_r[:, rows, :], aq_r[:, j], u_r[:, rows, :], w_r[:, rows, :],
                             kd_r[:, rows, :], gl_r[:, j], sts_r[j])
            g0, g1, g2, g3, g4, g5, ds = vjp((do_r[:, rows, :], ds))
            dqd[:, rows, :] = g0
            daq[:, j] = g1
            du[:, rows, :] = g2
            dw[:, rows, :] = g3
            dkd[:, rows, :] = g4
            dgl[:, j] = g5
        dst[...] = ds

    s, nch = cfg["s"], cfg["s"] // c
    return pl.pallas_call(
        body, name=name, grid=(nblk,), in_specs=ins + [st_spec, o_spec], out_specs=ins,
        out_shape=[jax.ShapeDtypeStruct((hd, s, dk), F32), jax.ShapeDtypeStruct((hd, nch, c, c), F32),
                   jax.ShapeDtypeStruct((hd, s, dv), F32), jax.ShapeDtypeStruct((hd, s, dk), F32),
                   jax.ShapeDtypeStruct((hd, s, dk), F32), jax.ShapeDtypeStruct((hd, nch, 1, LANES), F32)],
        scratch_shapes=[pltpu.VMEM((hd, dk, dv), F32)], compiler_params=_cp("arbitrary"),
    )(qd, aq, u, w, kd, gl, states, do)


def _loss_head(h, target, w, name):
    s, d = h.shape
    t = _tile(s, 512, 8)

    def body(h_r, t_r, w_r, loss_r, dh_r, dw_r):
        i = pl.program_id(0)
        (y,), vjp = jax.vjp(_rms_fn, h_r[...], w_r[...])
        err = y - t_r[...]
        part = 0.5 * jnp.sum(jnp.mean(err * err, axis=-1, keepdims=True))
        gh, gw = vjp((err * (1.0 / d),))
        dh_r[...] = gh

        @pl.when(i == 0)
        def _():
            loss_r[...] = jnp.zeros(loss_r.shape, F32) + part
            dw_r[...] = gw

        @pl.when(i > 0)
        def _():
            loss_r[...] += part
            dw_r[...] += gw

    row = pl.BlockSpec((t, d), lambda i: (i, 0))
    return pl.pallas_call(
        body, name=name, grid=(s // t,), in_specs=[row, row, pl.BlockSpec((1, d), lambda i: (0, 0))],
        out_specs=[pl.BlockSpec((8, LANES), lambda i: (0, 0)), row, pl.BlockSpec((1, d), lambda i: (0, 0))],
        out_shape=[jax.ShapeDtypeStruct((8, LANES), F32), jax.ShapeDtypeStruct((s, d), F32), jax.ShapeDtypeStruct((1, d), F32)],
        compiler_params=_cp("arbitrary"),
    )(h, target, w)


def _adamw_math(g, w, m, v):
    m = ADAM_B1 * m + (1.0 - ADAM_B1) * g
    v = ADAM_B2 * v + (1.0 - ADAM_B2) * jnp.square(g)
    m_hat = m / (1.0 - ADAM_B1 ** ADAM_STEP)
    v_hat = v / (1.0 - ADAM_B2 ** ADAM_STEP)
    delta = -ADAM_LR * (m_hat / (jnp.sqrt(v_hat) + ADAM_EPS) + ADAM_WD * w)
    return delta, m, v


def _pair_sum(mine, theirs, name):
    _, nch, nl, r, c = mine.shape
    tr = _tile(r, 128, 16)

    def body(a_r, b_r, o_r):
        o_r[...] = (a_r[...].astype(F32) + b_r[...].astype(F32)).astype(o_r.dtype)

    return pl.pallas_call(
        body, name=name, grid=(nch, nl, r // tr),
        in_specs=[pl.BlockSpec((None, None, None, tr, c), lambda p, a, i: (0, p, a, i, 0)),
                  pl.BlockSpec((None, None, tr, c), lambda p, a, i: (p, a, i, 0))],
        out_specs=pl.BlockSpec((None, None, tr, c), lambda p, a, i: (p, a, i, 0)),
        out_shape=jax.ShapeDtypeStruct(theirs.shape, theirs.dtype), compiler_params=_cp("parallel", "parallel", "parallel"),
    )(mine, theirs)


def _adamw_sum(parts, w, m, v, name):
    nl, r, c = w.shape
    n_parts = parts.shape[0]
    tr = _tile(r, 64, 8)

    def body(p_r, w_r, m_r, v_r, g_o, d_o, m_o, v_o):
        g = p_r[0].astype(F32)
        for q in range(1, n_parts):
            g = g + p_r[q].astype(F32)
        delta, mn, vn = _adamw_math(g, w_r[...], m_r[...], v_r[...])
        g_o[...] = g
        d_o[...] = delta
        m_o[...] = mn
        v_o[...] = vn

    blk = pl.BlockSpec((None, tr, c), lambda a, i: (a, i, 0))
    return pl.pallas_call(
        body, name=name, grid=(nl, r // tr),
        in_specs=[pl.BlockSpec((n_parts, None, tr, c), lambda a, i: (0, a, i, 0)), blk, blk, blk],
        out_specs=[blk] * 4, out_shape=[jax.ShapeDtypeStruct(w.shape, F32)] * 4, compiler_params=_cp("parallel", "parallel"),
    )(parts, w, m, v)


def _sum_parts(parts, name):
    _, r, c = parts.shape
    tr = _tile(r, 512, 8)

    def body(p_r, o_r):
        g = p_r[0]
        for q in range(1, N_DEV):
            g = g + p_r[q]
        o_r[...] = g

    return pl.pallas_call(
        body, name=name, grid=(r // tr,), in_specs=[pl.BlockSpec((N_DEV, tr, c), lambda i: (0, i, 0))],
        out_specs=pl.BlockSpec((tr, c), lambda i: (i, 0)), out_shape=jax.ShapeDtypeStruct((r, c), F32),
        compiler_params=_cp("parallel"),
    )(parts)


def _adamw_flat(g, w, m, v, name):
    r, c = w.shape
    tr = _tile(r, 512, 8)

    def body(g_r, w_r, m_r, v_r, d_o, m_o, v_o):
        delta, mn, vn = _adamw_math(g_r[...], w_r[...], m_r[...], v_r[...])
        d_o[...] = delta
        m_o[...] = mn
        v_o[...] = vn

    blk = pl.BlockSpec((tr, c), lambda i: (i, 0))
    return pl.pallas_call(
        body, name=name, grid=(r // tr,), in_specs=[blk] * 4, out_specs=[blk] * 3,
        out_shape=[jax.ShapeDtypeStruct(w.shape, F32)] * 3, compiler_params=_cp("parallel"),
    )(g, w, m, v)


def _remote(src, dst, send_sems, recv_sems, idx, to):
    return pltpu.make_async_remote_copy(src_ref=src, dst_ref=dst, send_sem=send_sems.at[idx], recv_sem=recv_sems.at[idx],
                                        device_id=to, device_id_type=MESH)


def _comm_call(body, name, arrays, out_shapes, n_sems):
    nt = len(arrays)
    return pl.pallas_call(
        body, name=name, in_specs=[pl.BlockSpec(memory_space=pl.ANY)] * nt, out_specs=[pl.BlockSpec(memory_space=pl.ANY)] * nt,
        out_shape=out_shapes,
        scratch_shapes=[pltpu.SemaphoreType.DMA((nt, n_sems)), pltpu.SemaphoreType.DMA((nt, n_sems)),
                        pltpu.SemaphoreType.DMA((nt,))],
    )(*arrays)


def _gather_all(arrays, name):
    nt = len(arrays)

    def body(*refs):
        srcs, dsts = refs[:nt], refs[nt:2 * nt]
        send_sems, recv_sems, local_sems = refs[2 * nt:]
        x, y, c = lax.axis_index("x"), lax.axis_index("y"), lax.axis_index("c")
        slot = lambda px, py, pc: 4 * px + 2 * py + pc
        me, sib = slot(x, y, c), (x, y, 1 - c)
        chips = [(1 - x, y), (x, 1 - y), (1 - x, 1 - y)]
        local = [pltpu.make_async_copy(srcs[t], dsts[t].at[me], local_sems.at[t]) for t in range(nt)]
        for cp in local:
            cp.start()
        sends = []
        for t in range(nt):
            sends.append(_remote(srcs[t], dsts[t].at[me], send_sems, recv_sems, (t, 0), sib))
            for j, (px, py) in enumerate(chips):
                sends.append(_remote(srcs[t], dsts[t].at[me], send_sems, recv_sems, (t, 1 + j), (px, py, c)))
        for cp in sends:
            cp.start()
        for j, (px, py) in enumerate(chips):
            landed = slot(px, py, c)
            for t in range(nt):
                _remote(srcs[t], dsts[t].at[landed], send_sems, recv_sems, (t, 1 + j), (px, py, c)).wait_recv()
                fwd = _remote(dsts[t].at[landed], dsts[t].at[landed], send_sems, recv_sems, (t, 4 + j), sib)
                fwd.start()
                sends.append(fwd)
        for t in range(nt):
            _remote(srcs[t], dsts[t].at[slot(x, y, 1 - c)], send_sems, recv_sems, (t, 0), sib).wait_recv()
            for j, (px, py) in enumerate(chips):
                _remote(srcs[t], dsts[t].at[slot(px, py, 1 - c)], send_sems, recv_sems, (t, 4 + j), sib).wait_recv()
        for cp in sends:
            cp.wait_send()
        for cp in local:
            cp.wait()

    return _comm_call(body, name, arrays, [jax.ShapeDtypeStruct((N_DEV,) + a.shape, a.dtype) for a in arrays], N_DEV - 1)


def _sibling_swap(arrays, name):
    nt = len(arrays)

    def body(*refs):
        srcs, dsts = refs[:nt], refs[nt:2 * nt]
        send_sems, recv_sems, _ = refs[2 * nt:]
        sib = (lax.axis_index("x"), lax.axis_index("y"), 1 - lax.axis_index("c"))
        copies = [_remote(srcs[t].at[1], dsts[t], send_sems, recv_sems, (t, 0), sib) for t in range(nt)]
        for cp in copies:
            cp.start()
        for cp in copies:
            cp.wait()

    return _comm_call(body, name, arrays, [jax.ShapeDtypeStruct(a.shape[1:], a.dtype) for a in arrays], 1)


def _chip_scatter(arrays, name):
    nt = len(arrays)

    def body(*refs):
        srcs, dsts = refs[:nt], refs[nt:2 * nt]
        send_sems, recv_sems, local_sems = refs[2 * nt:]
        x, y, c = lax.axis_index("x"), lax.axis_index("y"), lax.axis_index("c")
        mine = 2 * x + y
        local = [pltpu.make_async_copy(srcs[t].at[mine], dsts[t].at[mine], local_sems.at[t]) for t in range(nt)]
        for cp in local:
            cp.start()
        sends, arrivals = [], []
        for j, (px, py) in enumerate([(1 - x, y), (x, 1 - y), (1 - x, 1 - y)]):
            theirs = 2 * px + py
            for t in range(nt):
                sends.append(_remote(srcs[t].at[theirs], dsts[t].at[mine], send_sems, recv_sems, (t, j), (px, py, c)))
                arrivals.append(_remote(srcs[t].at[theirs], dsts[t].at[theirs], send_sems, recv_sems, (t, j), (px, py, c)))
        for cp in sends:
            cp.start()
        for cp in arrivals:
            cp.wait_recv()
        for cp in sends:
            cp.wait_send()
        for cp in local:
            cp.wait()

    return _comm_call(body, name, arrays, [jax.ShapeDtypeStruct(a.shape, a.dtype) for a in arrays], N_CHIP - 1)


def _config(x, ffn1_w_out, ssm_conv_b, ssm_dt_bias, ssm_norm, dn_conv_w, dn_dt_bias, dn_norm, dn_w_branch):
    s, d = x.shape[-2], x.shape[-1]
    f = ffn1_w_out.shape[1] * N_DEV
    cs, hs, inner = ssm_conv_b.shape[1], ssm_dt_bias.shape[1], ssm_norm.shape[1]
    gn = (cs - inner) // 2
    hd, dv = dn_dt_bias.shape[1], dn_norm.shape[1]
    cd = dn_conv_w.shape[2] * N_DEV
    vd = hd * dv
    kd = (cd - vd) // 2
    cfg = dict(s=s, d=d, f=f, cs=cs, hs=hs, inner=inner, gn=gn, n=gn // SSM_GROUPS, hg=hs // SSM_GROUPS, p=inner // hs,
               hp=inner // SSM_GROUPS, hd=hd, dv=dv, dk=kd // hd, cd=cd, vd=vd, kd=kd)
    offs, o = {}, 0
    for nm, wd in (("xbc", cs), ("qkv", cd), ("gates", 2 * d), ("zs", inner), ("zd", vd), ("small", SMALL_W)):
        offs[nm] = o
        o += wd
    cfg["offs"], cfg["pw"] = offs, o
    cfg["cw"] = 512
    cfg["pw_main"] = offs["small"]
    cfg["prep_rows"] = min(s, 8 * DN_CHUNK)
    cfg["scan_rows"] = min(s, 4 * DN_CHUNK)
    cfg["in_split"] = (inner, cs, hs, cd, vd, hd, hd, d, d)
    assert hs + 2 * hd <= SMALL_R and cfg["dk"] == dv and dv == LANES and LANES % cfg["p"] == 0 and cfg["hp"] % LANES == 0
    assert offs["qkv"] % cfg["cw"] == 0 and offs["gates"] % (2 * d) == 0 and offs["zs"] % cfg["hp"] == 0
    assert offs["zd"] % dv == 0 and all(wd % cfg["cw"] == 0 for wd in (inner, gn, kd, vd)) and inner % cfg["n"] == 0
    return cfg


def _permute_w_in(w, cfg):
    pts = [0]
    for wd in cfg["in_split"]:
        pts.append(pts[-1] + wd)
    z_s, xbc, dt, qkv, z_d, b_d, a_d, g_s, g_d = [w[:, pts[i]:pts[i + 1]] for i in range(9)]
    pad = jnp.zeros((w.shape[0], SMALL_W - dt.shape[1] - b_d.shape[1] - a_d.shape[1]), w.dtype)
    return jnp.concatenate([xbc, qkv, g_s, g_d, z_s, z_d, dt, b_d, a_d, pad], axis=1)


def _unpermute_w_in(g, cfg):
    o, d = cfg["offs"], cfg["d"]
    hs, hd = cfg["hs"], cfg["hd"]
    sm = g[:, o["small"]:]
    return jnp.concatenate([
        g[:, o["zs"]:o["zs"] + cfg["inner"]], g[:, o["xbc"]:o["xbc"] + cfg["cs"]], sm[:, :hs],
        g[:, o["qkv"]:o["qkv"] + cfg["cd"]], g[:, o["zd"]:o["zd"] + cfg["vd"]], sm[:, hs:hs + hd], sm[:, hs + hd:hs + 2 * hd],
        g[:, o["gates"]:o["gates"] + d], g[:, o["gates"] + d:o["gates"] + 2 * d]], axis=1)


def _lane_row(v, off):
    return jnp.pad(v.astype(F32), (off, SMALL_R - off - v.shape[0]))[None]


def _pack(arrs):
    flat = []
    for a in arrs:
        v = a.reshape(-1)
        flat.append(jnp.pad(v, (0, (-v.shape[0]) % LANES)))
    v = jnp.concatenate(flat)
    v = jnp.pad(v, (0, (-v.shape[0]) % (8 * LANES)))
    return v.reshape(-1, LANES)


def _unpack(packed, shapes):
    v, out, o = packed.reshape(-1), [], 0
    for sh in shapes:
        n = math.prod(sh)
        out.append(v[o:o + n].reshape(sh))
        o += n + (-n) % LANES
    return out


def _cols_gathered(g):
    nd, nl, r, c = g.shape
    return jnp.transpose(g, (1, 2, 0, 3)).reshape(nl, r, nd * c)


def _rows_gathered(g):
    nd, nl, r, c = g.shape
    return jnp.transpose(g, (1, 0, 2, 3)).reshape(nl, nd * r, c)


def _scatter_layout(g, cols, core):
    nl = g.shape[0]
    if cols:
        r, c = g.shape[1], g.shape[2] // N_DEV
        t = jnp.transpose(g.reshape(nl, r, N_CHIP, 2, c), (3, 2, 0, 1, 4))
    else:
        r, c = g.shape[1] // N_DEV, g.shape[2]
        t = jnp.transpose(g.reshape(nl, N_CHIP, 2, r, c), (2, 1, 0, 3, 4))
    return jnp.where(core == 0, t, t[::-1]).astype(BF16)


def _ffn_fwd(h, nw, w_in, w_out, cfg, tag):
    s, d, f = cfg["s"], cfg["d"], cfg["f"]
    t = _tile(s, 512, 8)
    xn, = _rw_fwd(_rms_fn, f"{tag}_norm", (1, s // t), [(h, _rows(t, d), None), (nw, _par(d), None)],
                  [(jax.ShapeDtypeStruct((s, d), BF16), _rows(t, d), None)])
    gu = _matmul(xn, w_in, mode="nn", name=f"{tag}_in", out_dtype=BF16)
    t2 = _tile(s, 256, 8)
    act, = _rw_fwd(_swiglu_fn, f"{tag}_act", (1, s // t2), [(gu, _rows(t2, 2 * f), [f, f])],
                   [(jax.ShapeDtypeStruct((s, f), BF16), _rows(t2, f), None)])
    h_out = _matmul(act, w_out, mode="nn", name=f"{tag}_out", res=h, scale=0.5)
    return h_out, dict(h=h, xn=xn, gu=gu, act=act)


def _ffn_bwd(gh, r, nw, w_in, w_out, cfg, tag):
    s, d, f = cfg["s"], cfg["d"], cfg["f"]
    d_wout = _matmul(r["act"], gh, mode="tn", name=f"{tag}_dwout", scale=0.5)
    d_act = _matmul(gh, w_out, mode="nt", name=f"{tag}_dact", scale=0.5, out_dtype=BF16)
    t2 = _tile(s, 256, 8)
    d_gu, = _rw_bwd(_swiglu_fn, f"{tag}_dgu", (1, s // t2), [(r["gu"], _rows(t2, 2 * f), [f, f])],
                    [(d_act, _rows(t2, f), None)], [(0, jax.ShapeDtypeStruct((s, 2 * f), BF16), _rows(t2, 2 * f), "tile")])
    d_win = _matmul(r["xn"], d_gu, mode="tn", name=f"{tag}_dwin")
    d_xn = _matmul(d_gu, w_in, mode="nt", name=f"{tag}_dxn", out_dtype=BF16)
    t = _tile(s, 512, 8)
    d_h, d_nw = _rw_bwd(_rms_fn, f"{tag}_dnorm", (1, s // t), [(r["h"], _rows(t, d), None), (nw, _par(d), None)],
                        [(d_xn, _rows(t, d), None)],
                        [(0, jax.ShapeDtypeStruct((s, d), F32), _rows(t, d), "tile"),
                         (1, jax.ShapeDtypeStruct((1, d), F32), _par(d), "acc_all")],
                        add=(gh, _rows(t, d), 0))
    return d_h, (d_nw, d_win, d_wout)


def _mix_fwd(h, p, cfg, tag):
    s, d, o = cfg["s"], cfg["d"], cfg["offs"]
    t = _tile(s, 512, 8)
    u, = _rw_fwd(_rms_fn, f"{tag}_norm", (1, s // t), [(h, _rows(t, d), None), (p["mix_norm"], _par(d), None)],
                 [(jax.ShapeDtypeStruct((s, d), BF16), _rows(t, d), None)])
    proj = _matmul(u, p["w_in"], mode="nn", name=f"{tag}_in", out_dtype=BF16)
    small = _matmul(u, p["w_small"], mode="nn", name=f"{tag}_insmall")
    cw, tr = cfg["cw"], _tile(s, 512, 8)
    xbc_c = _conv_fwd(proj, o["xbc"] // cw, p["ssm_conv_w"], p["ssm_conv_b"], name=f"{tag}_sconv", cw=cw, tr=tr)
    qkv_c = _conv_fwd(proj, o["qkv"] // cw, p["dn_conv_w"], jnp.zeros((1, cfg["cd"]), F32), name=f"{tag}_dconv", cw=cw, tr=tr)
    y, s_states = _ssd_fwd(xbc_c, small, p["ssm_dtb"], p["ssm_alog"], p["ssm_dsk"], cfg, f"{tag}_ssd")
    hp, inner, g = cfg["hp"], cfg["inner"], SSM_GROUPS
    t4 = _tile(s, 512, 8)
    zs_blk = o["zs"] // hp
    y_s, = _rw_fwd(_ssm_out_fn, f"{tag}_sout", (g, s // t4),
                   [(y, _rows(t4, hp, lambda j: j), None), (proj, _rows(t4, hp, lambda j: zs_blk + j), None),
                    (p["ssm_norm"], _par(hp, True), None)],
                   [(jax.ShapeDtypeStruct((s, inner), BF16), _rows(t4, hp, lambda j: j), None)])
    uu, ww, qd, kd, aq, gl = _gdn_prep_fwd(qkv_c, small, p["dn_alog"], p["dn_dtb"], cfg, f"{tag}_prep")
    o_dn, d_states = _gdn_scan_fwd(qd, aq, uu, ww, kd, gl, cfg, f"{tag}_scan")
    hd, dv = cfg["hd"], cfg["dv"]
    zd_blk = o["zd"] // dv
    o_spec = pl.BlockSpec((None, t4, dv), lambda j, i: (j, i, 0))
    y_d, = _rw_fwd(_dn_out_fn, f"{tag}_dout", (hd, s // t4),
                   [(o_dn, o_spec, None), (proj, _rows(t4, dv, lambda j: zd_blk + j), None), (p["dn_norm"], _par(dv), None)],
                   [(jax.ShapeDtypeStruct((s, cfg["vd"]), BF16), _rows(t4, dv, lambda j: j), None)])
    ps = _matmul(y_s, p["ssm_w_branch"], mode="nn", name=f"{tag}_sbr")
    pd = _matmul(y_d, p["dn_w_branch"], mode="nn", name=f"{tag}_dbr")
    t6 = _tile(s, 256, 8)
    merged, = _rw_fwd(_merge_fn, f"{tag}_merge", (1, s // t6),
                      [(proj, _rows(t6, 2 * d, o["gates"] // (2 * d)), [d, d]), (ps, _rows(t6, d), None), (pd, _rows(t6, d), None)],
                      [(jax.ShapeDtypeStruct((s, d), BF16), _rows(t6, d), None)])
    h_out = _matmul(merged, p["w_out"], mode="nn", name=f"{tag}_out", res=h)
    res = dict(h=h, u=u, proj=proj, small=small, xbc_c=xbc_c, qkv_c=qkv_c, y=y, s_states=s_states, y_s=y_s, uu=uu, ww=ww, qd=qd, kd=kd,
               aq=aq, gl=gl, o_dn=o_dn, d_states=d_states, y_d=y_d, ps=ps, pd=pd, merged=merged)
    return h_out, res


def _mix_bwd(gh, r, p, cfg, tag):
    s, d, o = cfg["s"], cfg["d"], cfg["offs"]
    cw, tr = cfg["cw"], _tile(s, 512, 8)
    hp, inner, g, hd, dv, dk = cfg["hp"], cfg["inner"], SSM_GROUPS, cfg["hd"], cfg["dv"], cfg["dk"]
    proj = r["proj"]
    grads = {}
    grads["w_out"] = _matmul(r["merged"], gh, mode="tn", name=f"{tag}_dwout")
    d_merged = _matmul(gh, p["w_out"], mode="nt", name=f"{tag}_dmerged", out_dtype=BF16)
    dproj = jax.ShapeDtypeStruct((s, cfg["pw_main"]), BF16)
    t6 = _tile(s, 256, 8)
    gates_spec = _rows(t6, 2 * d, o["gates"] // (2 * d))
    dproj, d_ps, d_pd = _rw_bwd(
        _merge_fn, f"{tag}_dmerge", (1, s // t6),
        [(proj, gates_spec, [d, d]), (r["ps"], _rows(t6, d), None), (r["pd"], _rows(t6, d), None)],
        [(d_merged, _rows(t6, d), None)],
        [(0, dproj, gates_spec, "tile"), (1, jax.ShapeDtypeStruct((s, d), BF16), _rows(t6, d), "tile"),
         (2, jax.ShapeDtypeStruct((s, d), BF16), _rows(t6, d), "tile")])
    grads["ssm_w_branch"] = _matmul(r["y_s"], d_ps, mode="tn", name=f"{tag}_dwsbr")
    grads["dn_w_branch"] = _matmul(r["y_d"], d_pd, mode="tn", name=f"{tag}_dwdbr")
    d_ys = _matmul(d_ps, p["ssm_w_branch"], mode="nt", name=f"{tag}_dys", out_dtype=BF16)
    d_yd = _matmul(d_pd, p["dn_w_branch"], mode="nt", name=f"{tag}_dyd", out_dtype=BF16)
    t4 = _tile(s, 512, 8)
    zs_blk, zd_blk = o["zs"] // hp, o["zd"] // dv
    zs_spec = _rows(t4, hp, lambda j: zs_blk + j)
    d_y, dproj, grads["ssm_norm"] = _rw_bwd(
        _ssm_out_fn, f"{tag}_dsout", (g, s // t4),
        [(r["y"], _rows(t4, hp, lambda j: j), None), (proj, zs_spec, None), (p["ssm_norm"], _par(hp, True), None)],
        [(d_ys, _rows(t4, hp, lambda j: j), None)],
        [(0, jax.ShapeDtypeStruct((s, inner), F32), _rows(t4, hp, lambda j: j), "tile"),
         (1, jax.ShapeDtypeStruct(dproj.shape, BF16), zs_spec, "tile"),
         (2, jax.ShapeDtypeStruct((1, inner), F32), _par(hp, True), "acc_row")],
        alias=(dproj, 1))
    o_spec = pl.BlockSpec((None, t4, dv), lambda j, i: (j, i, 0))
    zd_spec = _rows(t4, dv, lambda j: zd_blk + j)
    d_o, dproj, grads["dn_norm"] = _rw_bwd(
        _dn_out_fn, f"{tag}_ddout", (hd, s // t4),
        [(r["o_dn"], o_spec, None), (proj, zd_spec, None), (p["dn_norm"], _par(dv), None)],
        [(d_yd, _rows(t4, dv, lambda j: j), None)],
        [(0, jax.ShapeDtypeStruct((hd, s, dv), F32), o_spec, "tile"),
         (1, jax.ShapeDtypeStruct(dproj.shape, BF16), zd_spec, "tile"),
         (2, jax.ShapeDtypeStruct((1, dv), F32), _par(dv), "acc_all")],
        alias=(dproj, 1))
    d_xs, d_bm, d_cm, dsm_s, g_dtb, g_alog, g_dsk = _ssd_bwd(
        r["xbc_c"], r["small"], p["ssm_dtb"], p["ssm_alog"], p["ssm_dsk"], r["s_states"], d_y, cfg, f"{tag}_dssd")
    grads["ssm_dt_bias"], grads["ssm_a_log"], grads["ssm_d"] = (v[0, :cfg["hs"]] for v in (g_dtb, g_alog, g_dsk))
    dws, dbs, col = [], [], 0
    for nm, dy in (("xs", d_xs), ("bm", d_bm), ("cm", d_cm)):
        wd = dy.shape[1]
        dproj, dw_, db_ = _conv_bwd(proj, (o["xbc"] + col) // cw, p["ssm_conv_w"][:, col:col + wd],
                                    p["ssm_conv_b"][:, col:col + wd], dy, dproj, (o["xbc"] + col) // cw,
                                    name=f"{tag}_dsconv_{nm}", cw=cw, tr=tr)
        dws.append(dw_)
        dbs.append(db_)
        col += wd
    grads["ssm_conv_w"] = jnp.concatenate(dws, axis=1)
    grads["ssm_conv_b"] = jnp.concatenate(dbs, axis=1)[0]
    cts = _gdn_scan_bwd(r["qd"], r["aq"], r["uu"], r["ww"], r["kd"], r["gl"], r["d_states"], d_o, cfg, f"{tag}_dscan")
    d_qd, d_aq, d_uu, d_ww, d_kd, d_gl = cts
    d_q, d_k, d_v, dsm_d, g_alog_d, g_dtb_d = _gdn_prep_bwd(
        r["qkv_c"], r["small"], p["dn_alog"], p["dn_dtb"], (d_uu, d_ww, d_qd, d_kd, d_aq, d_gl), cfg, f"{tag}_dprep")
    a0 = cfg["hs"] + hd
    grads["dn_a_log"], grads["dn_dt_bias"] = g_alog_d[0, a0:a0 + hd], g_dtb_d[0, a0:a0 + hd]
    dws, col = [], 0
    zero_b = jnp.zeros((1, cfg["cd"]), F32)
    for nm, dy in (("q", d_q), ("k", d_k), ("v", d_v)):
        wd = dy.shape[1]
        dproj, dw_, _ = _conv_bwd(proj, (o["qkv"] + col) // cw, p["dn_conv_w"][:, col:col + wd], zero_b[:, col:col + wd], dy,
                                  dproj, (o["qkv"] + col) // cw, name=f"{tag}_ddconv_{nm}", cw=cw, tr=tr)
        dws.append(dw_)
        col += wd
    grads["dn_conv_w"] = jnp.concatenate(dws, axis=1)
    d_small = _dsmall(dsm_s, dsm_d, f"{tag}_dsmall")
    grads["w_in"] = jnp.concatenate([_matmul(r["u"], dproj, mode="tn", name=f"{tag}_dwin"),
                                     _matmul(r["u"], d_small, mode="tn", name=f"{tag}_dwinsmall")], axis=1)
    d_u = _matmul(d_small, p["w_small"], mode="nt", name=f"{tag}_dusmall")
    d_u = _matmul(dproj, p["w_in"], mode="nt", name=f"{tag}_du", res=d_u, out_dtype=BF16)
    t = _tile(s, 512, 8)
    d_h, grads["mix_norm"] = _rw_bwd(
        _rms_fn, f"{tag}_dnorm", (1, s // t), [(r["h"], _rows(t, d), None), (p["mix_norm"], _par(d), None)],
        [(d_u, _rows(t, d), None)],
        [(0, jax.ShapeDtypeStruct((s, d), F32), _rows(t, d), "tile"), (1, jax.ShapeDtypeStruct((1, d), F32), _par(d), "acc_all")],
        add=(gh, _rows(t, d), 0))
    return d_h, grads


_BIG = ("ffn1_w_in", "ffn1_w_out", "w_in", "ssm_w_branch", "dn_w_branch", "w_out", "ffn2_w_in", "ffn2_w_out")
_COL_SHARDED = ("ffn1_w_in", "w_in", "ffn2_w_in")
_CONV = ("ssm_conv_w", "dn_conv_w")
_NAMES = ("ffn1_norm", "ffn1_w_in", "ffn1_w_out", "mix_norm", "w_in", "ssm_conv_w", "ssm_conv_b", "ssm_dt_bias", "ssm_a_log",
          "ssm_d", "ssm_norm", "ssm_w_branch", "dn_conv_w", "dn_dt_bias", "dn_a_log", "dn_norm", "dn_w_branch", "w_out",
          "ffn2_norm", "ffn2_w_in", "ffn2_w_out", "final_norm")


def kernel(x, ffn1_norm, ffn1_w_in, ffn1_w_out, mix_norm, w_in, ssm_conv_w, ssm_conv_b, ssm_dt_bias, ssm_a_log, ssm_d, ssm_norm, ssm_w_branch, dn_conv_w, dn_dt_bias, dn_a_log, dn_norm, dn_w_branch, w_out, ffn2_norm, ffn2_w_in, ffn2_w_out, final_norm, loss_target, m_ffn1_norm, m_ffn1_w_in, m_ffn1_w_out, m_mix_norm, m_w_in, m_ssm_conv_w, m_ssm_conv_b, m_ssm_dt_bias, m_ssm_a_log, m_ssm_d, m_ssm_norm, m_ssm_w_branch, m_dn_conv_w, m_dn_dt_bias, m_dn_a_log, m_dn_norm, m_dn_w_branch, m_w_out, m_ffn2_norm, m_ffn2_w_in, m_ffn2_w_out, m_final_norm, v_ffn1_norm, v_ffn1_w_in, v_ffn1_w_out, v_mix_norm, v_w_in, v_ssm_conv_w, v_ssm_conv_b, v_ssm_dt_bias, v_ssm_a_log, v_ssm_d, v_ssm_norm, v_ssm_w_branch, v_dn_conv_w, v_dn_dt_bias, v_dn_a_log, v_dn_norm, v_dn_w_branch, v_w_out, v_ffn2_norm, v_ffn2_w_in, v_ffn2_w_out, v_final_norm):
    args = locals()
    w = {n: args[n] for n in _NAMES}
    mom = {n: args["m_" + n] for n in _NAMES}
    var = {n: args["v_" + n] for n in _NAMES}
    cfg = _config(x, ffn1_w_out, ssm_conv_b, ssm_dt_bias, ssm_norm, dn_conv_w, dn_dt_bias, dn_norm, dn_w_branch)
    depth, s, d = ffn1_norm.shape[0], cfg["s"], cfg["d"]
    me = 4 * lax.axis_index("x") + 2 * lax.axis_index("y") + lax.axis_index("c")

    gathered = _gather_all([w[n].astype(BF16) for n in _BIG] + [w[n] for n in _CONV], "gather_weights")
    full = {}
    for n, g in zip(_BIG + _CONV, gathered):
        full[n] = _cols_gathered(g) if (n in _COL_SHARDED or n in _CONV) else _rows_gathered(g)

    hs, hd = cfg["hs"], cfg["hd"]
    layers = []
    for l in range(depth):
        w_perm = _permute_w_in(full["w_in"][l], cfg)
        layers.append(dict(
            ffn1_norm=ffn1_norm[l][None], ffn1_w_in=full["ffn1_w_in"][l], ffn1_w_out=full["ffn1_w_out"][l],
            mix_norm=mix_norm[l][None], w_in=w_perm[:, :cfg["pw_main"]], w_small=w_perm[:, cfg["pw_main"]:],
            ssm_conv_w=full["ssm_conv_w"][l], ssm_conv_b=ssm_conv_b[l][None],
            ssm_dtb=_lane_row(ssm_dt_bias[l], 0), ssm_alog=_lane_row(ssm_a_log[l], 0), ssm_dsk=_lane_row(ssm_d[l], 0),
            ssm_norm=ssm_norm[l][None], ssm_w_branch=full["ssm_w_branch"][l], dn_conv_w=full["dn_conv_w"][l],
            dn_dtb=_lane_row(dn_dt_bias[l], hs + hd), dn_alog=_lane_row(dn_a_log[l], hs + hd), dn_norm=dn_norm[l][None],
            dn_w_branch=full["dn_w_branch"][l], w_out=full["w_out"][l],
            ffn2_norm=ffn2_norm[l][None], ffn2_w_in=full["ffn2_w_in"][l], ffn2_w_out=full["ffn2_w_out"][l]))

    h = x.reshape(s, d)
    saved = []
    for l, p in enumerate(layers):
        h, r1 = _ffn_fwd(h, p["ffn1_norm"], p["ffn1_w_in"], p["ffn1_w_out"], cfg, f"l{l}_ffn1")
        h, rm = _mix_fwd(h, p, cfg, f"l{l}_mix")
        h, r2 = _ffn_fwd(h, p["ffn2_norm"], p["ffn2_w_in"], p["ffn2_w_out"], cfg, f"l{l}_ffn2")
        saved.append((r1, rm, r2))
    loss_blk, gh, g_final = _loss_head(h, loss_target.reshape(s, d), final_norm[None], "loss_head")
    loss = lax.psum(loss_blk[0, 0], ("x", "y", "c"))

    lg = [None] * depth
    for l in reversed(range(depth)):
        p, (r1, rm, r2) = layers[l], saved[l]
        gh, (g_n2, g_win2, g_wout2) = _ffn_bwd(gh, r2, p["ffn2_norm"], p["ffn2_w_in"], p["ffn2_w_out"], cfg, f"l{l}_ffn2")
        gh, gm = _mix_bwd(gh, rm, p, cfg, f"l{l}_mix")
        gh, (g_n1, g_win1, g_wout1) = _ffn_bwd(gh, r1, p["ffn1_norm"], p["ffn1_w_in"], p["ffn1_w_out"], cfg, f"l{l}_ffn1")
        gm["w_in"] = _unpermute_w_in(gm["w_in"], cfg)
        gm.update(ffn1_norm=g_n1[0], ffn1_w_in=g_win1, ffn1_w_out=g_wout1, ffn2_norm=g_n2[0], ffn2_w_in=g_win2,
                  ffn2_w_out=g_wout2, mix_norm=gm["mix_norm"][0], ssm_norm=gm["ssm_norm"][0], dn_norm=gm["dn_norm"][0])
        lg[l] = gm
    grad_x = gh.reshape(x.shape)
    local = {n: jnp.stack([lg[l][n] for l in range(depth)]) for n in _NAMES if n != "final_norm"}
    local["final_norm"] = g_final[0]

    shares = [_scatter_layout(local[n], n in _COL_SHARDED, lax.axis_index("c")) for n in _BIG]
    from_sibling = _sibling_swap(shares, "swap_grads")
    chip_sums = [_pair_sum(a, b, f"pair_sum_{n}") for n, a, b in zip(_BIG, shares, from_sibling)]
    parts = _chip_scatter(chip_sums, "scatter_grads")
    out_g, out_d, out_m, out_v = {}, {}, {}, {}
    for n, pt in zip(_BIG, parts):
        out_g[n], out_d[n], out_m[n], out_v[n] = _adamw_sum(pt, w[n], mom[n], var[n], f"adamw_{n}")

    small = [n for n in _NAMES if n not in _BIG]
    packed, = _gather_all([_pack([local[n] for n in small])], "gather_small_grads")
    total = _unpack(_sum_parts(packed, "sum_small_grads"), [local[n].shape for n in small])
    for n, g in zip(small, total):
        if n in _CONV:
            c = w[n].shape[2]
            g = lax.dynamic_slice_in_dim(g, me * c, c, axis=2)
        out_g[n] = g
    shapes = [w[n].shape for n in small]
    upd = _adamw_flat(_pack([out_g[n] for n in small]), _pack([w[n] for n in small]), _pack([mom[n] for n in small]),
                      _pack([var[n] for n in small]), "adamw_small")
    for dst, pk in zip((out_d, out_m, out_v), upd):
        for n, a in zip(small, _unpack(pk, shapes)):
            dst[n] = a

    return (loss, grad_x, *[out_g[n] for n in _NAMES], *[out_d[n] for n in _NAMES], *[out_m[n] for n in _NAMES],
            *[out_v[n] for n in _NAMES])
```

```python
import functools
import math

import jax
import jax.numpy as jnp
from jax import lax
from jax.experimental import pallas as pl
from jax.experimental.pallas import tpu as pltpu

F32, BF16 = jnp.float32, jnp.bfloat16
MESH = pl.DeviceIdType.MESH

N_DEV = 8
N_CHIP = 4
EPS = 1e-6
CONV_K = 4
SSM_GROUPS = 4
SSM_CHUNK = 128
DN_CHUNK = 64
ADAM_LR, ADAM_B1, ADAM_B2, ADAM_EPS, ADAM_WD, ADAM_STEP = 0.001, 0.9, 0.999, 1e-08, 0.01, 10

V7X_VMEM_BYTES = 64 * 1024 * 1024
VMEM_LIMIT = 52 * 1024 * 1024
MATMUL_VMEM_BUDGET = 44 * 1024 * 1024
LANES = 128
SMALL_W = 256
SMALL_R = 128
CONV_HALO = 8
HALO_BLK = 16
CONV_STRIP = 32


def _tile(n, target, quantum):
    if n <= target:
        return n
    t = (target // quantum) * quantum
    while t >= quantum:
        if n % t == 0:
            return t
        t -= quantum
    return n


def _cp(*sem):
    return pltpu.CompilerParams(dimension_semantics=sem, vmem_limit_bytes=VMEM_LIMIT)


def _softplus(x):
    return jnp.maximum(x, 0.0) + jnp.log1p(jnp.exp(-jnp.abs(x)))


def _silu(x):
    return x * jax.nn.sigmoid(x)


def _bdot(a, b, dims):
    return lax.dot_general(a.astype(BF16), b.astype(BF16), dims, preferred_element_type=F32)


_NN = (((1,), (0,)), ((), ()))
_NT = (((1,), (1,)), ((), ()))
_TN = (((0,), (0,)), ((), ()))
_MM_DIMS = {"nn": _NN, "nt": _NT, "tn": _TN}


def _mm3(spec, a, b):
    ah, bh = a.astype(BF16), b.astype(BF16)
    al, bl = (a - ah.astype(F32)).astype(BF16), (b - bh.astype(F32)).astype(BF16)
    e = lambda x, y: jnp.einsum(spec, x, y, preferred_element_type=F32)
    return e(ah, bh) + (e(ah, bl) + e(al, bh))


@jax.custom_vjp
def _pmm(a, b):
    return _mm3("bij,bjk->bik", a, b)


def _pmm_fwd(a, b):
    return _pmm(a, b), (a, b)


def _pmm_bwd(res, g):
    a, b = res
    return _mm3("bik,bjk->bij", g, b), _mm3("bji,bjk->bik", a, g)


_pmm.defvjp(_pmm_fwd, _pmm_bwd)


@jax.custom_vjp
def _neumann_inv(n):
    c = n.shape[-1]
    r = lax.broadcasted_iota(jnp.int32, (c, c), 0)
    cc = lax.broadcasted_iota(jnp.int32, (c, c), 1)
    t, pw = (r == cc).astype(F32)[None] + n, n
    for _ in range(int(math.log2(c)) - 1):
        pw = _mm3("bij,bjk->bik", pw, pw)
        t = t + _mm3("bij,bjk->bik", t, pw)
    return t


def _neumann_fwd(n):
    t = _neumann_inv(n)
    return t, t


def _neumann_bwd(t, g):
    return (_mm3("bik,bjk->bij", _mm3("bji,bjk->bik", t, g), t),)


_neumann_inv.defvjp(_neumann_fwd, _neumann_bwd)


def _matmul(a, b, *, mode, name, out_dtype=F32, res=None, scale=1.0):
    if mode == "nn":
        (m, k), (k2, n) = a.shape, b.shape
    elif mode == "nt":
        (m, k), (n, k2) = a.shape, b.shape
    else:
        (k, m), (k2, n) = a.shape, b.shape
    assert k == k2, (name, a.shape, b.shape)
    has_res = res is not None
    tn = _tile(n, 1408, LANES)
    shapes = ((512, 4096), (1024, 2048), (1024, 1024), (512, 512)) if mode == "tn" else (
        (2048, 2816), (1024, 2816), (1024, 2048), (1024, 1408), (1024, 1024), (512, 512))
    for rows, depth in shapes:
        tm, tk = _tile(m, rows, LANES), _tile(k, depth, LANES)
        need = 2 * tk * (tm * a.dtype.itemsize + tn * b.dtype.itemsize) + tm * tn * (4 + 2 * jnp.dtype(out_dtype).itemsize)
        need += 2 * tm * tn * res.dtype.itemsize if has_res else 0
        if need <= MATMUL_VMEM_BUDGET:
            break
    nk = k // tk
    dims = _MM_DIMS[mode]
    a_spec = pl.BlockSpec((tk, tm), lambda i, j, q: (q, i)) if mode == "tn" else pl.BlockSpec((tm, tk), lambda i, j, q: (i, q))
    b_spec = pl.BlockSpec((tn, tk), lambda i, j, q: (j, q)) if mode == "nt" else pl.BlockSpec((tk, tn), lambda i, j, q: (q, j))
    o_spec = pl.BlockSpec((tm, tn), lambda i, j, q: (i, j))

    def body(*refs):
        a_ref, b_ref = refs[0], refs[1]
        res_ref = refs[2] if has_res else None
        o_ref = refs[3 if has_res else 2]
        acc_ref = refs[-1] if nk > 1 else None

        def finish(acc):
            val = acc * scale if scale != 1.0 else acc
            if has_res:
                val = res_ref[...].astype(F32) + val
            o_ref[...] = val.astype(o_ref.dtype)

        if nk == 1:
            finish(_bdot(a_ref[...], b_ref[...], dims))
        else:
            q = pl.program_id(2)

            @pl.when(q == 0)
            def _():
                acc_ref[...] = jnp.zeros(acc_ref.shape, F32)

            acc_ref[...] += _bdot(a_ref[...], b_ref[...], dims)

            @pl.when(q == nk - 1)
            def _():
                finish(acc_ref[...])

    ins = [a, b] + ([res] if has_res else [])
    in_specs = [a_spec, b_spec] + ([o_spec] if has_res else [])
    return pl.pallas_call(
        body, name=name, grid=(m // tm, n // tn, nk), in_specs=in_specs, out_specs=o_spec,
        out_shape=jax.ShapeDtypeStruct((m, n), out_dtype),
        scratch_shapes=[pltpu.VMEM((tm, tn), F32)] if nk > 1 else [],
        compiler_params=_cp("parallel", "parallel", "arbitrary"),
    )(*ins)


def _read(ref, split, rows):
    rows = slice(None) if ref.shape[0] == 1 else rows
    if split is None:
        return [ref[rows, :].astype(F32)]
    out, off = [], 0
    for w in split:
        out.append(ref[rows, off:off + w].astype(F32))
        off += w
    return out


def _write(ref, vals, split, rows):
    if split is None:
        ref[rows, :] = vals[0].astype(ref.dtype)
        return
    off = 0
    for w, v in zip(split, vals):
        ref[rows, off:off + w] = v.astype(ref.dtype)
        off += w


def _rw_fwd(fn, name, grid, ins, outs):
    n_in = len(ins)

    def body(*refs):
        rows = slice(None)
        args = []
        for r, (_, _, split) in zip(refs[:n_in], ins):
            args += _read(r, split, rows)
        vals = list(fn(*args))
        for r, (_, _, split) in zip(refs[n_in:], outs):
            n = 1 if split is None else len(split)
            _write(r, vals[:n], split, rows)
            vals = vals[n:]

    return pl.pallas_call(
        body, name=name, grid=grid, in_specs=[s for _, s, _ in ins], out_specs=[s for _, s, _ in outs],
        out_shape=[o for o, _, _ in outs], compiler_params=_cp("parallel", "parallel"),
    )(*[a for a, _, _ in ins])


def _rw_bwd(fn, name, grid, ins, cts, grads, add=None, alias=None):
    n_in, n_ct = len(ins), len(cts)
    n_fixed = n_in + n_ct + (1 if add is not None else 0) + (1 if alias is not None else 0)
    arg_pos, pos = [], 0
    for _, _, split in ins:
        n = 1 if split is None else len(split)
        arg_pos.append((pos, n))
        pos += n

    def body(*refs):
        out_refs = refs[n_fixed:]
        rows = slice(None)
        wrt = []
        for idx, _, _, _ in grads:
            p, n = arg_pos[idx]
            wrt += list(range(p, p + n))
        args = []
        for r, (_, _, split) in zip(refs[:n_in], ins):
            args += _read(r, split, rows)
        ct_vals = []
        for r, (_, _, split) in zip(refs[n_in:n_in + n_ct], cts):
            ct_vals += _read(r, split, rows)

        def f(*w):
            full = list(args)
            for p, v in zip(wrt, w):
                full[p] = v
            return tuple(fn(*full))

        _, vjp = jax.vjp(f, *[args[p] for p in wrt])
        g = list(vjp(tuple(ct_vals)))
        first_row = pl.program_id(1) == 0
        first_all = jnp.logical_and(pl.program_id(0) == 0, first_row)
        for gi, (idx, _, _, mode) in enumerate(grads):
            n = arg_pos[idx][1]
            vals, g = g[:n], g[n:]
            o = out_refs[gi]
            if mode == "tile":
                if add is not None and add[2] == gi:
                    vals = [vals[0] + refs[n_in + n_ct][...].astype(F32)]
                _write(o, vals, ins[idx][2], rows)
            else:
                first = first_row if mode == "acc_row" else first_all

                @pl.when(first)
                def _(o=o, val=vals[0]):
                    o[...] = val

                @pl.when(jnp.logical_not(first))
                def _(o=o, val=vals[0]):
                    o[...] += val

    arrays = [a for a, _, _ in ins] + [a for a, _, _ in cts]
    in_specs = [s for _, s, _ in ins] + [s for _, s, _ in cts]
    if add is not None:
        arrays.append(add[0])
        in_specs.append(add[1])
    aliases = {}
    if alias is not None:
        aliases = {len(arrays): alias[1]}
        arrays.append(alias[0])
        in_specs.append(pl.BlockSpec(memory_space=pl.ANY))
    return pl.pallas_call(
        body, name=name, grid=grid, in_specs=in_specs, out_specs=[s for _, _, s, _ in grads],
        out_shape=[o for _, o, _, _ in grads], input_output_aliases=aliases,
        compiler_params=_cp("arbitrary", "arbitrary"),
    )(*arrays)


def _rows(t, w, col=0):
    if callable(col):
        return pl.BlockSpec((t, w), lambda j, i: (i, col(j)))
    return pl.BlockSpec((t, w), lambda j, i: (i, col))


def _par(w, per_col=False):
    return pl.BlockSpec((1, w), (lambda j, i: (0, j)) if per_col else (lambda j, i: (0, 0)))


def _rms_fn(x, w):
    return (x * lax.rsqrt(jnp.mean(x * x, axis=-1, keepdims=True) + EPS) * w,)


def _swiglu_fn(gate, up):
    return (_silu(gate) * up,)


def _ssm_out_fn(y, z, w):
    yg = y * _silu(z)
    return (yg * lax.rsqrt(jnp.mean(yg * yg, axis=-1, keepdims=True) + EPS) * w,)


def _dn_out_fn(o, z, w):
    return (o * lax.rsqrt(jnp.mean(o * o, axis=-1, keepdims=True) + EPS) * w * _silu(z),)


def _merge_fn(gs, gd, ps, pd):
    return (jax.nn.sigmoid(gs) * ps + jax.nn.sigmoid(gd) * pd,)


def _dsmall(a, b, name):
    s = a.shape[0]
    t = _tile(s, 1024, 16)

    def body(a_r, b_r, o_r):
        o_r[:, :SMALL_R] = (a_r[...] + b_r[...]).astype(o_r.dtype)
        o_r[:, SMALL_R:] = jnp.zeros((t, SMALL_W - SMALL_R), o_r.dtype)

    row = pl.BlockSpec((t, SMALL_R), lambda i: (i, 0))
    return pl.pallas_call(
        body, name=name, grid=(s // t,), in_specs=[row, row], out_specs=pl.BlockSpec((t, SMALL_W), lambda i: (i, 0)),
        out_shape=jax.ShapeDtypeStruct((s, SMALL_W), BF16), compiler_params=_cp("parallel"),
    )(a, b)


def _conv_fwd(x, x_col0, w, b, *, name, cw, tr):
    s, c = x.shape[0], w.shape[1]
    nr, ncol, hb = s // tr, c // cw, tr // HALO_BLK
    rs, lo = CONV_STRIP, CONV_HALO - (CONV_K - 1)

    def body(x_ref, prev_ref, w_ref, b_ref, o_ref, buf):
        i = pl.program_id(1)
        buf[0:CONV_HALO, :] = jnp.where(i > 0, prev_ref[HALO_BLK - CONV_HALO:, :].astype(F32), 0.0)
        buf[CONV_HALO:, :] = x_ref[...].astype(F32)
        taps = [w_ref[q:q + 1, :] for q in range(CONV_K)]
        bias = b_ref[...]

        def strip(k, carry):
            r0 = pl.multiple_of(k * rs, rs)
            ext = buf[pl.ds(r0, rs + CONV_HALO), :]
            acc = bias + taps[0] * ext[lo:lo + rs]
            for q in range(1, CONV_K):
                acc = acc + taps[q] * ext[lo + q:lo + q + rs]
            o_ref[pl.ds(r0, rs), :] = _silu(acc).astype(o_ref.dtype)
            return carry

        lax.fori_loop(0, tr // rs, strip, 0)

    return pl.pallas_call(
        body, name=name, grid=(ncol, nr),
        in_specs=[pl.BlockSpec((tr, cw), lambda j, i: (i, x_col0 + j)),
                  pl.BlockSpec((HALO_BLK, cw), lambda j, i: (jnp.maximum(i * hb - 1, 0), x_col0 + j)),
                  pl.BlockSpec((CONV_K, cw), lambda j, i: (0, j)), pl.BlockSpec((1, cw), lambda j, i: (0, j))],
        out_specs=pl.BlockSpec((tr, cw), lambda j, i: (i, j)),
        out_shape=jax.ShapeDtypeStruct((s, c), BF16),
        scratch_shapes=[pltpu.VMEM((CONV_HALO + tr, cw), F32)],
        compiler_params=_cp("parallel", "parallel"),
    )(x, x, w, b)


def _conv_bwd(x, x_col0, w, b, dy, dproj, out_col0, *, name, cw, tr):
    s, c = dy.shape
    nr, ncol = s // tr, c // cw
    hb, last_hb = tr // HALO_BLK, s // HALO_BLK - 1
    hb8, last_hb8 = tr // CONV_HALO, s // CONV_HALO - 1
    ext = tr + CONV_HALO
    rs, lo = CONV_STRIP, CONV_HALO - (CONV_K - 1)
    fresh = isinstance(dproj, jax.ShapeDtypeStruct)

    def body(x_ref, prev_ref, next_ref, dy_ref, dyn_ref, w_ref, b_ref, *rest):
        dx_ref, dw_ref, db_ref, xbuf, gbuf = rest[-5:]
        i = pl.program_id(1)
        xbuf[0:CONV_HALO, :] = jnp.where(i > 0, prev_ref[HALO_BLK - CONV_HALO:, :].astype(F32), 0.0)
        xbuf[CONV_HALO:CONV_HALO + tr, :] = x_ref[...].astype(F32)
        xbuf[CONV_HALO + tr:, :] = next_ref[0:CONV_HALO, :].astype(F32)
        taps = [w_ref[q:q + 1, :] for q in range(CONV_K)]
        bias = b_ref[...]

        def dpre(xe, dy, n):
            pre = bias + taps[0] * xe[lo:lo + n]
            for q in range(1, CONV_K):
                pre = pre + taps[q] * xe[lo + q:lo + q + n]
            sg = jax.nn.sigmoid(pre)
            return dy * (sg * (1.0 + pre * (1.0 - sg)))

        def strip1(k, carry):
            r0 = pl.multiple_of(k * rs, rs)
            gbuf[pl.ds(r0, rs), :] = dpre(xbuf[pl.ds(r0, rs + CONV_HALO), :], dy_ref[pl.ds(r0, rs), :].astype(F32), rs)
            return carry

        lax.fori_loop(0, tr // rs, strip1, 0)
        gbuf[tr:, :] = dpre(xbuf[tr:, :], jnp.where(i < nr - 1, dyn_ref[...].astype(F32), 0.0), CONV_HALO)

        def fold(v):
            acc = v[0:8]
            for a in range(1, rs // 8):
                acc = acc + v[8 * a:8 * a + 8]
            return acc

        def strip2(k, carry):
            r0 = pl.multiple_of(k * rs, rs)
            ge = gbuf[pl.ds(r0, rs + CONV_HALO), :]
            x_own = xbuf[pl.ds(r0 + CONV_HALO, rs), :]
            dx = jnp.zeros((rs, cw), F32)
            new = []
            for q in range(CONV_K):
                g_q = ge[CONV_K - 1 - q:CONV_K - 1 - q + rs]
                dx = dx + taps[q] * g_q
                new.append(carry[q] + fold(x_own * g_q))
            dx_ref[pl.ds(r0, rs), :] = dx.astype(dx_ref.dtype)
            return tuple(new) + (carry[CONV_K] + fold(ge[0:rs]),)

        sums = lax.fori_loop(0, tr // rs, strip2, tuple(jnp.zeros((8, cw), F32) for _ in range(CONV_K + 1)))
        dws = [jnp.sum(sums[q], axis=0, keepdims=True) for q in range(CONV_K)]
        dbv = jnp.sum(sums[CONV_K], axis=0, keepdims=True)

        @pl.when(i == 0)
        def _():
            for q in range(CONV_K):
                dw_ref[q:q + 1, :] = dws[q]
            db_ref[...] = dbv

        @pl.when(i > 0)
        def _():
            for q in range(CONV_K):
                dw_ref[q:q + 1, :] += dws[q]
            db_ref[...] += dbv

    xmap = lambda j, i: (i, x_col0 + j)
    ins = [x, x, x, dy, dy, w, b]
    in_specs = [pl.BlockSpec((tr, cw), xmap),
                pl.BlockSpec((HALO_BLK, cw), lambda j, i: (jnp.maximum(i * hb - 1, 0), x_col0 + j)),
                pl.BlockSpec((HALO_BLK, cw), lambda j, i: (jnp.minimum((i + 1) * hb, last_hb), x_col0 + j)),
                pl.BlockSpec((tr, cw), lambda j, i: (i, j)),
                pl.BlockSpec((CONV_HALO, cw), lambda j, i: (jnp.minimum((i + 1) * hb8, last_hb8), j)),
                pl.BlockSpec((CONV_K, cw), lambda j, i: (0, j)), pl.BlockSpec((1, cw), lambda j, i: (0, j))]
    aliases = {}
    if not fresh:
        aliases = {len(ins): 0}
        ins.append(dproj)
        in_specs.append(pl.BlockSpec(memory_space=pl.ANY))
    return pl.pallas_call(
        body, name=name, grid=(ncol, nr), in_specs=in_specs,
        out_specs=[pl.BlockSpec((tr, cw), lambda j, i: (i, out_col0 + j)),
                   pl.BlockSpec((CONV_K, cw), lambda j, i: (0, j)), pl.BlockSpec((1, cw), lambda j, i: (0, j))],
        out_shape=[jax.ShapeDtypeStruct(dproj.shape, dproj.dtype), jax.ShapeDtypeStruct((CONV_K, c), F32),
                   jax.ShapeDtypeStruct((1, c), F32)],
        scratch_shapes=[pltpu.VMEM((CONV_HALO + ext, cw), F32), pltpu.VMEM((ext, cw), F32)],
        input_output_aliases=aliases, compiler_params=_cp("arbitrary", "arbitrary"),
    )(*ins)


def _ssd_gates(small, dtb, alog):
    l = small.shape[0]
    r = lax.broadcasted_iota(jnp.int32, (l, l), 0)
    c = lax.broadcasted_iota(jnp.int32, (l, l), 1)
    dt_all = _softplus(small + dtb)
    return dt_all, _pmm((r >= c).astype(F32)[None], (dt_all * (-jnp.exp(alog)))[None])[0]


def _ssd_chunk(xs, bm, cm, dt_all, acum_all, dsk, st, g, *, hg, p):
    l, w = dt_all.shape
    per = LANES // p
    lane_w = lax.broadcasted_iota(jnp.int32, (1, w), 1)
    lane = lax.broadcasted_iota(jnp.int32, (1, LANES), 1)
    r = lax.broadcasted_iota(jnp.int32, (l, l), 0)
    c = lax.broadcasted_iota(jnp.int32, (l, l), 1)
    tri = r >= c
    eye = (r == c).astype(F32)
    last = (lax.broadcasted_iota(jnp.int32, (l, 1), 0) == l - 1).astype(F32)
    cb = _bdot(cm, bm, _NT)
    ys, sts = [], []
    for q, (x_q, st_q) in enumerate(zip(xs, st)):
        dt_e = jnp.zeros((l, LANES), F32)
        acum_e = jnp.zeros((l, LANES), F32)
        d_e = jnp.zeros((1, LANES), F32)
        decays, masks = [], []
        for jj in range(per):
            mh = (lane_w == g * hg + q * per + jj).astype(F32)
            mj = jnp.logical_and(lane >= jj * p, lane < (jj + 1) * p).astype(F32)
            ac = jnp.sum(acum_all * mh, axis=1, keepdims=True)
            dt_e = dt_e + jnp.sum(dt_all * mh, axis=1, keepdims=True) * mj
            acum_e = acum_e + ac * mj
            d_e = d_e + jnp.sum(dsk * mh, axis=1, keepdims=True) * mj
            seg = ac - jnp.sum(ac * eye, axis=0, keepdims=True)
            decays.append(jnp.where(tri, jnp.exp(jnp.where(tri, seg, 0.0)), 0.0))
            masks.append(mj)
        xdt = x_q * dt_e
        y = x_q * d_e + _bdot(jnp.concatenate([cb * dec for dec in decays], axis=1),
                              jnp.concatenate([xdt * mj for mj in masks], axis=0), _NN)
        a_last = jnp.sum(acum_e * last, axis=0, keepdims=True)
        ys.append(y + jnp.exp(acum_e) * _bdot(cm, st_q, _NN))
        sts.append(st_q * jnp.exp(a_last) + _bdot(bm, xdt * jnp.exp(a_last - acum_e), _TN))
    return tuple(ys) + tuple(sts)


def _pieces(ref, lead=()):
    return [ref[lead + (slice(None), slice(q * LANES, (q + 1) * LANES))].astype(F32) for q in range(ref.shape[-1] // LANES)]


def _ssd_specs(cfg, rev):
    l, hp, n, g = SSM_CHUNK, cfg["hp"], cfg["n"], SSM_GROUPS
    nc = cfg["s"] // l
    cc = (lambda c: nc - 1 - c) if rev else (lambda c: c)
    nb = cfg["inner"] // n
    par = pl.BlockSpec((1, SMALL_R), lambda c, q: (0, 0))
    specs = [pl.BlockSpec((l, hp), lambda c, q: (cc(c), q)),
             pl.BlockSpec((l, n), lambda c, q: (cc(c), nb + q)),
             pl.BlockSpec((l, n), lambda c, q: (cc(c), nb + g + q)),
             pl.BlockSpec((l, SMALL_R), lambda c, q: (cc(c), 0)), par, par, par]
    st_spec = pl.BlockSpec((None, None, n, hp), lambda c, q: (cc(c), q, 0, 0))
    y_spec = pl.BlockSpec((l, hp), lambda c, q: (cc(c), q))
    return specs, st_spec, y_spec, nc


def _ssd_fwd(xbc_c, proj, dtb, alog, dsk, cfg, name):
    specs, st_spec, y_spec, nc = _ssd_specs(cfg, False)
    fn = functools.partial(_ssd_chunk, hg=cfg["hg"], p=cfg["p"])
    npc = cfg["hp"] // LANES

    def body(xs, bm, cm, sm, dtb_r, alog_r, dsk_r, y_ref, sts_ref, st, dt_s, ac_s):
        c, g = pl.program_id(0), pl.program_id(1)

        @pl.when(c == 0)
        def _():
            st[g] = jnp.zeros(st.shape[1:], F32)

        @pl.when(g == 0)
        def _():
            dt_s[...], ac_s[...] = _ssd_gates(sm[...], dtb_r[...], alog_r[...])

        sts_ref[...] = st[g]
        out = fn(_pieces(xs), bm[...].astype(F32), cm[...].astype(F32), dt_s[...], ac_s[...], dsk_r[...],
                 _pieces(st, (g,)), g)
        for q in range(npc):
            y_ref[:, q * LANES:(q + 1) * LANES] = out[q]
            st[g, :, q * LANES:(q + 1) * LANES] = out[npc + q]

    return pl.pallas_call(
        body, name=name, grid=(nc, SSM_GROUPS), in_specs=specs, out_specs=[y_spec, st_spec],
        out_shape=[jax.ShapeDtypeStruct((cfg["s"], cfg["inner"]), F32),
                   jax.ShapeDtypeStruct((nc, SSM_GROUPS, cfg["n"], cfg["hp"]), F32)],
        scratch_shapes=[pltpu.VMEM((SSM_GROUPS, cfg["n"], cfg["hp"]), F32)] + [pltpu.VMEM((SSM_CHUNK, SMALL_R), F32)] * 2,
        compiler_params=_cp("arbitrary", "arbitrary"),
    )(xbc_c, xbc_c, xbc_c, proj, dtb, alog, dsk)


def _ssd_bwd(xbc_c, proj, dtb, alog, dsk, states, dy, cfg, name):
    specs, st_spec, y_spec, nc = _ssd_specs(cfg, True)
    l, n, gn = SSM_CHUNK, cfg["n"], SSM_GROUPS * cfg["n"]
    fn = functools.partial(_ssd_chunk, hg=cfg["hg"], p=cfg["p"])
    npc = cfg["hp"] // LANES
    rc = lambda c: nc - 1 - c

    def body(xs, bm, cm, sm, dtb_r, alog_r, dsk_r, sts_ref, dy_ref, dxs, dbm, dcm, dsm, ddtb, dalog, ddsk, dst,
             dt_s, ac_s, gdt_s, gac_s):
        c, g = pl.program_id(0), pl.program_id(1)

        @pl.when(c == 0)
        def _():
            dst[g] = jnp.zeros(dst.shape[1:], F32)

        @pl.when(g == 0)
        def _():
            dt_s[...], ac_s[...] = _ssd_gates(sm[...], dtb_r[...], alog_r[...])
            gdt_s[...] = jnp.zeros(gdt_s.shape, F32)
            gac_s[...] = jnp.zeros(gac_s.shape, F32)

        def f(*a):
            return fn(a[:npc], *a[npc:npc + 5], a[npc + 5:], g)

        _, vjp = jax.vjp(f, *_pieces(xs), bm[...].astype(F32), cm[...].astype(F32), dt_s[...], ac_s[...], dsk_r[...],
                         *_pieces(sts_ref))
        grads = vjp(tuple(_pieces(dy_ref)) + tuple(_pieces(dst, (g,))))
        gb, gc, gdt, gac, g3 = grads[npc:npc + 5]
        for q in range(npc):
            dxs[:, q * LANES:(q + 1) * LANES] = grads[q]
            dst[g, :, q * LANES:(q + 1) * LANES] = grads[npc + 5 + q]
        dbm[...] = gb
        dcm[...] = gc
        gdt_s[...] += gdt
        gac_s[...] += gac
        first = jnp.logical_and(c == 0, g == 0)

        @pl.when(first)
        def _():
            ddsk[...] = g3

        @pl.when(jnp.logical_not(first))
        def _():
            ddsk[...] += g3

        @pl.when(g == SSM_GROUPS - 1)
        def _():
            _, vjp_gates = jax.vjp(_ssd_gates, sm[...], dtb_r[...], alog_r[...])
            gs, g1, g2 = vjp_gates((gdt_s[...], gac_s[...]))
            dsm[...] = gs

            @pl.when(c == 0)
            def _():
                ddtb[...] = g1
                dalog[...] = g2

            @pl.when(c > 0)
            def _():
                ddtb[...] += g1
                dalog[...] += g2

    par = pl.BlockSpec((1, SMALL_R), lambda c, q: (0, 0))
    return pl.pallas_call(
        body, name=name, grid=(nc, SSM_GROUPS), in_specs=specs + [st_spec, y_spec],
        out_specs=[y_spec, pl.BlockSpec((l, n), lambda c, q: (rc(c), q)), pl.BlockSpec((l, n), lambda c, q: (rc(c), q)),
                   pl.BlockSpec((l, SMALL_R), lambda c, q: (rc(c), 0)), par, par, par],
        out_shape=[jax.ShapeDtypeStruct((cfg["s"], cfg["inner"]), F32), jax.ShapeDtypeStruct((cfg["s"], gn), F32),
                   jax.ShapeDtypeStruct((cfg["s"], gn), F32), jax.ShapeDtypeStruct((cfg["s"], SMALL_R), F32),
                   jax.ShapeDtypeStruct((1, SMALL_R), F32), jax.ShapeDtypeStruct((1, SMALL_R), F32),
                   jax.ShapeDtypeStruct((1, SMALL_R), F32)],
        scratch_shapes=[pltpu.VMEM((SSM_GROUPS, cfg["n"], cfg["hp"]), F32)] + [pltpu.VMEM((SSM_CHUNK, SMALL_R), F32)] * 4,
        compiler_params=_cp("arbitrary", "arbitrary"),
    )(xbc_c, xbc_c, xbc_c, proj, dtb, alog, dsk, states, dy)


def _gdn_gates(small, alog, dtb):
    c = DN_CHUNK
    rr, w_ = small.shape
    nb = rr // c
    r = lax.broadcasted_iota(jnp.int32, (c, c), 0)
    cc = lax.broadcasted_iota(jnp.int32, (c, c), 1)
    g_all = -jnp.exp(alog) * _softplus(small + dtb)
    tri_b = jnp.broadcast_to((r >= cc).astype(F32)[None], (nb, c, c))
    return _pmm(tri_b, g_all.reshape(nb, c, w_)), jax.nn.sigmoid(small)


def _gdn_prep(q, k, v, gcum_all, beta_all, h, *, boff, aoff):
    c = DN_CHUNK
    rr, dk = q.shape
    nb, w_ = rr // c, beta_all.shape[1]
    lane = lax.broadcasted_iota(jnp.int32, (1, w_), 1)
    r = lax.broadcasted_iota(jnp.int32, (c, c), 0)
    cc = lax.broadcasted_iota(jnp.int32, (c, c), 1)
    incl, strict = (r >= cc)[None], (r > cc)[None]
    eye = (r == cc).astype(F32)[None]
    gc = jnp.sum(gcum_all * (lane == aoff + h).astype(F32)[None], axis=2, keepdims=True)
    beta = jnp.sum(beta_all * (lane == boff + h).astype(F32), axis=1, keepdims=True).reshape(nb, c, 1)
    qn = (q * lax.rsqrt(jnp.sum(q * q, axis=1, keepdims=True) + EPS) * (dk ** -0.5)).reshape(nb, c, dk)
    kn = (k * lax.rsqrt(jnp.sum(k * k, axis=1, keepdims=True) + EPS)).reshape(nb, c, dk)
    v3 = v.reshape(nb, c, v.shape[1])
    g_row = jnp.sum(gc * eye, axis=1, keepdims=True)
    decay = jnp.where(incl, jnp.exp(jnp.where(incl, gc - g_row, 0.0)), 0.0)
    kb = kn * beta
    kk = jnp.einsum("bik,bjk->bij", kb.astype(BF16), kn.astype(BF16), preferred_element_type=F32)
    neg_m = jnp.where(strict, -kk * decay, 0.0)
    t_inv = _neumann_inv(neg_m)
    eg = jnp.exp(gc)
    u = _pmm(t_inv, v3 * beta)
    w = _pmm(t_inv, kb * eg)
    a_qk = jnp.einsum("bik,bjk->bij", qn.astype(BF16), kn.astype(BF16), preferred_element_type=F32) * decay
    last = (lax.broadcasted_iota(jnp.int32, (1, c, 1), 1) == c - 1).astype(F32)
    g_last = jnp.sum(gc * last, axis=1, keepdims=True)
    q_dec = qn * eg
    k_dec = kn * jnp.exp(g_last - gc)
    gl = jnp.broadcast_to(g_last, (nb, 1, LANES))
    return (u.reshape(rr, -1), w.reshape(rr, dk), q_dec.reshape(rr, dk), k_dec.reshape(rr, dk), a_qk, gl)


def _gdn_prep_specs(cfg):
    rr, hd, dk, dv = cfg["prep_rows"], cfg["hd"], cfg["dk"], cfg["dv"]
    nb = rr // DN_CHUNK
    ins = [pl.BlockSpec((rr, dk), lambda i, h: (i, h)), pl.BlockSpec((rr, dk), lambda i, h: (i, hd + h)),
           pl.BlockSpec((rr, dv), lambda i, h: (i, (2 * hd * dk) // dv + h)),
           pl.BlockSpec((rr, SMALL_R), lambda i, h: (i, 0)),
           pl.BlockSpec((1, SMALL_R), lambda i, h: (0, 0)), pl.BlockSpec((1, SMALL_R), lambda i, h: (0, 0))]
    hs = lambda d: pl.BlockSpec((None, rr, d), lambda i, h: (h, i, 0))
    outs = [hs(dv), hs(dk), hs(dk), hs(dk), pl.BlockSpec((None, nb, DN_CHUNK, DN_CHUNK), lambda i, h: (h, i, 0, 0)),
            pl.BlockSpec((None, nb, 1, LANES), lambda i, h: (h, i, 0, 0))]
    s, nch = cfg["s"], cfg["s"] // DN_CHUNK
    shapes = [jax.ShapeDtypeStruct((hd, s, dv), F32)] + [jax.ShapeDtypeStruct((hd, s, dk), F32)] * 3 + [
        jax.ShapeDtypeStruct((hd, nch, DN_CHUNK, DN_CHUNK), F32), jax.ShapeDtypeStruct((hd, nch, 1, LANES), F32)]
    return ins, outs, shapes


def _gdn_prep_fwd(qkv_c, proj, alog, dtb, cfg, name):
    ins, outs, shapes = _gdn_prep_specs(cfg)
    fn = functools.partial(_gdn_prep, boff=cfg["hs"], aoff=cfg["hs"] + cfg["hd"])

    rr = cfg["prep_rows"]

    def body(q, k, v, sm, al, db, *rest):
        o, (gc_s, be_s) = rest[:-2], rest[-2:]

        @pl.when(pl.program_id(1) == 0)
        def _():
            gc_s[...], be_s[...] = _gdn_gates(sm[...], al[...], db[...])

        vals = fn(q[...].astype(F32), k[...].astype(F32), v[...].astype(F32), gc_s[...], be_s[...], pl.program_id(1))
        for ref, val in zip(o, vals):
            ref[...] = val

    return pl.pallas_call(
        body, name=name, grid=(cfg["s"] // rr, cfg["hd"]), in_specs=ins, out_specs=outs, out_shape=shapes,
        scratch_shapes=[pltpu.VMEM((rr // DN_CHUNK, DN_CHUNK, SMALL_R), F32), pltpu.VMEM((rr, SMALL_R), F32)],
        compiler_params=_cp("parallel", "arbitrary"),
    )(qkv_c, qkv_c, qkv_c, proj, alog, dtb)


def _gdn_prep_bwd(qkv_c, proj, alog, dtb, cts, cfg, name):
    ins, outs, _ = _gdn_prep_specs(cfg)
    rr, hd, dk, dv, s = cfg["prep_rows"], cfg["hd"], cfg["dk"], cfg["dv"], cfg["s"]
    fn = functools.partial(_gdn_prep, boff=cfg["hs"], aoff=cfg["hs"] + cfg["hd"])

    def body(q, k, v, sm, al, db, c0, c1, c2, c3, c4, c5, dq, dkk, dvv, dsm, dal, ddb, gc_s, be_s, ggc_s, gbe_s):
        i, h = pl.program_id(0), pl.program_id(1)

        @pl.when(h == 0)
        def _():
            gc_s[...], be_s[...] = _gdn_gates(sm[...], al[...], db[...])
            ggc_s[...] = jnp.zeros(ggc_s.shape, F32)
            gbe_s[...] = jnp.zeros(gbe_s.shape, F32)

        f = lambda *a: fn(*a, h)
        _, vjp = jax.vjp(f, q[...].astype(F32), k[...].astype(F32), v[...].astype(F32), gc_s[...], be_s[...])
        gq, gk, gv, ggc, gbe = vjp((c0[...], c1[...], c2[...], c3[...], c4[...], c5[...]))
        dq[...] = gq
        dkk[...] = gk
        dvv[...] = gv
        ggc_s[...] += ggc
        gbe_s[...] += gbe

        @pl.when(h == hd - 1)
        def _():
            _, vjp_gates = jax.vjp(_gdn_gates, sm[...], al[...], db[...])
            gs, ga, gd = vjp_gates((ggc_s[...], gbe_s[...]))
            dsm[...] = gs

            @pl.when(i == 0)
            def _():
                dal[...] = ga
                ddb[...] = gd

            @pl.when(i > 0)
            def _():
                dal[...] += ga
                ddb[...] += gd

    par = pl.BlockSpec((1, SMALL_R), lambda i, h: (0, 0))
    return pl.pallas_call(
        body, name=name, grid=(s // rr, hd), in_specs=ins + outs,
        out_specs=[pl.BlockSpec((rr, dk), lambda i, h: (i, h)), pl.BlockSpec((rr, dk), lambda i, h: (i, h)),
                   pl.BlockSpec((rr, dv), lambda i, h: (i, h)), pl.BlockSpec((rr, SMALL_R), lambda i, h: (i, 0)), par, par],
        out_shape=[jax.ShapeDtypeStruct((s, hd * dk), F32), jax.ShapeDtypeStruct((s, hd * dk), F32),
                   jax.ShapeDtypeStruct((s, hd * dv), F32), jax.ShapeDtypeStruct((s, SMALL_R), F32),
                   jax.ShapeDtypeStruct((1, SMALL_R), F32), jax.ShapeDtypeStruct((1, SMALL_R), F32)],
        scratch_shapes=[pltpu.VMEM((rr // DN_CHUNK, DN_CHUNK, SMALL_R), F32), pltpu.VMEM((rr, SMALL_R), F32)] * 2,
        compiler_params=_cp("arbitrary", "arbitrary"),
    )(qkv_c, qkv_c, qkv_c, proj, alog, dtb, *cts)


def _gdn_scan_chunk(qd, aq, u, w, kd, gl, st):
    bm = lambda spec, a, b: jnp.einsum(spec, a.astype(BF16), b.astype(BF16), preferred_element_type=F32)
    v_new = u - bm("hck,hkv->hcv", w, st)
    o = bm("hck,hkv->hcv", qd, st) + bm("hij,hjv->hiv", aq, v_new)
    st_new = st * jnp.exp(gl) + bm("hck,hcv->hkv", kd, v_new)
    return o, st_new


def _gdn_scan_specs(cfg, rev):
    rr, hd, dk, dv = cfg["scan_rows"], cfg["hd"], cfg["dk"], cfg["dv"]
    nb, nblk = rr // DN_CHUNK, cfg["s"] // rr
    ii = (lambda i: nblk - 1 - i) if rev else (lambda i: i)
    hs = lambda d: pl.BlockSpec((hd, rr, d), lambda i: (0, ii(i), 0))
    ins = [hs(dk), pl.BlockSpec((hd, nb, DN_CHUNK, DN_CHUNK), lambda i: (0, ii(i), 0, 0)), hs(dv), hs(dk), hs(dk),
           pl.BlockSpec((hd, nb, 1, LANES), lambda i: (0, ii(i), 0, 0))]
    st_spec = pl.BlockSpec((nb, hd, dk, dv), lambda i: (ii(i), 0, 0, 0))
    return ins, hs(dv), st_spec, nb, nblk


def _gdn_scan_fwd(qd, aq, u, w, kd, gl, cfg, name):
    ins, o_spec, st_spec, nb, nblk = _gdn_scan_specs(cfg, False)
    hd, dk, dv, c = cfg["hd"], cfg["dk"], cfg["dv"], DN_CHUNK

    def body(qd_r, aq_r, u_r, w_r, kd_r, gl_r, o_r, sts_r, st):
        @pl.when(pl.program_id(0) == 0)
        def _():
            st[...] = jnp.zeros(st.shape, F32)

        s = st[...]
        for j in range(nb):
            rows = slice(j * c, (j + 1) * c)
            sts_r[j] = s
            o, s = _gdn_scan_chunk(qd_r[:, rows, :], aq_r[:, j], u_r[:, rows, :], w_r[:, rows, :], kd_r[:, rows, :],
                                   gl_r[:, j], s)
            o_r[:, rows, :] = o
        st[...] = s

    return pl.pallas_call(
        body, name=name, grid=(nblk,), in_specs=ins, out_specs=[o_spec, st_spec],
        out_shape=[jax.ShapeDtypeStruct((hd, cfg["s"], dv), F32),
                   jax.ShapeDtypeStruct((cfg["s"] // c, hd, dk, dv), F32)],
        scratch_shapes=[pltpu.VMEM((hd, dk, dv), F32)], compiler_params=_cp("arbitrary"),
    )(qd, aq, u, w, kd, gl)


def _gdn_scan_bwd(qd, aq, u, w, kd, gl, states, do, cfg, name):
    ins, o_spec, st_spec, nb, nblk = _gdn_scan_specs(cfg, True)
    hd, dk, dv, c = cfg["hd"], cfg["dk"], cfg["dv"], DN_CHUNK

    def body(qd_r, aq_r, u_r, w_r, kd_r, gl_r, sts_r, do_r, dqd, daq, du, dw, dkd, dgl, dst):
        @pl.when(pl.program_id(0) == 0)
        def _():
            dst[...] = jnp.zeros(dst.shape, F32)

        ds = dst[...]
        for j in reversed(range(nb)):
            rows = slice(j * c, (j + 1) * c)
            _, vjp = jax.vjp(_gdn_scan_chunk, qd_r[:, rows, :], aq_r[:, j], u_r[:, rows, :], w_r[:, rows, :],
                             kd_r[:, rows, :], gl_r[:, j], sts_r[j])
            g0, g1, g2, g3, g4, g5, ds = vjp((do_r[:, rows, :], ds))
            dqd[:, rows, :] = g0
            daq[:, j] = g1
            du[:, rows, :] = g2
            dw[:, rows, :] = g3
            dkd[:, rows, :] = g4
            dgl[:, j] = g5
        dst[...] = ds

    s, nch = cfg["s"], cfg["s"] // c
    return pl.pallas_call(
        body, name=name, grid=(nblk,), in_specs=ins + [st_spec, o_spec], out_specs=ins,
        out_shape=[jax.ShapeDtypeStruct((hd, s, dk), F32), jax.ShapeDtypeStruct((hd, nch, c, c), F32),
                   jax.ShapeDtypeStruct((hd, s, dv), F32), jax.ShapeDtypeStruct((hd, s, dk), F32),
                   jax.ShapeDtypeStruct((hd, s, dk), F32), jax.ShapeDtypeStruct((hd, nch, 1, LANES), F32)],
        scratch_shapes=[pltpu.VMEM((hd, dk, dv), F32)], compiler_params=_cp("arbitrary"),
    )(qd, aq, u, w, kd, gl, states, do)


def _loss_head(h, target, w, name):
    s, d = h.shape
    t = _tile(s, 512, 8)

    def body(h_r, t_r, w_r, loss_r, dh_r, dw_r):
        i = pl.program_id(0)
        (y,), vjp = jax.vjp(_rms_fn, h_r[...], w_r[...])
        err = y - t_r[...]
        part = 0.5 * jnp.sum(jnp.mean(err * err, axis=-1, keepdims=True))
        gh, gw = vjp((err * (1.0 / d),))
        dh_r[...] = gh

        @pl.when(i == 0)
        def _():
            loss_r[...] = jnp.zeros(loss_r.shape, F32) + part
            dw_r[...] = gw

        @pl.when(i > 0)
        def _():
            loss_r[...] += part
            dw_r[...] += gw

    row = pl.BlockSpec((t, d), lambda i: (i, 0))
    return pl.pallas_call(
        body, name=name, grid=(s // t,), in_specs=[row, row, pl.BlockSpec((1, d), lambda i: (0, 0))],
        out_specs=[pl.BlockSpec((8, LANES), lambda i: (0, 0)), row, pl.BlockSpec((1, d), lambda i: (0, 0))],
        out_shape=[jax.ShapeDtypeStruct((8, LANES), F32), jax.ShapeDtypeStruct((s, d), F32), jax.ShapeDtypeStruct((1, d), F32)],
        compiler_params=_cp("arbitrary"),
    )(h, target, w)


def _adamw_math(g, w, m, v):
    m = ADAM_B1 * m + (1.0 - ADAM_B1) * g
    v = ADAM_B2 * v + (1.0 - ADAM_B2) * jnp.square(g)
    m_hat = m / (1.0 - ADAM_B1 ** ADAM_STEP)
    v_hat = v / (1.0 - ADAM_B2 ** ADAM_STEP)
    delta = -ADAM_LR * (m_hat / (jnp.sqrt(v_hat) + ADAM_EPS) + ADAM_WD * w)
    return delta, m, v


def _pair_sum(mine, theirs, name):
    _, nch, nl, r, c = mine.shape
    tr = _tile(r, 128, 16)

    def body(a_r, b_r, o_r):
        o_r[...] = (a_r[...].astype(F32) + b_r[...].astype(F32)).astype(o_r.dtype)

    return pl.pallas_call(
        body, name=name, grid=(nch, nl, r // tr),
        in_specs=[pl.BlockSpec((None, None, None, tr, c), lambda p, a, i: (0, p, a, i, 0)),
                  pl.BlockSpec((None, None, tr, c), lambda p, a, i: (p, a, i, 0))],
        out_specs=pl.BlockSpec((None, None, tr, c), lambda p, a, i: (p, a, i, 0)),
        out_shape=jax.ShapeDtypeStruct(theirs.shape, theirs.dtype), compiler_params=_cp("parallel", "parallel", "parallel"),
    )(mine, theirs)


def _adamw_sum(parts, w, m, v, name):
    nl, r, c = w.shape
    n_parts = parts.shape[0]
    tr = _tile(r, 64, 8)

    def body(p_r, w_r, m_r, v_r, g_o, d_o, m_o, v_o):
        g = p_r[0].astype(F32)
        for q in range(1, n_parts):
            g = g + p_r[q].astype(F32)
        delta, mn, vn = _adamw_math(g, w_r[...], m_r[...], v_r[...])
        g_o[...] = g
        d_o[...] = delta
        m_o[...] = mn
        v_o[...] = vn

    blk = pl.BlockSpec((None, tr, c), lambda a, i: (a, i, 0))
    return pl.pallas_call(
        body, name=name, grid=(nl, r // tr),
        in_specs=[pl.BlockSpec((n_parts, None, tr, c), lambda a, i: (0, a, i, 0)), blk, blk, blk],
        out_specs=[blk] * 4, out_shape=[jax.ShapeDtypeStruct(w.shape, F32)] * 4, compiler_params=_cp("parallel", "parallel"),
    )(parts, w, m, v)


def _sum_parts(parts, name):
    _, r, c = parts.shape
    tr = _tile(r, 512, 8)

    def body(p_r, o_r):
        g = p_r[0]
        for q in range(1, N_DEV):
            g = g + p_r[q]
        o_r[...] = g

    return pl.pallas_call(
        body, name=name, grid=(r // tr,), in_specs=[pl.BlockSpec((N_DEV, tr, c), lambda i: (0, i, 0))],
        out_specs=pl.BlockSpec((tr, c), lambda i: (i, 0)), out_shape=jax.ShapeDtypeStruct((r, c), F32),
        compiler_params=_cp("parallel"),
    )(parts)


def _adamw_flat(g, w, m, v, name):
    r, c = w.shape
    tr = _tile(r, 512, 8)

    def body(g_r, w_r, m_r, v_r, d_o, m_o, v_o):
        delta, mn, vn = _adamw_math(g_r[...], w_r[...], m_r[...], v_r[...])
        d_o[...] = delta
        m_o[...] = mn
        v_o[...] = vn

    blk = pl.BlockSpec((tr, c), lambda i: (i, 0))
    return pl.pallas_call(
        body, name=name, grid=(r // tr,), in_specs=[blk] * 4, out_specs=[blk] * 3,
        out_shape=[jax.ShapeDtypeStruct(w.shape, F32)] * 3, compiler_params=_cp("parallel"),
    )(g, w, m, v)


def _remote(src, dst, send_sems, recv_sems, idx, to):
    return pltpu.make_async_remote_copy(src_ref=src, dst_ref=dst, send_sem=send_sems.at[idx], recv_sem=recv_sems.at[idx],
                                        device_id=to, device_id_type=MESH)


def _comm_call(body, name, arrays, out_shapes, n_sems):
    nt = len(arrays)
    return pl.pallas_call(
        body, name=name, in_specs=[pl.BlockSpec(memory_space=pl.ANY)] * nt, out_specs=[pl.BlockSpec(memory_space=pl.ANY)] * nt,
        out_shape=out_shapes,
        scratch_shapes=[pltpu.SemaphoreType.DMA((nt, n_sems)), pltpu.SemaphoreType.DMA((nt, n_sems)),
                        pltpu.SemaphoreType.DMA((nt,))],
    )(*arrays)


def _gather_all(arrays, name):
    nt = len(arrays)

    def body(*refs):
        srcs, dsts = refs[:nt], refs[nt:2 * nt]
        send_sems, recv_sems, local_sems = refs[2 * nt:]
        x, y, c = lax.axis_index("x"), lax.axis_index("y"), lax.axis_index("c")
        slot = lambda px, py, pc: 4 * px + 2 * py + pc
        me, sib = slot(x, y, c), (x, y, 1 - c)
        chips = [(1 - x, y), (x, 1 - y), (1 - x, 1 - y)]
        local = [pltpu.make_async_copy(srcs[t], dsts[t].at[me], local_sems.at[t]) for t in range(nt)]
        for cp in local:
            cp.start()
        sends = []
        for t in range(nt):
            sends.append(_remote(srcs[t], dsts[t].at[me], send_sems, recv_sems, (t, 0), sib))
            for j, (px, py) in enumerate(chips):
                sends.append(_remote(srcs[t], dsts[t].at[me], send_sems, recv_sems, (t, 1 + j), (px, py, c)))
        for cp in sends:
            cp.start()
        for j, (px, py) in enumerate(chips):
            landed = slot(px, py, c)
            for t in range(nt):
                _remote(srcs[t], dsts[t].at[landed], send_sems, recv_sems, (t, 1 + j), (px, py, c)).wait_recv()
                fwd = _remote(dsts[t].at[landed], dsts[t].at[landed], send_sems, recv_sems, (t, 4 + j), sib)
                fwd.start()
                sends.append(fwd)
        for t in range(nt):
            _remote(srcs[t], dsts[t].at[slot(x, y, 1 - c)], send_sems, recv_sems, (t, 0), sib).wait_recv()
            for j, (px, py) in enumerate(chips):
                _remote(srcs[t], dsts[t].at[slot(px, py, 1 - c)], send_sems, recv_sems, (t, 4 + j), sib).wait_recv()
        for cp in sends:
            cp.wait_send()
        for cp in local:
            cp.wait()

    return _comm_call(body, name, arrays, [jax.ShapeDtypeStruct((N_DEV,) + a.shape, a.dtype) for a in arrays], N_DEV - 1)


def _sibling_swap(arrays, name):
    nt = len(arrays)

    def body(*refs):
        srcs, dsts = refs[:nt], refs[nt:2 * nt]
        send_sems, recv_sems, _ = refs[2 * nt:]
        sib = (lax.axis_index("x"), lax.axis_index("y"), 1 - lax.axis_index("c"))
        copies = [_remote(srcs[t].at[1], dsts[t], send_sems, recv_sems, (t, 0), sib) for t in range(nt)]
        for cp in copies:
            cp.start()
        for cp in copies:
            cp.wait()

    return _comm_call(body, name, arrays, [jax.ShapeDtypeStruct(a.shape[1:], a.dtype) for a in arrays], 1)


def _chip_scatter(arrays, name):
    nt = len(arrays)

    def body(*refs):
        srcs, dsts = refs[:nt], refs[nt:2 * nt]
        send_sems, recv_sems, local_sems = refs[2 * nt:]
        x, y, c = lax.axis_index("x"), lax.axis_index("y"), lax.axis_index("c")
        mine = 2 * x + y
        local = [pltpu.make_async_copy(srcs[t].at[mine], dsts[t].at[mine], local_sems.at[t]) for t in range(nt)]
        for cp in local:
            cp.start()
        sends, arrivals = [], []
        for j, (px, py) in enumerate([(1 - x, y), (x, 1 - y), (1 - x, 1 - y)]):
            theirs = 2 * px + py
            for t in range(nt):
                sends.append(_remote(srcs[t].at[theirs], dsts[t].at[mine], send_sems, recv_sems, (t, j), (px, py, c)))
                arrivals.append(_remote(srcs[t].at[theirs], dsts[t].at[theirs], send_sems, recv_sems, (t, j), (px, py, c)))
        for cp in sends:
            cp.start()
        for cp in arrivals:
            cp.wait_recv()
        for cp in sends:
            cp.wait_send()
        for cp in local:
            cp.wait()

    return _comm_call(body, name, arrays, [jax.ShapeDtypeStruct(a.shape, a.dtype) for a in arrays], N_CHIP - 1)


def _config(x, ffn1_w_out, ssm_conv_b, ssm_dt_bias, ssm_norm, dn_conv_w, dn_dt_bias, dn_norm, dn_w_branch):
    s, d = x.shape[-2], x.shape[-1]
    f = ffn1_w_out.shape[1] * N_DEV
    cs, hs, inner = ssm_conv_b.shape[1], ssm_dt_bias.shape[1], ssm_norm.shape[1]
    gn = (cs - inner) // 2
    hd, dv = dn_dt_bias.shape[1], dn_norm.shape[1]
    cd = dn_conv_w.shape[2] * N_DEV
    vd = hd * dv
    kd = (cd - vd) // 2
    cfg = dict(s=s, d=d, f=f, cs=cs, hs=hs, inner=inner, gn=gn, n=gn // SSM_GROUPS, hg=hs // SSM_GROUPS, p=inner // hs,
               hp=inner // SSM_GROUPS, hd=hd, dv=dv, dk=kd // hd, cd=cd, vd=vd, kd=kd)
    offs, o = {}, 0
    for nm, wd in (("xbc", cs), ("qkv", cd), ("gates", 2 * d), ("zs", inner), ("zd", vd), ("small", SMALL_W)):
        offs[nm] = o
        o += wd
    cfg["offs"], cfg["pw"] = offs, o
    cfg["cw"] = 512
    cfg["pw_main"] = offs["small"]
    cfg["prep_rows"] = min(s, 8 * DN_CHUNK)
    cfg["scan_rows"] = min(s, 4 * DN_CHUNK)
    cfg["in_split"] = (inner, cs, hs, cd, vd, hd, hd, d, d)
    assert hs + 2 * hd <= SMALL_R and cfg["dk"] == dv and dv == LANES and LANES % cfg["p"] == 0 and cfg["hp"] % LANES == 0
    assert offs["qkv"] % cfg["cw"] == 0 and offs["gates"] % (2 * d) == 0 and offs["zs"] % cfg["hp"] == 0
    assert offs["zd"] % dv == 0 and all(wd % cfg["cw"] == 0 for wd in (inner, gn, kd, vd)) and inner % cfg["n"] == 0
    return cfg


def _permute_w_in(w, cfg):
    pts = [0]
    for wd in cfg["in_split"]:
        pts.append(pts[-1] + wd)
    z_s, xbc, dt, qkv, z_d, b_d, a_d, g_s, g_d = [w[:, pts[i]:pts[i + 1]] for i in range(9)]
    pad = jnp.zeros((w.shape[0], SMALL_W - dt.shape[1] - b_d.shape[1] - a_d.shape[1]), w.dtype)
    return jnp.concatenate([xbc, qkv, g_s, g_d, z_s, z_d, dt, b_d, a_d, pad], axis=1)


def _unpermute_w_in(g, cfg):
    o, d = cfg["offs"], cfg["d"]
    hs, hd = cfg["hs"], cfg["hd"]
    sm = g[:, o["small"]:]
    return jnp.concatenate([
        g[:, o["zs"]:o["zs"] + cfg["inner"]], g[:, o["xbc"]:o["xbc"] + cfg["cs"]], sm[:, :hs],
        g[:, o["qkv"]:o["qkv"] + cfg["cd"]], g[:, o["zd"]:o["zd"] + cfg["vd"]], sm[:, hs:hs + hd], sm[:, hs + hd:hs + 2 * hd],
        g[:, o["gates"]:o["gates"] + d], g[:, o["gates"] + d:o["gates"] + 2 * d]], axis=1)


def _lane_row(v, off):
    return jnp.pad(v.astype(F32), (off, SMALL_R - off - v.shape[0]))[None]


def _pack(arrs):
    flat = []
    for a in arrs:
        v = a.reshape(-1)
        flat.append(jnp.pad(v, (0, (-v.shape[0]) % LANES)))
    v = jnp.concatenate(flat)
    v = jnp.pad(v, (0, (-v.shape[0]) % (8 * LANES)))
    return v.reshape(-1, LANES)


def _unpack(packed, shapes):
    v, out, o = packed.reshape(-1), [], 0
    for sh in shapes:
        n = math.prod(sh)
        out.append(v[o:o + n].reshape(sh))
        o += n + (-n) % LANES
    return out


def _cols_gathered(g):
    nd, nl, r, c = g.shape
    return jnp.transpose(g, (1, 2, 0, 3)).reshape(nl, r, nd * c)


def _rows_gathered(g):
    nd, nl, r, c = g.shape
    return jnp.transpose(g, (1, 0, 2, 3)).reshape(nl, nd * r, c)


def _scatter_layout(g, cols, core):
    nl = g.shape[0]
    if cols:
        r, c = g.shape[1], g.shape[2] // N_DEV
        t = jnp.transpose(g.reshape(nl, r, N_CHIP, 2, c), (3, 2, 0, 1, 4))
    else:
        r, c = g.shape[1] // N_DEV, g.shape[2]
        t = jnp.transpose(g.reshape(nl, N_CHIP, 2, r, c), (2, 1, 0, 3, 4))
    return jnp.where(core == 0, t, t[::-1]).astype(BF16)


def _ffn_fwd(h, nw, w_in, w_out, cfg, tag):
    s, d, f = cfg["s"], cfg["d"], cfg["f"]
    t = _tile(s, 512, 8)
    xn, = _rw_fwd(_rms_fn, f"{tag}_norm", (1, s // t), [(h, _rows(t, d), None), (nw, _par(d), None)],
                  [(jax.ShapeDtypeStruct((s, d), BF16), _rows(t, d), None)])
    gu = _matmul(xn, w_in, mode="nn", name=f"{tag}_in", out_dtype=BF16)
    t2 = _tile(s, 256, 8)
    act, = _rw_fwd(_swiglu_fn, f"{tag}_act", (1, s // t2), [(gu, _rows(t2, 2 * f), [f, f])],
                   [(jax.ShapeDtypeStruct((s, f), BF16), _rows(t2, f), None)])
    h_out = _matmul(act, w_out, mode="nn", name=f"{tag}_out", res=h, scale=0.5)
    return h_out, dict(h=h, xn=xn, gu=gu, act=act)


def _ffn_bwd(gh, r, nw, w_in, w_out, cfg, tag):
    s, d, f = cfg["s"], cfg["d"], cfg["f"]
    d_wout = _matmul(r["act"], gh, mode="tn", name=f"{tag}_dwout", scale=0.5)
    d_act = _matmul(gh, w_out, mode="nt", name=f"{tag}_dact", scale=0.5, out_dtype=BF16)
    t2 = _tile(s, 256, 8)
    d_gu, = _rw_bwd(_swiglu_fn, f"{tag}_dgu", (1, s // t2), [(r["gu"], _rows(t2, 2 * f), [f, f])],
                    [(d_act, _rows(t2, f), None)], [(0, jax.ShapeDtypeStruct((s, 2 * f), BF16), _rows(t2, 2 * f), "tile")])
    d_win = _matmul(r["xn"], d_gu, mode="tn", name=f"{tag}_dwin")
    d_xn = _matmul(d_gu, w_in, mode="nt", name=f"{tag}_dxn", out_dtype=BF16)
    t = _tile(s, 512, 8)
    d_h, d_nw = _rw_bwd(_rms_fn, f"{tag}_dnorm", (1, s // t), [(r["h"], _rows(t, d), None), (nw, _par(d), None)],
                        [(d_xn, _rows(t, d), None)],
                        [(0, jax.ShapeDtypeStruct((s, d), F32), _rows(t, d), "tile"),
                         (1, jax.ShapeDtypeStruct((1, d), F32), _par(d), "acc_all")],
                        add=(gh, _rows(t, d), 0))
    return d_h, (d_nw, d_win, d_wout)


def _mix_fwd(h, p, cfg, tag):
    s, d, o = cfg["s"], cfg["d"], cfg["offs"]
    t = _tile(s, 512, 8)
    u, = _rw_fwd(_rms_fn, f"{tag}_norm", (1, s // t), [(h, _rows(t, d), None), (p["mix_norm"], _par(d), None)],
                 [(jax.ShapeDtypeStruct((s, d), BF16), _rows(t, d), None)])
    proj = _matmul(u, p["w_in"], mode="nn", name=f"{tag}_in", out_dtype=BF16)
    small = _matmul(u, p["w_small"], mode="nn", name=f"{tag}_insmall")
    cw, tr = cfg["cw"], _tile(s, 512, 8)
    xbc_c = _conv_fwd(proj, o["xbc"] // cw, p["ssm_conv_w"], p["ssm_conv_b"], name=f"{tag}_sconv", cw=cw, tr=tr)
    qkv_c = _conv_fwd(proj, o["qkv"] // cw, p["dn_conv_w"], jnp.zeros((1, cfg["cd"]), F32), name=f"{tag}_dconv", cw=cw, tr=tr)
    y, s_states = _ssd_fwd(xbc_c, small, p["ssm_dtb"], p["ssm_alog"], p["ssm_dsk"], cfg, f"{tag}_ssd")
    hp, inner, g = cfg["hp"], cfg["inner"], SSM_GROUPS
    t4 = _tile(s, 512, 8)
    zs_blk = o["zs"] // hp
    y_s, = _rw_fwd(_ssm_out_fn, f"{tag}_sout", (g, s // t4),
                   [(y, _rows(t4, hp, lambda j: j), None), (proj, _rows(t4, hp, lambda j: zs_blk + j), None),
                    (p["ssm_norm"], _par(hp, True), None)],
                   [(jax.ShapeDtypeStruct((s, inner), BF16), _rows(t4, hp, lambda j: j), None)])
    uu, ww, qd, kd, aq, gl = _gdn_prep_fwd(qkv_c, small, p["dn_alog"], p["dn_dtb"], cfg, f"{tag}_prep")
    o_dn, d_states = _gdn_scan_fwd(qd, aq, uu, ww, kd, gl, cfg, f"{tag}_scan")
    hd, dv = cfg["hd"], cfg["dv"]
    zd_blk = o["zd"] // dv
    o_spec = pl.BlockSpec((None, t4, dv), lambda j, i: (j, i, 0))
    y_d, = _rw_fwd(_dn_out_fn, f"{tag}_dout", (hd, s // t4),
                   [(o_dn, o_spec, None), (proj, _rows(t4, dv, lambda j: zd_blk + j), None), (p["dn_norm"], _par(dv), None)],
                   [(jax.ShapeDtypeStruct((s, cfg["vd"]), BF16), _rows(t4, dv, lambda j: j), None)])
    ps = _matmul(y_s, p["ssm_w_branch"], mode="nn", name=f"{tag}_sbr")
    pd = _matmul(y_d, p["dn_w_branch"], mode="nn", name=f"{tag}_dbr")
    t6 = _tile(s, 256, 8)
    merged, = _rw_fwd(_merge_fn, f"{tag}_merge", (1, s // t6),
                      [(proj, _rows(t6, 2 * d, o["gates"] // (2 * d)), [d, d]), (ps, _rows(t6, d), None), (pd, _rows(t6, d), None)],
                      [(jax.ShapeDtypeStruct((s, d), BF16), _rows(t6, d), None)])
    h_out = _matmul(merged, p["w_out"], mode="nn", name=f"{tag}_out", res=h)
    res = dict(h=h, u=u, proj=proj, small=small, xbc_c=xbc_c, qkv_c=qkv_c, y=y, s_states=s_states, y_s=y_s, uu=uu, ww=ww, qd=qd, kd=kd,
               aq=aq, gl=gl, o_dn=o_dn, d_states=d_states, y_d=y_d, ps=ps, pd=pd, merged=merged)
    return h_out, res


def _mix_bwd(gh, r, p, cfg, tag):
    s, d, o = cfg["s"], cfg["d"], cfg["offs"]
    cw, tr = cfg["cw"], _tile(s, 512, 8)
    hp, inner, g, hd, dv, dk = cfg["hp"], cfg["inner"], SSM_GROUPS, cfg["hd"], cfg["dv"], cfg["dk"]
    proj = r["proj"]
    grads = {}
    grads["w_out"] = _matmul(r["merged"], gh, mode="tn", name=f"{tag}_dwout")
    d_merged = _matmul(gh, p["w_out"], mode="nt", name=f"{tag}_dmerged", out_dtype=BF16)
    dproj = jax.ShapeDtypeStruct((s, cfg["pw_main"]), BF16)
    t6 = _tile(s, 256, 8)
    gates_spec = _rows(t6, 2 * d, o["gates"] // (2 * d))
    dproj, d_ps, d_pd = _rw_bwd(
        _merge_fn, f"{tag}_dmerge", (1, s // t6),
        [(proj, gates_spec, [d, d]), (r["ps"], _rows(t6, d), None), (r["pd"], _rows(t6, d), None)],
        [(d_merged, _rows(t6, d), None)],
        [(0, dproj, gates_spec, "tile"), (1, jax.ShapeDtypeStruct((s, d), BF16), _rows(t6, d), "tile"),
         (2, jax.ShapeDtypeStruct((s, d), BF16), _rows(t6, d), "tile")])
    grads["ssm_w_branch"] = _matmul(r["y_s"], d_ps, mode="tn", name=f"{tag}_dwsbr")
    grads["dn_w_branch"] = _matmul(r["y_d"], d_pd, mode="tn", name=f"{tag}_dwdbr")
    d_ys = _matmul(d_ps, p["ssm_w_branch"], mode="nt", name=f"{tag}_dys", out_dtype=BF16)
    d_yd = _matmul(d_pd, p["dn_w_branch"], mode="nt", name=f"{tag}_dyd", out_dtype=BF16)
    t4 = _tile(s, 512, 8)
    zs_blk, zd_blk = o["zs"] // hp, o["zd"] // dv
    zs_spec = _rows(t4, hp, lambda j: zs_blk + j)
    d_y, dproj, grads["ssm_norm"] = _rw_bwd(
        _ssm_out_fn, f"{tag}_dsout", (g, s // t4),
        [(r["y"], _rows(t4, hp, lambda j: j), None), (proj, zs_spec, None), (p["ssm_norm"], _par(hp, True), None)],
        [(d_ys, _rows(t4, hp, lambda j: j), None)],
        [(0, jax.ShapeDtypeStruct((s, inner), F32), _rows(t4, hp, lambda j: j), "tile"),
         (1, jax.ShapeDtypeStruct(dproj.shape, BF16), zs_spec, "tile"),
         (2, jax.ShapeDtypeStruct((1, inner), F32), _par(hp, True), "acc_row")],
        alias=(dproj, 1))
    o_spec = pl.BlockSpec((None, t4, dv), lambda j, i: (j, i, 0))
    zd_spec = _rows(t4, dv, lambda j: zd_blk + j)
    d_o, dproj, grads["dn_norm"] = _rw_bwd(
        _dn_out_fn, f"{tag}_ddout", (hd, s // t4),
        [(r["o_dn"], o_spec, None), (proj, zd_spec, None), (p["dn_norm"], _par(dv), None)],
        [(d_yd, _rows(t4, dv, lambda j: j), None)],
        [(0, jax.ShapeDtypeStruct((hd, s, dv), F32), o_spec, "tile"),
         (1, jax.ShapeDtypeStruct(dproj.shape, BF16), zd_spec, "tile"),
         (2, jax.ShapeDtypeStruct((1, dv), F32), _par(dv), "acc_all")],
        alias=(dproj, 1))
    d_xs, d_bm, d_cm, dsm_s, g_dtb, g_alog, g_dsk = _ssd_bwd(
        r["xbc_c"], r["small"], p["ssm_dtb"], p["ssm_alog"], p["ssm_dsk"], r["s_states"], d_y, cfg, f"{tag}_dssd")
    grads["ssm_dt_bias"], grads["ssm_a_log"], grads["ssm_d"] = (v[0, :cfg["hs"]] for v in (g_dtb, g_alog, g_dsk))
    dws, dbs, col = [], [], 0
    for nm, dy in (("xs", d_xs), ("bm", d_bm), ("cm", d_cm)):
        wd = dy.shape[1]
        dproj, dw_, db_ = _conv_bwd(proj, (o["xbc"] + col) // cw, p["ssm_conv_w"][:, col:col + wd],
                                    p["ssm_conv_b"][:, col:col + wd], dy, dproj, (o["xbc"] + col) // cw,
                                    name=f"{tag}_dsconv_{nm}", cw=cw, tr=tr)
        dws.append(dw_)
        dbs.append(db_)
        col += wd
    grads["ssm_conv_w"] = jnp.concatenate(dws, axis=1)
    grads["ssm_conv_b"] = jnp.concatenate(dbs, axis=1)[0]
    cts = _gdn_scan_bwd(r["qd"], r["aq"], r["uu"], r["ww"], r["kd"], r["gl"], r["d_states"], d_o, cfg, f"{tag}_dscan")
    d_qd, d_aq, d_uu, d_ww, d_kd, d_gl = cts
    d_q, d_k, d_v, dsm_d, g_alog_d, g_dtb_d = _gdn_prep_bwd(
        r["qkv_c"], r["small"], p["dn_alog"], p["dn_dtb"], (d_uu, d_ww, d_qd, d_kd, d_aq, d_gl), cfg, f"{tag}_dprep")
    a0 = cfg["hs"] + hd
    grads["dn_a_log"], grads["dn_dt_bias"] = g_alog_d[0, a0:a0 + hd], g_dtb_d[0, a0:a0 + hd]
    dws, col = [], 0
    zero_b = jnp.zeros((1, cfg["cd"]), F32)
    for nm, dy in (("q", d_q), ("k", d_k), ("v", d_v)):
        wd = dy.shape[1]
        dproj, dw_, _ = _conv_bwd(proj, (o["qkv"] + col) // cw, p["dn_conv_w"][:, col:col + wd], zero_b[:, col:col + wd], dy,
                                  dproj, (o["qkv"] + col) // cw, name=f"{tag}_ddconv_{nm}", cw=cw, tr=tr)
        dws.append(dw_)
        col += wd
    grads["dn_conv_w"] = jnp.concatenate(dws, axis=1)
    d_small = _dsmall(dsm_s, dsm_d, f"{tag}_dsmall")
    grads["w_in"] = jnp.concatenate([_matmul(r["u"], dproj, mode="tn", name=f"{tag}_dwin"),
                                     _matmul(r["u"], d_small, mode="tn", name=f"{tag}_dwinsmall")], axis=1)
    d_u = _matmul(d_small, p["w_small"], mode="nt", name=f"{tag}_dusmall")
    d_u = _matmul(dproj, p["w_in"], mode="nt", name=f"{tag}_du", res=d_u, out_dtype=BF16)
    t = _tile(s, 512, 8)
    d_h, grads["mix_norm"] = _rw_bwd(
        _rms_fn, f"{tag}_dnorm", (1, s // t), [(r["h"], _rows(t, d), None), (p["mix_norm"], _par(d), None)],
        [(d_u, _rows(t, d), None)],
        [(0, jax.ShapeDtypeStruct((s, d), F32), _rows(t, d), "tile"), (1, jax.ShapeDtypeStruct((1, d), F32), _par(d), "acc_all")],
        add=(gh, _rows(t, d), 0))
    return d_h, grads


_BIG = ("ffn1_w_in", "ffn1_w_out", "w_in", "ssm_w_branch", "dn_w_branch", "w_out", "ffn2_w_in", "ffn2_w_out")
_COL_SHARDED = ("ffn1_w_in", "w_in", "ffn2_w_in")
_CONV = ("ssm_conv_w", "dn_conv_w")
_NAMES = ("ffn1_norm", "ffn1_w_in", "ffn1_w_out", "mix_norm", "w_in", "ssm_conv_w", "ssm_conv_b", "ssm_dt_bias", "ssm_a_log",
          "ssm_d", "ssm_norm", "ssm_w_branch", "dn_conv_w", "dn_dt_bias", "dn_a_log", "dn_norm", "dn_w_branch", "w_out",
          "ffn2_norm", "ffn2_w_in", "ffn2_w_out", "final_norm")


def kernel(x, ffn1_norm, ffn1_w_in, ffn1_w_out, mix_norm, w_in, ssm_conv_w, ssm_conv_b, ssm_dt_bias, ssm_a_log, ssm_d, ssm_norm, ssm_w_branch, dn_conv_w, dn_dt_bias, dn_a_log, dn_norm, dn_w_branch, w_out, ffn2_norm, ffn2_w_in, ffn2_w_out, final_norm, loss_target, m_ffn1_norm, m_ffn1_w_in, m_ffn1_w_out, m_mix_norm, m_w_in, m_ssm_conv_w, m_ssm_conv_b, m_ssm_dt_bias, m_ssm_a_log, m_ssm_d, m_ssm_norm, m_ssm_w_branch, m_dn_conv_w, m_dn_dt_bias, m_dn_a_log, m_dn_norm, m_dn_w_branch, m_w_out, m_ffn2_norm, m_ffn2_w_in, m_ffn2_w_out, m_final_norm, v_ffn1_norm, v_ffn1_w_in, v_ffn1_w_out, v_mix_norm, v_w_in, v_ssm_conv_w, v_ssm_conv_b, v_ssm_dt_bias, v_ssm_a_log, v_ssm_d, v_ssm_norm, v_ssm_w_branch, v_dn_conv_w, v_dn_dt_bias, v_dn_a_log, v_dn_norm, v_dn_w_branch, v_w_out, v_ffn2_norm, v_ffn2_w_in, v_ffn2_w_out, v_final_norm):
    args = locals()
    w = {n: args[n] for n in _NAMES}
    mom = {n: args["m_" + n] for n in _NAMES}
    var = {n: args["v_" + n] for n in _NAMES}
    cfg = _config(x, ffn1_w_out, ssm_conv_b, ssm_dt_bias, ssm_norm, dn_conv_w, dn_dt_bias, dn_norm, dn_w_branch)
    depth, s, d = ffn1_norm.shape[0], cfg["s"], cfg["d"]
    me = 4 * lax.axis_index("x") + 2 * lax.axis_index("y") + lax.axis_index("c")

    gathered = _gather_all([w[n].astype(BF16) for n in _BIG] + [w[n] for n in _CONV], "gather_weights")
    full = {}
    for n, g in zip(_BIG + _CONV, gathered):
        full[n] = _cols_gathered(g) if (n in _COL_SHARDED or n in _CONV) else _rows_gathered(g)

    hs, hd = cfg["hs"], cfg["hd"]
    layers = []
    for l in range(depth):
        w_perm = _permute_w_in(full["w_in"][l], cfg)
        layers.append(dict(
            ffn1_norm=ffn1_norm[l][None], ffn1_w_in=full["ffn1_w_in"][l], ffn1_w_out=full["ffn1_w_out"][l],
            mix_norm=mix_norm[l][None], w_in=w_perm[:, :cfg["pw_main"]], w_small=w_perm[:, cfg["pw_main"]:],
            ssm_conv_w=full["ssm_conv_w"][l], ssm_conv_b=ssm_conv_b[l][None],
            ssm_dtb=_lane_row(ssm_dt_bias[l], 0), ssm_alog=_lane_row(ssm_a_log[l], 0), ssm_dsk=_lane_row(ssm_d[l], 0),
            ssm_norm=ssm_norm[l][None], ssm_w_branch=full["ssm_w_branch"][l], dn_conv_w=full["dn_conv_w"][l],
            dn_dtb=_lane_row(dn_dt_bias[l], hs + hd), dn_alog=_lane_row(dn_a_log[l], hs + hd), dn_norm=dn_norm[l][None],
            dn_w_branch=full["dn_w_branch"][l], w_out=full["w_out"][l],
            ffn2_norm=ffn2_norm[l][None], ffn2_w_in=full["ffn2_w_in"][l], ffn2_w_out=full["ffn2_w_out"][l]))

    h = x.reshape(s, d)
    saved = []
    for l, p in enumerate(layers):
        h, r1 = _ffn_fwd(h, p["ffn1_norm"], p["ffn1_w_in"], p["ffn1_w_out"], cfg, f"l{l}_ffn1")
        h, rm = _mix_fwd(h, p, cfg, f"l{l}_mix")
        h, r2 = _ffn_fwd(h, p["ffn2_norm"], p["ffn2_w_in"], p["ffn2_w_out"], cfg, f"l{l}_ffn2")
        saved.append((r1, rm, r2))
    loss_blk, gh, g_final = _loss_head(h, loss_target.reshape(s, d), final_norm[None], "loss_head")
    loss = lax.psum(loss_blk[0, 0], ("x", "y", "c"))

    lg = [None] * depth
    for l in reversed(range(depth)):
        p, (r1, rm, r2) = layers[l], saved[l]
        gh, (g_n2, g_win2, g_wout2) = _ffn_bwd(gh, r2, p["ffn2_norm"], p["ffn2_w_in"], p["ffn2_w_out"], cfg, f"l{l}_ffn2")
        gh, gm = _mix_bwd(gh, rm, p, cfg, f"l{l}_mix")
        gh, (g_n1, g_win1, g_wout1) = _ffn_bwd(gh, r1, p["ffn1_norm"], p["ffn1_w_in"], p["ffn1_w_out"], cfg, f"l{l}_ffn1")
        gm["w_in"] = _unpermute_w_in(gm["w_in"], cfg)
        gm.update(ffn1_norm=g_n1[0], ffn1_w_in=g_win1, ffn1_w_out=g_wout1, ffn2_norm=g_n2[0], ffn2_w_in=g_win2,
                  ffn2_w_out=g_wout2, mix_norm=gm["mix_norm"][0], ssm_norm=gm["ssm_norm"][0], dn_norm=gm["dn_norm"][0])
        lg[l] = gm
    grad_x = gh.reshape(x.shape)
    local = {n: jnp.stack([lg[l][n] for l in range(depth)]) for n in _NAMES if n != "final_norm"}
    local["final_norm"] = g_final[0]

    shares = [_scatter_layout(local[n], n in _COL_SHARDED, lax.axis_index("c")) for n in _BIG]
    from_sibling = _sibling_swap(shares, "swap_grads")
    chip_sums = [_pair_sum(a, b, f"pair_sum_{n}") for n, a, b in zip(_BIG, shares, from_sibling)]
    parts = _chip_scatter(chip_sums, "scatter_grads")
    out_g, out_d, out_m, out_v = {}, {}, {}, {}
    for n, pt in zip(_BIG, parts):
        out_g[n], out_d[n], out_m[n], out_v[n] = _adamw_sum(pt, w[n], mom[n], var[n], f"adamw_{n}")

    small = [n for n in _NAMES if n not in _BIG]
    packed, = _gather_all([_pack([local[n] for n in small])], "gather_small_grads")
    total = _unpack(_sum_parts(packed, "sum_small_grads"), [local[n].shape for n in small])
    for n, g in zip(small, total):
        if n in _CONV:
            c = w[n].shape[2]
            g = lax.dynamic_slice_in_dim(g, me * c, c, axis=2)
        out_g[n] = g
    shapes = [w[n].shape for n in small]
    upd = _adamw_flat(_pack([out_g[n] for n in small]), _pack([w[n] for n in small]), _pack([mom[n] for n in small]),
                      _pack([var[n] for n in small]), "adamw_small")
    for dst, pk in zip((out_d, out_m, out_v), upd):
        for n, a in zip(small, _unpack(pk, shapes)):
            dst[n] = a

    return (loss, grad_x, *[out_g[n] for n in _NAMES], *[out_d[n] for n in _NAMES], *[out_m[n] for n in _NAMES],
            *[out_v[n] for n in _NAMES])
```
